```python
import jax, jax.numpy as jnp
from jax import lax
import numpy as np

D_MODEL = 1024
BATCH = 4
SEQ = 8192
DEPTH = 2

CHUNK = 64
N_MEM = 256
EPS = 1e-6

MEM_HEADS = 4
MEM_HEAD_DIM = 64
MEM_WIDTH = MEM_HEADS * MEM_HEAD_DIM
MAIN_WIDTH = D_MODEL - MEM_WIDTH

GLA_HEADS = 4
GLA_KEY_WIDTH = MAIN_WIDTH // 2
GLA_DK = GLA_KEY_WIDTH // GLA_HEADS
GLA_DV = MAIN_WIDTH // GLA_HEADS
GLA_GATE_RANK = 16
GLA_GATE_TAU = 16.0

FOX_HEADS = 12
FOX_HEAD_DIM = MAIN_WIDTH // FOX_HEADS
FOX_QBLOCK = 128

D_FF = 2816
N_EXPERTS = 8
TOP_K = 2

N_A_LAYERS = DEPTH // 2
N_B_LAYERS = DEPTH - N_A_LAYERS
N_DENSE = (DEPTH + 1) // 2
N_MOE = DEPTH // 2

A_IN_WIDTH = 2 * GLA_KEY_WIDTH + 2 * MAIN_WIDTH + GLA_GATE_RANK + MEM_WIDTH
B_IN_WIDTH = MAIN_WIDTH + MEM_WIDTH
KV_WIDTH = 2 * MAIN_WIDTH + FOX_HEADS

kernel_name = 'hybrid_gla_fox_yoco_moe_encoder'


def rmsnorm(x, g):
    xf = x.astype(jnp.float32)
    y = xf * lax.rsqrt(jnp.mean(xf * xf, axis=-1, keepdims=True) + EPS)
    return (y * g.astype(jnp.float32)).astype(x.dtype)


def split_cols(t, widths):
    out, start = [], 0
    for w in widths:
        out.append(t[..., start:start + w])
        start += w
    return out


def swiglu(h, w_gate, w_up, w_down):
    return (jax.nn.silu(h @ w_gate) * (h @ w_up)) @ w_down


def gla_scan(q, k, v, log_a):
    bsz, seq, heads, dk = q.shape
    dv = v.shape[-1]
    nc = seq // CHUNK

    def to_chunks(t):
        return t.astype(jnp.float32).reshape(bsz, nc, CHUNK, heads, t.shape[-1]).transpose(1, 0, 3, 2, 4)

    qc, kc, vc = to_chunks(q), to_chunks(k), to_chunks(v)
    bc = jnp.cumsum(to_chunks(log_a), axis=-2)
    causal = jnp.tril(jnp.ones((CHUNK, CHUNK), dtype=bool))

    def step(state, inp):
        q_, k_, v_, b_ = inp
        o_inter = jnp.einsum('bhtk,bhkv->bhtv', q_ * jnp.exp(b_), state)
        diff = b_[:, :, :, None, :] - b_[:, :, None, :, :]
        decay = jnp.exp(jnp.where(causal[:, :, None], diff, -jnp.inf))
        scores = jnp.sum(q_[:, :, :, None, :] * k_[:, :, None, :, :] * decay, axis=-1)
        o_intra = jnp.einsum('bhts,bhsv->bhtv', scores, v_)
        b_last = b_[:, :, -1:, :]
        state = state * jnp.exp(b_last[:, :, 0, :, None]) + jnp.einsum(
            'bhsk,bhsv->bhkv', k_ * jnp.exp(b_last - b_), v_)
        return state, o_inter + o_intra

    state0 = jnp.zeros((bsz, heads, dk, dv), jnp.float32)
    _, o = lax.scan(step, state0, (qc, kc, vc, bc))
    return o.transpose(1, 0, 3, 2, 4).reshape(bsz, seq, heads, dv).astype(v.dtype)


def fox_attention(q, k, v, c):
    bsz, seq, heads, dh = q.shape
    nb = seq // FOX_QBLOCK
    qb = q.reshape(bsz, nb, FOX_QBLOCK, heads, dh).transpose(1, 0, 3, 2, 4)
    cb = c.reshape(bsz, heads, nb, FOX_QBLOCK).transpose(2, 0, 1, 3)
    kpos = jnp.arange(seq)
    scale = dh ** -0.5

    def block(args):
        i, q_, c_ = args
        s = jnp.einsum('bhqd,bshd->bhqs', q_, k).astype(jnp.float32) * scale
        s = s + c_[..., None] - c[:, :, None, :]
        qpos = i * FOX_QBLOCK + jnp.arange(FOX_QBLOCK)
        s = jnp.where(kpos[None, :] <= qpos[:, None], s, -jnp.inf)
        p = jax.nn.softmax(s, axis=-1).astype(v.dtype)
        return jnp.einsum('bhqs,bshd->bhqd', p, v)

    o = lax.map(block, (jnp.arange(nb), qb, cb))
    return o.transpose(1, 0, 3, 2, 4).reshape(bsz, seq, heads * dh)


def mem_attention(q, mem_n, w_mem_kv):
    bsz, seq, _ = q.shape
    k, v = split_cols(mem_n @ w_mem_kv, (MEM_WIDTH, MEM_WIDTH))
    k = k.reshape(bsz, -1, MEM_HEADS, MEM_HEAD_DIM)
    v = v.reshape(bsz, -1, MEM_HEADS, MEM_HEAD_DIM)
    qh = q.reshape(bsz, seq, MEM_HEADS, MEM_HEAD_DIM)
    s = jnp.einsum('bshd,bmhd->bhsm', qh, k).astype(jnp.float32) * (MEM_HEAD_DIM ** -0.5)
    p = jax.nn.softmax(s, axis=-1).astype(v.dtype)
    return jnp.einsum('bhsm,bmhd->bshd', p, v).reshape(bsz, seq, MEM_WIDTH)


def moe_swiglu(h, w_router, w_gate, w_up, w_down):
    logits = (h @ w_router).astype(jnp.float32)
    top_val, top_idx = lax.top_k(logits, TOP_K)
    top_w = jax.nn.softmax(top_val, axis=-1)
    gates = jnp.sum(jax.nn.one_hot(top_idx, N_EXPERTS, dtype=jnp.float32) * top_w[..., None],
                    axis=-2).astype(h.dtype)
    y = jnp.zeros_like(h)
    for e in range(N_EXPERTS):
        y = y + gates[..., e:e + 1] * swiglu(h, w_gate[e], w_up[e], w_down[e])
    return y


def setup_inputs(seed: int = 0) -> dict:
    key = jax.random.key(seed)
    ks = jax.random.split(key, 24)
    f32 = jnp.float32

    def nrm(k, shape, fan_in):
        return jax.random.normal(k, shape, f32) * (fan_in ** -0.5)

    def gain(k, shape):
        return 1.0 + 0.02 * jax.random.normal(k, shape, f32)

    return {
        'x': jax.random.normal(ks[0], (BATCH, SEQ, D_MODEL), f32),
        'mem': jax.random.normal(ks[1], (BATCH, N_MEM, D_MODEL), f32),
        'norm_mix': gain(ks[2], (DEPTH, D_MODEL)),
        'norm_mem': gain(ks[3], (DEPTH, D_MODEL)),
        'norm_ffn': gain(ks[4], (DEPTH, D_MODEL)),
        'norm_kv': gain(ks[5], (D_MODEL,)),
        'norm_final': gain(ks[6], (D_MODEL,)),
        'w_in_a': nrm(ks[7], (N_A_LAYERS, D_MODEL, A_IN_WIDTH), D_MODEL),
        'w_gla_gate2': nrm(ks[8], (N_A_LAYERS, GLA_GATE_RANK, GLA_KEY_WIDTH), GLA_GATE_RANK),
        'b_gla_gate2': 0.01 * jax.random.normal(ks[9], (N_A_LAYERS, GLA_KEY_WIDTH), f32),
        'gla_onorm': gain(ks[10], (N_A_LAYERS, GLA_DV)),
        'w_in_b': nrm(ks[11], (N_B_LAYERS, D_MODEL, B_IN_WIDTH), D_MODEL),
        'w_kv': nrm(ks[12], (D_MODEL, KV_WIDTH), D_MODEL),
        'b_forget': 3.0 + 0.5 * jax.random.normal(ks[13], (FOX_HEADS,), f32),
        'w_mem_kv': nrm(ks[14], (DEPTH, D_MODEL, 2 * MEM_WIDTH), D_MODEL),
        'w_out': nrm(ks[15], (DEPTH, MAIN_WIDTH + MEM_WIDTH, D_MODEL), MAIN_WIDTH + MEM_WIDTH),
        'w_ff_gate': nrm(ks[16], (N_DENSE, D_MODEL, D_FF), D_MODEL),
        'w_ff_up': nrm(ks[17], (N_DENSE, D_MODEL, D_FF), D_MODEL),
        'w_ff_down': nrm(ks[18], (N_DENSE, D_FF, D_MODEL), D_FF),
        'w_router': nrm(ks[19], (N_MOE, D_MODEL, N_EXPERTS), D_MODEL),
        'w_moe_gate': nrm(ks[20], (N_MOE, N_EXPERTS, D_MODEL, D_FF), D_MODEL),
        'w_moe_up': nrm(ks[21], (N_MOE, N_EXPERTS, D_MODEL, D_FF), D_MODEL),
        'w_moe_down': nrm(ks[22], (N_MOE, N_EXPERTS, D_FF, D_MODEL), D_FF),
    }


def reference(x, mem, norm_mix, norm_mem, norm_ffn, norm_kv, norm_final, w_in_a, w_gla_gate2,
              b_gla_gate2, gla_onorm, w_in_b, w_kv, b_forget, w_mem_kv, w_out, w_ff_gate, w_ff_up,
              w_ff_down, w_router, w_moe_gate, w_moe_up, w_moe_down):
    bsz, seq, _ = x.shape
    k_sh = v_sh = c_sh = None
    for layer in range(DEPTH):
        if layer == N_A_LAYERS:
            kvp = rmsnorm(x, norm_kv) @ w_kv
            k_sh, v_sh, f_logit = split_cols(kvp, (MAIN_WIDTH, MAIN_WIDTH, FOX_HEADS))
            k_sh = k_sh.reshape(bsz, seq, FOX_HEADS, FOX_HEAD_DIM)
            v_sh = v_sh.reshape(bsz, seq, FOX_HEADS, FOX_HEAD_DIM)
            log_f = jax.nn.log_sigmoid((f_logit + b_forget).astype(jnp.float32))
            c_sh = jnp.cumsum(log_f, axis=1).transpose(0, 2, 1)

        hn = rmsnorm(x, norm_mix[layer])
        mem_n = rmsnorm(mem, norm_mem[layer])
        if layer < N_A_LAYERS:
            i = layer
            q, k, v, r, glr, mq = split_cols(
                hn @ w_in_a[i],
                (GLA_KEY_WIDTH, GLA_KEY_WIDTH, MAIN_WIDTH, MAIN_WIDTH, GLA_GATE_RANK, MEM_WIDTH))
            log_a = jax.nn.log_sigmoid(
                (glr @ w_gla_gate2[i] + b_gla_gate2[i]).astype(jnp.float32)) / GLA_GATE_TAU
            o = gla_scan(
                q.reshape(bsz, seq, GLA_HEADS, GLA_DK) * (GLA_DK ** -0.5),
                k.reshape(bsz, seq, GLA_HEADS, GLA_DK),
                v.reshape(bsz, seq, GLA_HEADS, GLA_DV),
                log_a.reshape(bsz, seq, GLA_HEADS, GLA_DK))
            o = rmsnorm(o, gla_onorm[i]) * jax.nn.silu(r.reshape(bsz, seq, GLA_HEADS, GLA_DV))
            main = o.reshape(bsz, seq, MAIN_WIDTH)
        else:
            j = layer - N_A_LAYERS
            qf, mq = split_cols(hn @ w_in_b[j], (MAIN_WIDTH, MEM_WIDTH))
            main = fox_attention(qf.reshape(bsz, seq, FOX_HEADS, FOX_HEAD_DIM), k_sh, v_sh, c_sh)
        mixed = jnp.concatenate([main, mem_attention(mq, mem_n, w_mem_kv[layer])], axis=-1)
        x = x + mixed @ w_out[layer]

        hn = rmsnorm(x, norm_ffn[layer])
        f = layer // 2
        if layer % 2 == 0:
            x = x + swiglu(hn, w_ff_gate[f], w_ff_up[f], w_ff_down[f])
        else:
            x = x + moe_swiglu(hn, w_router[f], w_moe_gate[f], w_moe_up[f], w_moe_down[f])
    return rmsnorm(x, norm_final)
```

```python
import functools

import jax
import jax.numpy as jnp
from jax import lax
from jax.experimental import pallas as pl
from jax.experimental.pallas import tpu as pltpu

D_MODEL = 1024
EPS = 1e-6
GLA_CHUNK = 64
GLA_SUB = 16
MEM_HEADS = 4
MEM_HEAD_DIM = 64
MEM_WIDTH = MEM_HEADS * MEM_HEAD_DIM
MAIN_WIDTH = D_MODEL - MEM_WIDTH
GLA_HEADS = 4
GLA_DK = MAIN_WIDTH // 2 // GLA_HEADS
GLA_DV = MAIN_WIDTH // GLA_HEADS
GLA_DK_PAD = 128
GLA_DV_PAD = 256
GLA_GATE_RANK = 16
GLA_GATE_TAU = 16.0
FOX_HEADS = 12
FOX_HEAD_DIM = 64
D_FF = 2816
N_EXPERTS = 8
LANES = 128
FF_CHUNK = 256
VMEM_LIMIT = 56 * 1024 * 1024

BF16 = jnp.bfloat16
F32 = jnp.float32


def _params(n_axes, vmem=VMEM_LIMIT):
    return pltpu.CompilerParams(dimension_semantics=("arbitrary",) * n_axes, vmem_limit_bytes=vmem)


def _rms_normed(x, gain):
    ms = jnp.mean(x * x, axis=-1, keepdims=True)
    return x * lax.rsqrt(ms + EPS) * gain


def _log_sigmoid(z):
    return jnp.minimum(z, 0.0) - jnp.log(1.0 + jnp.exp(-jnp.abs(z)))


def _silu(z):
    return z / (1.0 + jnp.exp(-z))


def _dot(a, b):
    return jnp.dot(a, b, preferred_element_type=F32)


def _dot_nt(a, b):
    return lax.dot_general(a, b, (((1,), (1,)), ((), ())), preferred_element_type=F32)


def _dot_tn(a, b):
    return lax.dot_general(a, b, (((0,), (0,)), ((), ())), preferred_element_type=F32)


def _tril_ones(n):
    r = lax.broadcasted_iota(jnp.int32, (n, n), 0)
    c = lax.broadcasted_iota(jnp.int32, (n, n), 1)
    return (c <= r).astype(F32)


def _full(shape):
    return pl.BlockSpec(shape, lambda *_: (0,) * len(shape))


def _in_a_kernel(x_ref, g_ref, wq_ref, wk_ref, wv_ref, wr_ref, wmq_ref, wg1_ref, wg2_ref, bg2_ref,
                 q_ref, k_ref, v_ref, r_ref, mq_ref, la_ref):
    hn = _rms_normed(x_ref[...], g_ref[...]).astype(BF16)
    q_ref[...] = (_dot(hn, wq_ref[...]) * (GLA_DK ** -0.5)).astype(BF16)
    k_ref[...] = _dot(hn, wk_ref[...]).astype(BF16)
    v_ref[...] = _dot(hn, wv_ref[...]).astype(BF16)
    r_ref[...] = _dot(hn, wr_ref[...]).astype(BF16)
    mq_ref[...] = _dot(hn, wmq_ref[...]).astype(BF16)
    glr = _dot(hn, wg1_ref[...])
    z = jnp.dot(glr, wg2_ref[...], preferred_element_type=F32, precision=lax.Precision.HIGHEST)
    la_ref[...] = _log_sigmoid(z + bg2_ref[...]) * (1.0 / GLA_GATE_TAU)


def _in_a(x, gain, wq, wk, wv, wr, wmq, wg1, wg2, bg2, tm):
    t = x.shape[0]
    row = lambda n: pl.BlockSpec((tm, n), lambda i: (i, 0))
    kw, vw = GLA_HEADS * GLA_DK_PAD, GLA_HEADS * GLA_DV_PAD
    return pl.pallas_call(
        _in_a_kernel,
        grid=(t // tm,),
        in_specs=[row(D_MODEL), _full((1, D_MODEL)), _full(wq.shape), _full(wk.shape), _full(wv.shape),
                  _full(wr.shape), _full(wmq.shape), _full(wg1.shape), _full(wg2.shape), _full(bg2.shape)],
        out_specs=[row(kw), row(kw), row(vw), row(vw), row(MEM_WIDTH), row(kw)],
        out_shape=[jax.ShapeDtypeStruct((t, kw), BF16), jax.ShapeDtypeStruct((t, kw), BF16),
                   jax.ShapeDtypeStruct((t, vw), BF16), jax.ShapeDtypeStruct((t, vw), BF16),
                   jax.ShapeDtypeStruct((t, MEM_WIDTH), BF16), jax.ShapeDtypeStruct((t, kw), F32)],
        compiler_params=_params(1),
        name="in_proj_gla",
    )(x, gain, wq, wk, wv, wr, wmq, wg1, wg2, bg2)


def _norm_proj_kernel(n_out, scales, x_ref, g_ref, *refs):
    w_refs, o_refs = refs[:n_out], refs[n_out:]
    hn = _rms_normed(x_ref[...], g_ref[...]).astype(BF16)
    for w_ref, o_ref, s in zip(w_refs, o_refs, scales):
        o = _dot(hn, w_ref[...])
        if s != 1.0:
            o = o * s
        o_ref[...] = o.astype(o_ref.dtype)


def _norm_proj(x, gain, weights, scales, tm, name):
    t = x.shape[0]
    row = lambda n: pl.BlockSpec((tm, n), lambda i: (i, 0))
    return pl.pallas_call(
        functools.partial(_norm_proj_kernel, len(weights), scales),
        grid=(t // tm,),
        in_specs=[row(D_MODEL), _full((1, D_MODEL))] + [_full(w.shape) for w in weights],
        out_specs=[row(w.shape[1]) for w in weights],
        out_shape=[jax.ShapeDtypeStruct((t, w.shape[1]), BF16) for w in weights],
        compiler_params=_params(1),
        name=name,
    )(x, gain, *weights)


def _kv_kernel(tiles_per_seq, x_ref, g_ref, wk_ref, wv_ref, wf_ref, bf_ref,
               k_ref, v_ref, c_ref, ct_ref, carry_ref):
    @pl.when(pl.program_id(0) % tiles_per_seq == 0)
    def _():
        carry_ref[...] = jnp.zeros_like(carry_ref)

    hn = _rms_normed(x_ref[...], g_ref[...]).astype(BF16)
    k_ref[...] = _dot(hn, wk_ref[...]).astype(BF16)
    v_ref[...] = _dot(hn, wv_ref[...]).astype(BF16)
    log_f = _log_sigmoid(_dot(hn, wf_ref[...]) + bf_ref[...])
    tm = log_f.shape[0]
    c = carry_ref[...] + jnp.dot(_tril_ones(tm), log_f, preferred_element_type=F32,
                                 precision=lax.Precision.HIGHEST)
    carry_ref[...] = c[tm - 1:tm, :]
    c_ref[...] = c
    ct_ref[...] = c.T[:ct_ref.shape[0], :]


def _kv_proj(x, gain, wk, wv, wf, bf, bsz, seq, tm):
    t = x.shape[0]
    tiles_per_seq = seq // tm
    row = lambda n: pl.BlockSpec((tm, n), lambda i: (i, 0))
    n_rows = 16
    return pl.pallas_call(
        functools.partial(_kv_kernel, tiles_per_seq),
        grid=(t // tm,),
        in_specs=[row(D_MODEL), _full((1, D_MODEL)), _full(wk.shape), _full(wv.shape), _full(wf.shape),
                  _full(bf.shape)],
        out_specs=[row(MAIN_WIDTH), row(MAIN_WIDTH), row(LANES),
                   pl.BlockSpec((None, n_rows, tm), lambda i: (i // tiles_per_seq, 0, i % tiles_per_seq))],
        out_shape=[jax.ShapeDtypeStruct((t, MAIN_WIDTH), BF16), jax.ShapeDtypeStruct((t, MAIN_WIDTH), BF16),
                   jax.ShapeDtypeStruct((t, LANES), F32), jax.ShapeDtypeStruct((bsz, n_rows, seq), F32)],
        scratch_shapes=[pltpu.VMEM((1, LANES), F32)],
        compiler_params=_params(1),
        name="kv_proj_fox",
    )(x, gain, wk, wv, wf, bf)


def _gla_kernel(chunks, q_ref, k_ref, v_ref, r_ref, la_ref, on_ref, o_ref, st_ref):
    @pl.when(pl.program_id(1) == 0)
    def _():
        st_ref[...] = jnp.zeros_like(st_ref)

    tril = _tril_ones(GLA_CHUNK)
    n_sub = GLA_CHUNK // GLA_SUB
    masks = []
    for i in range(n_sub):
        n_keys = GLA_SUB * (i + 1)
        rr = lax.broadcasted_iota(jnp.int32, (GLA_SUB, n_keys), 0) + GLA_SUB * i
        cc = lax.broadcasted_iota(jnp.int32, (GLA_SUB, n_keys), 1)
        masks.append(cc <= rr)

    def chunk_body(c, carry):
        r0 = pl.multiple_of(c * GLA_CHUNK, GLA_CHUNK)
        rows = pl.ds(r0, GLA_CHUNK)
        b_all = jnp.dot(tril, la_ref[rows, :], preferred_element_type=F32, precision=lax.Precision.HIGHEST)
        for h in range(GLA_HEADS):
            ks = slice(h * GLA_DK_PAD, (h + 1) * GLA_DK_PAD)
            vs = slice(h * GLA_DV_PAD, (h + 1) * GLA_DV_PAD)
            q = q_ref[rows, ks].astype(F32)
            k = k_ref[rows, ks].astype(F32)
            v = v_ref[rows, vs]
            b = b_all[:, ks]
            b_last = b[GLA_CHUNK - 1:GLA_CHUNK, :]
            st = st_ref[h]
            o_inter = _dot_nt((q * jnp.exp(b)).astype(BF16), st.astype(BF16))
            o_parts = []
            for i in range(n_sub):
                lo, hi = GLA_SUB * i, GLA_SUB * (i + 1)
                b_i = b[lo:hi, :]
                qd = q[lo:hi, :] * jnp.exp(b_i - b[lo - 1:lo, :]) if i else q[lo:hi, :] * jnp.exp(b_i)
                kd = k[:hi, :] * jnp.exp(b[lo - 1:lo, :] - b[:hi, :]) if i else k[:hi, :] * jnp.exp(-b[:hi, :])
                a = jnp.where(masks[i], _dot_nt(qd.astype(BF16), kd.astype(BF16)), 0.0)
                o_parts.append(_dot(a.astype(BF16), v[:hi, :]))
            o = o_inter + jnp.concatenate(o_parts, axis=0)
            kdec = (k * jnp.exp(b_last - b)).astype(BF16)
            st_ref[h] = st * jnp.exp(b_last) + _dot_tn(v, kdec)
            ms = jnp.sum(o * o, axis=-1, keepdims=True) * (1.0 / GLA_DV)
            y = o * lax.rsqrt(ms + EPS) * on_ref[...] * _silu(r_ref[rows, vs].astype(F32))
            o_ref[rows, vs] = y.astype(BF16)
        return carry

    lax.fori_loop(0, chunks, chunk_body, 0)


def _gla(q, k, v, r, la, onorm, bsz, seq, tm):
    t = q.shape[0]
    tiles_per_seq = seq // tm
    kw, vw = GLA_HEADS * GLA_DK_PAD, GLA_HEADS * GLA_DV_PAD
    row = lambda n: pl.BlockSpec((tm, n), lambda b, i: (b * tiles_per_seq + i, 0))
    return pl.pallas_call(
        functools.partial(_gla_kernel, tm // GLA_CHUNK),
        grid=(bsz, tiles_per_seq),
        in_specs=[row(kw), row(kw), row(vw), row(vw), row(kw), pl.BlockSpec((1, GLA_DV_PAD), lambda b, i: (0, 0))],
        out_specs=row(vw),
        out_shape=jax.ShapeDtypeStruct((t, vw), BF16),
        scratch_shapes=[pltpu.VMEM((GLA_HEADS, GLA_DV_PAD, GLA_DK_PAD), F32)],
        compiler_params=_params(2),
        name="gla_scan",
    )(q, k, v, r, la, onorm)


def _fox_kernel(tq, q_ref, k_ref, v_ref, c_ref, ct_ref, o_ref):
    hp, i = pl.program_id(1), pl.program_id(2)
    lane = lax.broadcasted_iota(jnp.int32, (tq, LANES), 1)
    q = q_ref[...]
    c_tile = c_ref[...]
    rr = lax.broadcasted_iota(jnp.int32, (tq, tq), 0)
    cc = lax.broadcasted_iota(jnp.int32, (tq, tq), 1)
    causal = cc <= rr
    outs = []
    for hh in range(2):
        h = 2 * hp + hh
        own = (lane // FOX_HEAD_DIM) == hh
        qh = jnp.where(own, q, jnp.zeros_like(q))
        cq = jnp.sum(jnp.where(lane == h, c_tile, 0.0), axis=-1, keepdims=True)

        def step(j, carry, masked):
            m, l, acc = carry
            cols = pl.ds(pl.multiple_of(j * tq, tq), tq)
            s = _dot_nt(qh, k_ref[cols, :]) + (cq - ct_ref[pl.ds(h, 1), cols])
            if masked:
                s = jnp.where(causal, s, -jnp.inf)
            m_new = jnp.maximum(m, jnp.max(s, axis=-1, keepdims=True))
            alpha = jnp.exp(m - m_new)
            p = jnp.exp(s - m_new)
            l = alpha * l + jnp.sum(p, axis=-1, keepdims=True)
            acc = alpha * acc + _dot(p.astype(BF16), v_ref[cols, :])
            return m_new, l, acc

        init = (jnp.full((tq, 1), -jnp.inf, F32), jnp.zeros((tq, 1), F32), jnp.zeros((tq, LANES), F32))
        carry = lax.fori_loop(0, i, functools.partial(step, masked=False), init)
        m, l, acc = step(i, carry, masked=True)
        outs.append(acc / l)
    o_ref[...] = jnp.where((lane // FOX_HEAD_DIM) == 0, outs[0], outs[1]).astype(BF16)


def _fox(q, k, v, c, ct, bsz, seq, tq):
    t = q.shape[0]
    nq = seq // tq
    pairs = FOX_HEADS // 2
    return pl.pallas_call(
        functools.partial(_fox_kernel, tq),
        grid=(bsz, pairs, nq),
        in_specs=[pl.BlockSpec((tq, LANES), lambda b, p, i: (b * nq + i, p)),
                  pl.BlockSpec((seq, LANES), lambda b, p, i: (b, p)),
                  pl.BlockSpec((seq, LANES), lambda b, p, i: (b, p)),
                  pl.BlockSpec((tq, LANES), lambda b, p, i: (b * nq + i, 0)),
                  pl.BlockSpec((None, ct.shape[1], seq), lambda b, p, i: (b, 0, 0))],
        out_specs=pl.BlockSpec((tq, LANES), lambda b, p, i: (b * nq + i, p)),
        out_shape=jax.ShapeDtypeStruct((t, MAIN_WIDTH), BF16),
        compiler_params=_params(3),
        name="fox_attention",
    )(q, k, v, c, ct)


def _out_kernel(x_ref, main_ref, mq_ref, mk_ref, mv_ref, wo_main_ref, wo_mem_ref, o_ref):
    mq, mk, mv = mq_ref[...], mk_ref[...], mv_ref[...]
    heads = []
    for h in range(MEM_HEADS):
        hs = slice(h * MEM_HEAD_DIM, (h + 1) * MEM_HEAD_DIM)
        s = _dot_nt(mq[:, hs], mk[:, hs]) * (MEM_HEAD_DIM ** -0.5)
        p = jnp.exp(s - jnp.max(s, axis=-1, keepdims=True))
        l = jnp.sum(p, axis=-1, keepdims=True)
        heads.append(_dot(p.astype(BF16), mv[:, hs]) / l)
    mem_o = jnp.concatenate(heads, axis=-1).astype(BF16)
    o_ref[...] = x_ref[...] + _dot(main_ref[...], wo_main_ref[...]) + _dot(mem_o, wo_mem_ref[...])


def _out_proj(x, main, mq, mem_k, mem_v, wo_main, wo_mem, seq, tm):
    t = x.shape[0]
    tiles_per_seq = seq // tm
    n_mem = mem_k.shape[1]
    row = lambda n: pl.BlockSpec((tm, n), lambda i: (i, 0))
    mem = pl.BlockSpec((None, n_mem, MEM_WIDTH), lambda i: (i // tiles_per_seq, 0, 0))
    return pl.pallas_call(
        _out_kernel,
        grid=(t // tm,),
        in_specs=[row(D_MODEL), row(main.shape[1]), row(MEM_WIDTH), mem, mem, _full(wo_main.shape),
                  _full(wo_mem.shape)],
        out_specs=row(D_MODEL),
        out_shape=jax.ShapeDtypeStruct((t, D_MODEL), F32),
        compiler_params=_params(1),
        name="out_proj_mem_attn",
    )(x, main, mq, mem_k, mem_v, wo_main, wo_mem)


def _swiglu_acc(hn, wg_ref, wu_ref, wd_ref, acc_ref, row_scale=None):
    for j in range(D_FF // FF_CHUNK):
        cs = slice(j * FF_CHUNK, (j + 1) * FF_CHUNK)
        a = _silu(_dot(hn, wg_ref[:, cs])) * _dot(hn, wu_ref[:, cs])
        if row_scale is not None:
            a = a * row_scale
        acc_ref[...] += _dot(a.astype(BF16), wd_ref[cs, :])


def _ffn_kernel(x_ref, g_ref, wg_ref, wu_ref, wd_ref, o_ref):
    x = x_ref[...]
    o_ref[...] = x
    _swiglu_acc(_rms_normed(x, g_ref[...]).astype(BF16), wg_ref, wu_ref, wd_ref, o_ref)


def _ffn(x, gain, wg, wu, wd, tm):
    t = x.shape[0]
    row = pl.BlockSpec((tm, D_MODEL), lambda i: (i, 0))
    return pl.pallas_call(
        _ffn_kernel,
        grid=(t // tm,),
        in_specs=[row, _full((1, D_MODEL)), _full(wg.shape), _full(wu.shape), _full(wd.shape)],
        out_specs=row,
        out_shape=jax.ShapeDtypeStruct((t, D_MODEL), F32),
        compiler_params=_params(1),
        name="dense_swiglu",
    )(x, gain, wg, wu, wd)


def _router_kernel(x_ref, g_ref, wr_ref, hn_ref, gates_ref):
    hn = _rms_normed(x_ref[...], g_ref[...])
    hn_ref[...] = hn.astype(BF16)
    logits = jnp.dot(hn, wr_ref[...], preferred_element_type=F32, precision=lax.Precision.HIGHEST)
    lane = lax.broadcasted_iota(jnp.int32, logits.shape, 1)
    logits = jnp.where(lane < N_EXPERTS, logits, -jnp.inf)
    m1 = jnp.max(logits, axis=-1, keepdims=True)
    i1 = jnp.min(jnp.where(logits == m1, lane, LANES), axis=-1, keepdims=True)
    rest = jnp.where(lane == i1, -jnp.inf, logits)
    m2 = jnp.max(rest, axis=-1, keepdims=True)
    i2 = jnp.min(jnp.where(rest == m2, lane, LANES), axis=-1, keepdims=True)
    e2 = jnp.exp(m2 - m1)
    w1 = 1.0 / (1.0 + e2)
    gates_ref[...] = jnp.where(lane == i1, w1, 0.0) + jnp.where(lane == i2, e2 * w1, 0.0)


def _router(x, gain, wr, tm):
    t = x.shape[0]
    row = lambda n: pl.BlockSpec((tm, n), lambda i: (i, 0))
    return pl.pallas_call(
        _router_kernel,
        grid=(t // tm,),
        in_specs=[row(D_MODEL), _full((1, D_MODEL)), _full(wr.shape)],
        out_specs=[row(D_MODEL), row(LANES)],
        out_shape=[jax.ShapeDtypeStruct((t, D_MODEL), BF16), jax.ShapeDtypeStruct((t, LANES), F32)],
        compiler_params=_params(1),
        name="moe_router",
    )(x, gain, wr)


def _moe_kernel(x_ref, hn_ref, gates_ref, gf_ref, wg_ref, wu_ref, wd_ref, o_ref, acc_ref):
    e = pl.program_id(1)

    @pl.when(e == 0)
    def _():
        acc_ref[...] = x_ref[...]

    gates = gates_ref[...]
    lane = lax.broadcasted_iota(jnp.int32, gates.shape, 1)
    gate = jnp.sum(jnp.where(lane == e, gates, 0.0), axis=-1, keepdims=True)
    _swiglu_acc(hn_ref[...], wg_ref, wu_ref, wd_ref, acc_ref, row_scale=gate)

    @pl.when(e == N_EXPERTS - 1)
    def _():
        o_ref[...] = _rms_normed(acc_ref[...], gf_ref[...])


def _moe(x, hn, gates, gain_final, wg, wu, wd, tm):
    t = x.shape[0]
    row = lambda n: pl.BlockSpec((tm, n), lambda i, e: (i, 0))
    return pl.pallas_call(
        _moe_kernel,
        grid=(t // tm, N_EXPERTS),
        in_specs=[row(D_MODEL), row(D_MODEL), row(LANES), pl.BlockSpec((1, D_MODEL), lambda i, e: (0, 0)),
                  pl.BlockSpec((None, D_MODEL, D_FF), lambda i, e: (e, 0, 0)),
                  pl.BlockSpec((None, D_MODEL, D_FF), lambda i, e: (e, 0, 0)),
                  pl.BlockSpec((None, D_FF, D_MODEL), lambda i, e: (e, 0, 0))],
        out_specs=row(D_MODEL),
        out_shape=jax.ShapeDtypeStruct((t, D_MODEL), F32),
        scratch_shapes=[pltpu.VMEM((tm, D_MODEL), F32)],
        compiler_params=_params(2),
        name="moe_experts",
    )(x, hn, gates, gain_final, wg, wu, wd)


def _pad_heads(w, heads, width, padded):
    lead = w.shape[:-1]
    w = w.reshape(lead + (heads, width))
    w = jnp.pad(w, [(0, 0)] * len(lead) + [(0, 0), (0, padded - width)])
    return w.reshape(lead + (heads * padded,))


def _pad_cols(w, n):
    return jnp.pad(w, [(0, 0)] * (w.ndim - 1) + [(0, n - w.shape[-1])])


def kernel(x, mem, norm_mix, norm_mem, norm_ffn, norm_kv, norm_final, w_in_a, w_gla_gate2, b_gla_gate2,
           gla_onorm, w_in_b, w_kv, b_forget, w_mem_kv, w_out, w_ff_gate, w_ff_up, w_ff_down, w_router,
           w_moe_gate, w_moe_up, w_moe_down):
    bsz, seq, _ = x.shape
    n_mem = mem.shape[1]
    t = bsz * seq
    tm = 512
    xt = x.reshape(t, D_MODEL)
    memt = mem.reshape(bsz * n_mem, D_MODEL)
    gain = lambda g: g.reshape(1, D_MODEL)
    kw = GLA_HEADS * GLA_DK

    def mem_kv(layer):
        wk, wv = w_mem_kv[layer, :, :MEM_WIDTH], w_mem_kv[layer, :, MEM_WIDTH:]
        mk, mv = _norm_proj(memt, gain(norm_mem[layer]), [wk.astype(BF16), wv.astype(BF16)], (1.0, 1.0),
                            n_mem, "mem_kv_proj")
        return mk.reshape(bsz, n_mem, MEM_WIDTH), mv.reshape(bsz, n_mem, MEM_WIDTH)

    wa = w_in_a[0]
    wq = _pad_heads(wa[:, :kw], GLA_HEADS, GLA_DK, GLA_DK_PAD).astype(BF16)
    wk = _pad_heads(wa[:, kw:2 * kw], GLA_HEADS, GLA_DK, GLA_DK_PAD).astype(BF16)
    off = 2 * kw
    wv = _pad_heads(wa[:, off:off + MAIN_WIDTH], GLA_HEADS, GLA_DV, GLA_DV_PAD).astype(BF16)
    off += MAIN_WIDTH
    wr = _pad_heads(wa[:, off:off + MAIN_WIDTH], GLA_HEADS, GLA_DV, GLA_DV_PAD).astype(BF16)
    off += MAIN_WIDTH
    wg1 = _pad_cols(wa[:, off:off + GLA_GATE_RANK], LANES).astype(BF16)
    off += GLA_GATE_RANK
    wmq = wa[:, off:off + MEM_WIDTH].astype(BF16)
    wg2 = jnp.pad(_pad_heads(w_gla_gate2[0], GLA_HEADS, GLA_DK, GLA_DK_PAD), ((0, LANES - GLA_GATE_RANK), (0, 0)))
    bg2 = _pad_heads(b_gla_gate2[0], GLA_HEADS, GLA_DK, GLA_DK_PAD).reshape(1, -1)
    onorm = _pad_cols(gla_onorm[0], GLA_DV_PAD).reshape(1, GLA_DV_PAD)

    q, k, v, r, mq, la = _in_a(xt, gain(norm_mix[0]), wq, wk, wv, wr, wmq, wg1, wg2, bg2, tm)
    main = _gla(q, k, v, r, la, onorm, bsz, seq, tm)
    mk, mv = mem_kv(0)
    wo = w_out[0]
    wo_main = jnp.pad(wo[:MAIN_WIDTH].reshape(GLA_HEADS, GLA_DV, D_MODEL),
                      ((0, 0), (0, GLA_DV_PAD - GLA_DV), (0, 0))).reshape(GLA_HEADS * GLA_DV_PAD, D_MODEL)
    xt = _out_proj(xt, main, mq, mk, mv, wo_main.astype(BF16), wo[MAIN_WIDTH:].astype(BF16), seq, tm)
    xt = _ffn(xt, gain(norm_ffn[0]), w_ff_gate[0].astype(BF16), w_ff_up[0].astype(BF16),
              w_ff_down[0].astype(BF16), tm)

    wf = _pad_cols(w_kv[:, 2 * MAIN_WIDTH:], LANES).astype(BF16)
    bf = _pad_cols(b_forget, LANES).reshape(1, LANES)
    k_sh, v_sh, c, ct = _kv_proj(xt, gain(norm_kv), w_kv[:, :MAIN_WIDTH].astype(BF16),
                                 w_kv[:, MAIN_WIDTH:2 * MAIN_WIDTH].astype(BF16), wf, bf, bsz, seq, tm)

    wb = w_in_b[0]
    qf, mq = _norm_proj(xt, gain(norm_mix[1]), [wb[:, :MAIN_WIDTH].astype(BF16), wb[:, MAIN_WIDTH:].astype(BF16)],
                        (FOX_HEAD_DIM ** -0.5, 1.0), tm, "in_proj_fox")
    main = _fox(qf, k_sh, v_sh, c, ct, bsz, seq, 256)
    mk, mv = mem_kv(1)
    wo = w_out[1]
    xt = _out_proj(xt, main, mq, mk, mv, wo[:MAIN_WIDTH].astype(BF16), wo[MAIN_WIDTH:].astype(BF16), seq, tm)
    hn, gates = _router(xt, gain(norm_ffn[1]), _pad_cols(w_router[0], LANES), tm)
    out = _moe(xt, hn, gates, gain(norm_final), w_moe_gate[0].astype(BF16), w_moe_up[0].astype(BF16),
               w_moe_down[0].astype(BF16), tm)
    return out.reshape(bsz, seq, D_MODEL)
```

```python
import functools

import jax
import jax.numpy as jnp
from jax import lax
from jax.experimental import pallas as pl
from jax.experimental.pallas import tpu as pltpu

D_MODEL = 1024
EPS = 1e-6
GLA_CHUNK = 64
GLA_SUB = 16
MEM_HEADS = 4
MEM_HEAD_DIM = 64
MEM_WIDTH = MEM_HEADS * MEM_HEAD_DIM
MAIN_WIDTH = D_MODEL - MEM_WIDTH
GLA_HEADS = 4
GLA_DK = MAIN_WIDTH // 2 // GLA_HEADS
GLA_DV = MAIN_WIDTH // GLA_HEADS
GLA_DK_PAD = 128
GLA_DV_PAD = 256
GLA_GATE_RANK = 16
GLA_GATE_TAU = 16.0
FOX_HEADS = 12
FOX_HEAD_DIM = 64
D_FF = 2816
N_EXPERTS = 8
LANES = 128
FF_CHUNK = 256
VMEM_LIMIT = 56 * 1024 * 1024

BF16 = jnp.bfloat16
F32 = jnp.float32


def _params(n_axes, vmem=VMEM_LIMIT):
    return pltpu.CompilerParams(dimension_semantics=("arbitrary",) * n_axes, vmem_limit_bytes=vmem)


def _rms_normed(x, gain):
    ms = jnp.mean(x * x, axis=-1, keepdims=True)
    return x * lax.rsqrt(ms + EPS) * gain


def _log_sigmoid(z):
    return jnp.minimum(z, 0.0) - jnp.log(1.0 + jnp.exp(-jnp.abs(z)))


def _silu(z):
    return z / (1.0 + jnp.exp(-z))


def _dot(a, b):
    return jnp.dot(a, b, preferred_element_type=F32)


def _dot_nt(a, b):
    return lax.dot_general(a, b, (((1,), (1,)), ((), ())), preferred_element_type=F32)


def _dot_tn(a, b):
    return lax.dot_general(a, b, (((0,), (0,)), ((), ())), preferred_element_type=F32)


def _tril_ones(n):
    r = lax.broadcasted_iota(jnp.int32, (n, n), 0)
    c = lax.broadcasted_iota(jnp.int32, (n, n), 1)
    return (c <= r).astype(F32)


def _full(shape):
    return pl.BlockSpec(shape, lambda *_: (0,) * len(shape))


def _in_a_kernel(x_ref, g_ref, wq_ref, wk_ref, wv_ref, wr_ref, wmq_ref, wg1_ref, wg2_ref, bg2_ref,
                 q_ref, k_ref, v_ref, r_ref, mq_ref, la_ref):
    hn = _rms_normed(x_ref[...], g_ref[...]).astype(BF16)
    q_ref[...] = (_dot(hn, wq_ref[...]) * (GLA_DK ** -0.5)).astype(BF16)
    k_ref[...] = _dot(hn, wk_ref[...]).astype(BF16)
    v_ref[...] = _dot(hn, wv_ref[...]).astype(BF16)
    r_ref[...] = _dot(hn, wr_ref[...]).astype(BF16)
    mq_ref[...] = _dot(hn, wmq_ref[...]).astype(BF16)
    glr = _dot(hn, wg1_ref[...])
    z = jnp.dot(glr, wg2_ref[...], preferred_element_type=F32, precision=lax.Precision.HIGHEST)
    la_ref[...] = _log_sigmoid(z + bg2_ref[...]) * (1.0 / GLA_GATE_TAU)


def _in_a(x, gain, wq, wk, wv, wr, wmq, wg1, wg2, bg2, tm):
    t = x.shape[0]
    row = lambda n: pl.BlockSpec((tm, n), lambda i: (i, 0))
    kw, vw = GLA_HEADS * GLA_DK_PAD, GLA_HEADS * GLA_DV_PAD
    return pl.pallas_call(
        _in_a_kernel,
        grid=(t // tm,),
        in_specs=[row(D_MODEL), _full((1, D_MODEL)), _full(wq.shape), _full(wk.shape), _full(wv.shape),
                  _full(wr.shape), _full(wmq.shape), _full(wg1.shape), _full(wg2.shape), _full(bg2.shape)],
        out_specs=[row(kw), row(kw), row(vw), row(vw), row(MEM_WIDTH), row(kw)],
        out_shape=[jax.ShapeDtypeStruct((t, kw), BF16), jax.ShapeDtypeStruct((t, kw), BF16),
                   jax.ShapeDtypeStruct((t, vw), BF16), jax.ShapeDtypeStruct((t, vw), BF16),
                   jax.ShapeDtypeStruct((t, MEM_WIDTH), BF16), jax.ShapeDtypeStruct((t, kw), F32)],
        compiler_params=_params(1),
        name="in_proj_gla",
    )(x, gain, wq, wk, wv, wr, wmq, wg1, wg2, bg2)


def _norm_proj_kernel(n_out, scales, x_ref, g_ref, *refs):
    w_refs, o_refs = refs[:n_out], refs[n_out:]
    hn = _rms_normed(x_ref[...], g_ref[...]).astype(BF16)
    for w_ref, o_ref, s in zip(w_refs, o_refs, scales):
        o = _dot(hn, w_ref[...])
        if s != 1.0:
            o = o * s
        o_ref[...] = o.astype(o_ref.dtype)


def _norm_proj(x, gain, weights, scales, tm, name):
    t = x.shape[0]
    row = lambda n: pl.BlockSpec((tm, n), lambda i: (i, 0))
    return pl.pallas_call(
        functools.partial(_norm_proj_kernel, len(weights), scales),
        grid=(t // tm,),
        in_specs=[row(D_MODEL), _full((1, D_MODEL))] + [_full(w.shape) for w in weights],
        out_specs=[row(w.shape[1]) for w in weights],
        out_shape=[jax.ShapeDtypeStruct((t, w.shape[1]), BF16) for w in weights],
        compiler_params=_params(1),
        name=name,
    )(x, gain, *weights)


def _kv_kernel(tiles_per_seq, x_ref, g_ref, wk_ref, wv_ref, wf_ref, bf_ref,
               k_ref, v_ref, c_ref, ct_ref, carry_ref):
    @pl.when(pl.program_id(0) % tiles_per_seq == 0)
    def _():
        carry_ref[...] = jnp.zeros_like(carry_ref)

    hn = _rms_normed(x_ref[...], g_ref[...]).astype(BF16)
    k_ref[...] = _dot(hn, wk_ref[...]).astype(BF16)
    v_ref[...] = _dot(hn, wv_ref[...]).astype(BF16)
    log_f = _log_sigmoid(_dot(hn, wf_ref[...]) + bf_ref[...])
    tm = log_f.shape[0]
    c = carry_ref[...] + jnp.dot(_tril_ones(tm), log_f, preferred_element_type=F32,
                                 precision=lax.Precision.HIGHEST)
    carry_ref[...] = c[tm - 1:tm, :]
    c_ref[...] = c
    ct_ref[...] = c.T[:ct_ref.shape[0], :]


def _kv_proj(x, gain, wk, wv, wf, bf, bsz, seq, tm):
    t = x.shape[0]
    tiles_per_seq = seq // tm
    row = lambda n: pl.BlockSpec((tm, n), lambda i: (i, 0))
    n_rows = 16
    return pl.pallas_call(
        functools.partial(_kv_kernel, tiles_per_seq),
        grid=(t // tm,),
        in_specs=[row(D_MODEL), _full((1, D_MODEL)), _full(wk.shape), _full(wv.shape), _full(wf.shape),
                  _full(bf.shape)],
        out_specs=[row(MAIN_WIDTH), row(MAIN_WIDTH), row(LANES),
                   pl.BlockSpec((None, n_rows, tm), lambda i: (i // tiles_per_seq, 0, i % tiles_per_seq))],
        out_shape=[jax.ShapeDtypeStruct((t, MAIN_WIDTH), BF16), jax.ShapeDtypeStruct((t, MAIN_WIDTH), BF16),
                   jax.ShapeDtypeStruct((t, LANES), F32), jax.ShapeDtypeStruct((bsz, n_rows, seq), F32)],
        scratch_shapes=[pltpu.VMEM((1, LANES), F32)],
        compiler_params=_params(1),
        name="kv_proj_fox",
    )(x, gain, wk, wv, wf, bf)


def _gla_kernel(chunks, q_ref, k_ref, v_ref, r_ref, la_ref, on_ref, o_ref, st_ref):
    @pl.when(pl.program_id(1) == 0)
    def _():
        st_ref[...] = jnp.zeros_like(st_ref)

    tril = _tril_ones(GLA_CHUNK)
    n_sub = GLA_CHUNK // GLA_SUB
    masks = []
    for i in range(n_sub):
        n_keys = GLA_SUB * (i + 1)
        rr = lax.broadcasted_iota(jnp.int32, (GLA_SUB, n_keys), 0) + GLA_SUB * i
        cc = lax.broadcasted_iota(jnp.int32, (GLA_SUB, n_keys), 1)
        masks.append(cc <= rr)

    def chunk_body(c, carry):
        r0 = pl.multiple_of(c * GLA_CHUNK, GLA_CHUNK)
        rows = pl.ds(r0, GLA_CHUNK)
        b_all = jnp.dot(tril, la_ref[rows, :], preferred_element_type=F32, precision=lax.Precision.HIGHEST)
        for h in range(GLA_HEADS):
            ks = slice(h * GLA_DK_PAD, (h + 1) * GLA_DK_PAD)
            vs = slice(h * GLA_DV_PAD, (h + 1) * GLA_DV_PAD)
            q = q_ref[rows, ks].astype(F32)
            k = k_ref[rows, ks].astype(F32)
            v = v_ref[rows, vs]
            b = b_all[:, ks]
            b_last = b[GLA_CHUNK - 1:GLA_CHUNK, :]
            st = st_ref[h]
            o_inter = _dot_nt((q * jnp.exp(b)).astype(BF16), st.astype(BF16))
            o_parts = []
            for i in range(n_sub):
                lo, hi = GLA_SUB * i, GLA_SUB * (i + 1)
                b_i = b[lo:hi, :]
                qd = q[lo:hi, :] * jnp.exp(b_i - b[lo - 1:lo, :]) if i else q[lo:hi, :] * jnp.exp(b_i)
                kd = k[:hi, :] * jnp.exp(b[lo - 1:lo, :] - b[:hi, :]) if i else k[:hi, :] * jnp.exp(-b[:hi, :])
                a = jnp.where(masks[i], _dot_nt(qd.astype(BF16), kd.astype(BF16)), 0.0)
                o_parts.append(_dot(a.astype(BF16), v[:hi, :]))
            o = o_inter + jnp.concatenate(o_parts, axis=0)
            kdec = (k * jnp.exp(b_last - b)).astype(BF16)
            st_ref[h] = st * jnp.exp(b_last) + _dot_tn(v, kdec)
            ms = jnp.sum(o * o, axis=-1, keepdims=True) * (1.0 / GLA_DV)
            y = o * lax.rsqrt(ms + EPS) * on_ref[...] * _silu(r_ref[rows, vs].astype(F32))
            o_ref[rows, vs] = y.astype(BF16)
        return carry

    lax.fori_loop(0, chunks, chunk_body, 0)


def _gla(q, k, v, r, la, onorm, bsz, seq, tm):
    t = q.shape[0]
    tiles_per_seq = seq // tm
    kw, vw = GLA_HEADS * GLA_DK_PAD, GLA_HEADS * GLA_DV_PAD
    row = lambda n: pl.BlockSpec((tm, n), lambda b, i: (b * tiles_per_seq + i, 0))
    return pl.pallas_call(
        functools.partial(_gla_kernel, tm // GLA_CHUNK),
        grid=(bsz, tiles_per_seq),
        in_specs=[row(kw), row(kw), row(vw), row(vw), row(kw), pl.BlockSpec((1, GLA_DV_PAD), lambda b, i: (0, 0))],
        out_specs=row(vw),
        out_shape=jax.ShapeDtypeStruct((t, vw), BF16),
        scratch_shapes=[pltpu.VMEM((GLA_HEADS, GLA_DV_PAD, GLA_DK_PAD), F32)],
        compiler_params=_params(2),
        name="gla_scan",
    )(q, k, v, r, la, onorm)


def _fox_kernel(tq, tk, q_ref, k_ref, v_ref, c_ref, ct_ref, o_ref):
    hp, i = pl.program_id(1), pl.program_id(2)
    lane = lax.broadcasted_iota(jnp.int32, (tq, LANES), 1)
    first = lane < FOX_HEAD_DIM
    q = q_ref[...]
    c_tile = c_ref[...]
    zero = jnp.zeros_like(q)
    qs = (jnp.where(first, q, zero), jnp.where(first, zero, q))
    cqs = tuple(jnp.sum(jnp.where(lane == 2 * hp + hh, c_tile, 0.0), axis=-1, keepdims=True) for hh in range(2))
    q_pos = i * tq + lax.broadcasted_iota(jnp.int32, (tq, tk), 0)
    k_off = lax.broadcasted_iota(jnp.int32, (tq, tk), 1)

    def step(j, carry, masked):
        cols = pl.ds(pl.multiple_of(j * tk, tk), tk)
        kt, vt = k_ref[cols, :], v_ref[cols, :]
        new = []
        for hh in range(2):
            m, l, acc = carry[hh]
            s = _dot_nt(qs[hh], kt) + (cqs[hh] - ct_ref[pl.ds(2 * hp + hh, 1), cols])
            if masked:
                s = jnp.where(k_off + j * tk <= q_pos, s, -jnp.inf)
            m_new = jnp.maximum(m, jnp.max(s, axis=-1, keepdims=True))
            alpha = jnp.exp(m - m_new)
            p = jnp.exp(s - m_new)
            l = alpha * l + jnp.sum(p, axis=-1, keepdims=True)
            acc = alpha * acc + _dot(p.astype(BF16), vt)
            new.append((m_new, l, acc))
        return tuple(new)

    init = (jnp.full((tq, 1), -jnp.inf, F32), jnp.zeros((tq, 1), F32), jnp.zeros((tq, LANES), F32))
    n_before = (i * tq) // tk
    carry = lax.fori_loop(0, n_before, functools.partial(step, masked=False), (init, init))
    (_, l0, acc0), (_, l1, acc1) = step(n_before, carry, masked=True)
    o_ref[...] = jnp.where(first, acc0 / l0, acc1 / l1).astype(BF16)


def _fox(q, k, v, c, ct, bsz, seq, tq, tk):
    t = q.shape[0]
    nq = seq // tq
    pairs = FOX_HEADS // 2
    return pl.pallas_call(
        functools.partial(_fox_kernel, tq, tk),
        grid=(bsz, pairs, nq),
        in_specs=[pl.BlockSpec((tq, LANES), lambda b, p, i: (b * nq + i, p)),
                  pl.BlockSpec((seq, LANES), lambda b, p, i: (b, p)),
                  pl.BlockSpec((seq, LANES), lambda b, p, i: (b, p)),
                  pl.BlockSpec((tq, LANES), lambda b, p, i: (b * nq + i, 0)),
                  pl.BlockSpec((None, ct.shape[1], seq), lambda b, p, i: (b, 0, 0))],
        out_specs=pl.BlockSpec((tq, LANES), lambda b, p, i: (b * nq + i, p)),
        out_shape=jax.ShapeDtypeStruct((t, MAIN_WIDTH), BF16),
        compiler_params=_params(3),
        name="fox_attention",
    )(q, k, v, c, ct)


def _out_kernel(x_ref, main_ref, mq_ref, mk_ref, mv_ref, wo_main_ref, wo_mem_ref, o_ref):
    mq, mk, mv = mq_ref[...], mk_ref[...], mv_ref[...]
    heads = []
    for h in range(MEM_HEADS):
        hs = slice(h * MEM_HEAD_DIM, (h + 1) * MEM_HEAD_DIM)
        s = _dot_nt(mq[:, hs], mk[:, hs]) * (MEM_HEAD_DIM ** -0.5)
        p = jnp.exp(s - jnp.max(s, axis=-1, keepdims=True))
        l = jnp.sum(p, axis=-1, keepdims=True)
        heads.append(_dot(p.astype(BF16), mv[:, hs]) / l)
    mem_o = jnp.concatenate(heads, axis=-1).astype(BF16)
    o_ref[...] = x_ref[...] + _dot(main_ref[...], wo_main_ref[...]) + _dot(mem_o, wo_mem_ref[...])


def _out_proj(x, main, mq, mem_k, mem_v, wo_main, wo_mem, seq, tm):
    t = x.shape[0]
    tiles_per_seq = seq // tm
    n_mem = mem_k.shape[1]
    row = lambda n: pl.BlockSpec((tm, n), lambda i: (i, 0))
    mem = pl.BlockSpec((None, n_mem, MEM_WIDTH), lambda i: (i // tiles_per_seq, 0, 0))
    return pl.pallas_call(
        _out_kernel,
        grid=(t // tm,),
        in_specs=[row(D_MODEL), row(main.shape[1]), row(MEM_WIDTH), mem, mem, _full(wo_main.shape),
                  _full(wo_mem.shape)],
        out_specs=row(D_MODEL),
        out_shape=jax.ShapeDtypeStruct((t, D_MODEL), F32),
        compiler_params=_params(1),
        name="out_proj_mem_attn",
    )(x, main, mq, mem_k, mem_v, wo_main, wo_mem)


def _swiglu_acc(hn, wg_ref, wu_ref, wd_ref, acc_ref, row_scale=None):
    for j in range(D_FF // FF_CHUNK):
        cs = slice(j * FF_CHUNK, (j + 1) * FF_CHUNK)
        a = _silu(_dot(hn, wg_ref[:, cs])) * _dot(hn, wu_ref[:, cs])
        if row_scale is not None:
            a = a * row_scale
        acc_ref[...] += _dot(a.astype(BF16), wd_ref[cs, :])


def _ffn_kernel(x_ref, g_ref, wg_ref, wu_ref, wd_ref, o_ref):
    x = x_ref[...]
    o_ref[...] = x
    _swiglu_acc(_rms_normed(x, g_ref[...]).astype(BF16), wg_ref, wu_ref, wd_ref, o_ref)


def _ffn(x, gain, wg, wu, wd, tm):
    t = x.shape[0]
    row = pl.BlockSpec((tm, D_MODEL), lambda i: (i, 0))
    return pl.pallas_call(
        _ffn_kernel,
        grid=(t // tm,),
        in_specs=[row, _full((1, D_MODEL)), _full(wg.shape), _full(wu.shape), _full(wd.shape)],
        out_specs=row,
        out_shape=jax.ShapeDtypeStruct((t, D_MODEL), F32),
        compiler_params=_params(1),
        name="dense_swiglu",
    )(x, gain, wg, wu, wd)


def _router_kernel(x_ref, g_ref, wr_ref, hn_ref, gates_ref):
    hn = _rms_normed(x_ref[...], g_ref[...])
    hn_ref[...] = hn.astype(BF16)
    logits = jnp.dot(hn, wr_ref[...], preferred_element_type=F32, precision=lax.Precision.HIGHEST)
    lane = lax.broadcasted_iota(jnp.int32, logits.shape, 1)
    logits = jnp.where(lane < N_EXPERTS, logits, -jnp.inf)
    m1 = jnp.max(logits, axis=-1, keepdims=True)
    i1 = jnp.min(jnp.where(logits == m1, lane, LANES), axis=-1, keepdims=True)
    rest = jnp.where(lane == i1, -jnp.inf, logits)
    m2 = jnp.max(rest, axis=-1, keepdims=True)
    i2 = jnp.min(jnp.where(rest == m2, lane, LANES), axis=-1, keepdims=True)
    e2 = jnp.exp(m2 - m1)
    w1 = 1.0 / (1.0 + e2)
    gates_ref[...] = jnp.where(lane == i1, w1, 0.0) + jnp.where(lane == i2, e2 * w1, 0.0)


def _router(x, gain, wr, tm):
    t = x.shape[0]
    row = lambda n: pl.BlockSpec((tm, n), lambda i: (i, 0))
    return pl.pallas_call(
        _router_kernel,
        grid=(t // tm,),
        in_specs=[row(D_MODEL), _full((1, D_MODEL)), _full(wr.shape)],
        out_specs=[row(D_MODEL), row(LANES)],
        out_shape=[jax.ShapeDtypeStruct((t, D_MODEL), BF16), jax.ShapeDtypeStruct((t, LANES), F32)],
        compiler_params=_params(1),
        name="moe_router",
    )(x, gain, wr)


def _moe_kernel(x_ref, hn_ref, gates_ref, gf_ref, wg_ref, wu_ref, wd_ref, o_ref, acc_ref):
    e = pl.program_id(1)

    @pl.when(e == 0)
    def _():
        acc_ref[...] = x_ref[...]

    gates = gates_ref[...]
    lane = lax.broadcasted_iota(jnp.int32, gates.shape, 1)
    gate = jnp.sum(jnp.where(lane == e, gates, 0.0), axis=-1, keepdims=True)
    _swiglu_acc(hn_ref[...], wg_ref, wu_ref, wd_ref, acc_ref, row_scale=gate)

    @pl.when(e == N_EXPERTS - 1)
    def _():
        o_ref[...] = _rms_normed(acc_ref[...], gf_ref[...])


def _moe(x, hn, gates, gain_final, wg, wu, wd, tm):
    t = x.shape[0]
    row = lambda n: pl.BlockSpec((tm, n), lambda i, e: (i, 0))
    return pl.pallas_call(
        _moe_kernel,
        grid=(t // tm, N_EXPERTS),
        in_specs=[row(D_MODEL), row(D_MODEL), row(LANES), pl.BlockSpec((1, D_MODEL), lambda i, e: (0, 0)),
                  pl.BlockSpec((None, D_MODEL, D_FF), lambda i, e: (e, 0, 0)),
                  pl.BlockSpec((None, D_MODEL, D_FF), lambda i, e: (e, 0, 0)),
                  pl.BlockSpec((None, D_FF, D_MODEL), lambda i, e: (e, 0, 0))],
        out_specs=row(D_MODEL),
        out_shape=jax.ShapeDtypeStruct((t, D_MODEL), F32),
        scratch_shapes=[pltpu.VMEM((tm, D_MODEL), F32)],
        compiler_params=_params(2),
        name="moe_experts",
    )(x, hn, gates, gain_final, wg, wu, wd)


def _pad_heads(w, heads, width, padded):
    lead = w.shape[:-1]
    w = w.reshape(lead + (heads, width))
    w = jnp.pad(w, [(0, 0)] * len(lead) + [(0, 0), (0, padded - width)])
    return w.reshape(lead + (heads * padded,))


def _pad_cols(w, n):
    return jnp.pad(w, [(0, 0)] * (w.ndim - 1) + [(0, n - w.shape[-1])])


def kernel(x, mem, norm_mix, norm_mem, norm_ffn, norm_kv, norm_final, w_in_a, w_gla_gate2, b_gla_gate2,
           gla_onorm, w_in_b, w_kv, b_forget, w_mem_kv, w_out, w_ff_gate, w_ff_up, w_ff_down, w_router,
           w_moe_gate, w_moe_up, w_moe_down):
    bsz, seq, _ = x.shape
    n_mem = mem.shape[1]
    t = bsz * seq
    tm = 512
    xt = x.reshape(t, D_MODEL)
    memt = mem.reshape(bsz * n_mem, D_MODEL)
    gain = lambda g: g.reshape(1, D_MODEL)
    kw = GLA_HEADS * GLA_DK

    def mem_kv(layer):
        wk, wv = w_mem_kv[layer, :, :MEM_WIDTH], w_mem_kv[layer, :, MEM_WIDTH:]
        mk, mv = _norm_proj(memt, gain(norm_mem[layer]), [wk.astype(BF16), wv.astype(BF16)], (1.0, 1.0),
                            n_mem, "mem_kv_proj")
        return mk.reshape(bsz, n_mem, MEM_WIDTH), mv.reshape(bsz, n_mem, MEM_WIDTH)

    wa = w_in_a[0]
    wq = _pad_heads(wa[:, :kw], GLA_HEADS, GLA_DK, GLA_DK_PAD).astype(BF16)
    wk = _pad_heads(wa[:, kw:2 * kw], GLA_HEADS, GLA_DK, GLA_DK_PAD).astype(BF16)
    off = 2 * kw
    wv = _pad_heads(wa[:, off:off + MAIN_WIDTH], GLA_HEADS, GLA_DV, GLA_DV_PAD).astype(BF16)
    off += MAIN_WIDTH
    wr = _pad_heads(wa[:, off:off + MAIN_WIDTH], GLA_HEADS, GLA_DV, GLA_DV_PAD).astype(BF16)
    off += MAIN_WIDTH
    wg1 = _pad_cols(wa[:, off:off + GLA_GATE_RANK], LANES).astype(BF16)
    off += GLA_GATE_RANK
    wmq = wa[:, off:off + MEM_WIDTH].astype(BF16)
    wg2 = jnp.pad(_pad_heads(w_gla_gate2[0], GLA_HEADS, GLA_DK, GLA_DK_PAD), ((0, LANES - GLA_GATE_RANK), (0, 0)))
    bg2 = _pad_heads(b_gla_gate2[0], GLA_HEADS, GLA_DK, GLA_DK_PAD).reshape(1, -1)
    onorm = _pad_cols(gla_onorm[0], GLA_DV_PAD).reshape(1, GLA_DV_PAD)

    q, k, v, r, mq, la = _in_a(xt, gain(norm_mix[0]), wq, wk, wv, wr, wmq, wg1, wg2, bg2, tm)
    main = _gla(q, k, v, r, la, onorm, bsz, seq, tm)
    mk, mv = mem_kv(0)
    wo = w_out[0]
    wo_main = jnp.pad(wo[:MAIN_WIDTH].reshape(GLA_HEADS, GLA_DV, D_MODEL),
                      ((0, 0), (0, GLA_DV_PAD - GLA_DV), (0, 0))).reshape(GLA_HEADS * GLA_DV_PAD, D_MODEL)
    xt = _out_proj(xt, main, mq, mk, mv, wo_main.astype(BF16), wo[MAIN_WIDTH:].astype(BF16), seq, tm)
    xt = _ffn(xt, gain(norm_ffn[0]), w_ff_gate[0].astype(BF16), w_ff_up[0].astype(BF16),
              w_ff_down[0].astype(BF16), tm)

    wf = _pad_cols(w_kv[:, 2 * MAIN_WIDTH:], LANES).astype(BF16)
    bf = _pad_cols(b_forget, LANES).reshape(1, LANES)
    k_sh, v_sh, c, ct = _kv_proj(xt, gain(norm_kv), w_kv[:, :MAIN_WIDTH].astype(BF16),
                                 w_kv[:, MAIN_WIDTH:2 * MAIN_WIDTH].astype(BF16), wf, bf, bsz, seq, tm)

    wb = w_in_b[0]
    qf, mq = _norm_proj(xt, gain(norm_mix[1]), [wb[:, :MAIN_WIDTH].astype(BF16), wb[:, MAIN_WIDTH:].astype(BF16)],
                        (FOX_HEAD_DIM ** -0.5, 1.0), tm, "in_proj_fox")
    main = _fox(qf, k_sh, v_sh, c, ct, bsz, seq, 256, 1024)
    mk, mv = mem_kv(1)
    wo = w_out[1]
    xt = _out_proj(xt, main, mq, mk, mv, wo[:MAIN_WIDTH].astype(BF16), wo[MAIN_WIDTH:].astype(BF16), seq, tm)
    hn, gates = _router(xt, gain(norm_ffn[1]), _pad_cols(w_router[0], LANES), tm)
    out = _moe(xt, hn, gates, gain(norm_final), w_moe_gate[0].astype(BF16), w_moe_up[0].astype(BF16),
               w_moe_down[0].astype(BF16), tm)
    return out.reshape(bsz, seq, D_MODEL)
```

```python
import functools

import jax
import jax.numpy as jnp
from jax import lax
from jax.experimental import pallas as pl
from jax.experimental.pallas import tpu as pltpu

D_MODEL = 1024
EPS = 1e-6
GLA_CHUNK = 64
GLA_SUB = 16
MEM_HEADS = 4
MEM_HEAD_DIM = 64
MEM_WIDTH = MEM_HEADS * MEM_HEAD_DIM
MAIN_WIDTH = D_MODEL - MEM_WIDTH
GLA_HEADS = 4
GLA_DK = MAIN_WIDTH // 2 // GLA_HEADS
GLA_DV = MAIN_WIDTH // GLA_HEADS
GLA_DK_PAD = 128
GLA_DV_PAD = 256
GLA_GATE_RANK = 16
GLA_GATE_TAU = 16.0
FOX_HEADS = 12
FOX_HEAD_DIM = 64
D_FF = 2816
N_EXPERTS = 8
LANES = 128
FF_CHUNK = 256
VMEM_LIMIT = 56 * 1024 * 1024

BF16 = jnp.bfloat16
F32 = jnp.float32


def _params(n_axes, vmem=VMEM_LIMIT):
    return pltpu.CompilerParams(dimension_semantics=("arbitrary",) * n_axes, vmem_limit_bytes=vmem)


def _rms_normed(x, gain):
    ms = jnp.mean(x * x, axis=-1, keepdims=True)
    return x * lax.rsqrt(ms + EPS) * gain


def _log_sigmoid(z):
    return jnp.minimum(z, 0.0) - jnp.log(1.0 + jnp.exp(-jnp.abs(z)))


def _silu(z):
    return z / (1.0 + jnp.exp(-z))


def _dot(a, b):
    return jnp.dot(a, b, preferred_element_type=F32)


def _dot_nt(a, b):
    return lax.dot_general(a, b, (((1,), (1,)), ((), ())), preferred_element_type=F32)


def _dot_tn(a, b):
    return lax.dot_general(a, b, (((0,), (0,)), ((), ())), preferred_element_type=F32)


def _tril_ones(n):
    r = lax.broadcasted_iota(jnp.int32, (n, n), 0)
    c = lax.broadcasted_iota(jnp.int32, (n, n), 1)
    return (c <= r).astype(F32)


def _full(shape):
    return pl.BlockSpec(shape, lambda *_: (0,) * len(shape))


def _in_a_kernel(x_ref, g_ref, wq_ref, wk_ref, wv_ref, wr_ref, wmq_ref, wg1_ref, wg2_ref, bg2_ref,
                 q_ref, k_ref, v_ref, r_ref, mq_ref, la_ref):
    hn = _rms_normed(x_ref[...], g_ref[...]).astype(BF16)
    q_ref[...] = (_dot(hn, wq_ref[...]) * (GLA_DK ** -0.5)).astype(BF16)
    k_ref[...] = _dot(hn, wk_ref[...]).astype(BF16)
    v_ref[...] = _dot(hn, wv_ref[...]).astype(BF16)
    r_ref[...] = _dot(hn, wr_ref[...]).astype(BF16)
    mq_ref[...] = _dot(hn, wmq_ref[...]).astype(BF16)
    glr = _dot(hn, wg1_ref[...])
    z = jnp.dot(glr, wg2_ref[...], preferred_element_type=F32, precision=lax.Precision.HIGHEST)
    la_ref[...] = _log_sigmoid(z + bg2_ref[...]) * (1.0 / GLA_GATE_TAU)


def _in_a(x, gain, wq, wk, wv, wr, wmq, wg1, wg2, bg2, tm):
    t = x.shape[0]
    row = lambda n: pl.BlockSpec((tm, n), lambda i: (i, 0))
    kw, vw = GLA_HEADS * GLA_DK_PAD, GLA_HEADS * GLA_DV_PAD
    return pl.pallas_call(
        _in_a_kernel,
        grid=(t // tm,),
        in_specs=[row(D_MODEL), _full((1, D_MODEL)), _full(wq.shape), _full(wk.shape), _full(wv.shape),
                  _full(wr.shape), _full(wmq.shape), _full(wg1.shape), _full(wg2.shape), _full(bg2.shape)],
        out_specs=[row(kw), row(kw), row(vw), row(vw), row(MEM_WIDTH), row(kw)],
        out_shape=[jax.ShapeDtypeStruct((t, kw), BF16), jax.ShapeDtypeStruct((t, kw), BF16),
                   jax.ShapeDtypeStruct((t, vw), BF16), jax.ShapeDtypeStruct((t, vw), BF16),
                   jax.ShapeDtypeStruct((t, MEM_WIDTH), BF16), jax.ShapeDtypeStruct((t, kw), F32)],
        compiler_params=_params(1),
        name="in_proj_gla",
    )(x, gain, wq, wk, wv, wr, wmq, wg1, wg2, bg2)


def _norm_proj_kernel(n_out, scales, x_ref, g_ref, *refs):
    w_refs, o_refs = refs[:n_out], refs[n_out:]
    hn = _rms_normed(x_ref[...], g_ref[...]).astype(BF16)
    for w_ref, o_ref, s in zip(w_refs, o_refs, scales):
        o = _dot(hn, w_ref[...])
        if s != 1.0:
            o = o * s
        o_ref[...] = o.astype(o_ref.dtype)


def _norm_proj(x, gain, weights, scales, tm, name):
    t = x.shape[0]
    row = lambda n: pl.BlockSpec((tm, n), lambda i: (i, 0))
    return pl.pallas_call(
        functools.partial(_norm_proj_kernel, len(weights), scales),
        grid=(t // tm,),
        in_specs=[row(D_MODEL), _full((1, D_MODEL))] + [_full(w.shape) for w in weights],
        out_specs=[row(w.shape[1]) for w in weights],
        out_shape=[jax.ShapeDtypeStruct((t, w.shape[1]), BF16) for w in weights],
        compiler_params=_params(1),
        name=name,
    )(x, gain, *weights)


def _kv_kernel(tiles_per_seq, x_ref, g_ref, wk_ref, wv_ref, wf_ref, bf_ref,
               k_ref, v_ref, c_ref, ct_ref, carry_ref):
    @pl.when(pl.program_id(0) % tiles_per_seq == 0)
    def _():
        carry_ref[...] = jnp.zeros_like(carry_ref)

    hn = _rms_normed(x_ref[...], g_ref[...]).astype(BF16)
    k_ref[...] = _dot(hn, wk_ref[...]).astype(BF16)
    v_ref[...] = _dot(hn, wv_ref[...]).astype(BF16)
    log_f = _log_sigmoid(_dot(hn, wf_ref[...]) + bf_ref[...])
    tm = log_f.shape[0]
    c = carry_ref[...] + jnp.dot(_tril_ones(tm), log_f, preferred_element_type=F32,
                                 precision=lax.Precision.HIGHEST)
    carry_ref[...] = c[tm - 1:tm, :]
    c_ref[...] = c
    ct_ref[...] = c.T[:ct_ref.shape[0], :]


def _kv_proj(x, gain, wk, wv, wf, bf, bsz, seq, tm):
    t = x.shape[0]
    tiles_per_seq = seq // tm
    row = lambda n: pl.BlockSpec((tm, n), lambda i: (i, 0))
    n_rows = 16
    return pl.pallas_call(
        functools.partial(_kv_kernel, tiles_per_seq),
        grid=(t // tm,),
        in_specs=[row(D_MODEL), _full((1, D_MODEL)), _full(wk.shape), _full(wv.shape), _full(wf.shape),
                  _full(bf.shape)],
        out_specs=[row(MAIN_WIDTH), row(MAIN_WIDTH), row(LANES),
                   pl.BlockSpec((None, n_rows, tm), lambda i: (i // tiles_per_seq, 0, i % tiles_per_seq))],
        out_shape=[jax.ShapeDtypeStruct((t, MAIN_WIDTH), BF16), jax.ShapeDtypeStruct((t, MAIN_WIDTH), BF16),
                   jax.ShapeDtypeStruct((t, LANES), F32), jax.ShapeDtypeStruct((bsz, n_rows, seq), F32)],
        scratch_shapes=[pltpu.VMEM((1, LANES), F32)],
        compiler_params=_params(1),
        name="kv_proj_fox",
    )(x, gain, wk, wv, wf, bf)


def _gla_kernel(chunks, q_ref, k_ref, v_ref, r_ref, la_ref, on_ref, o_ref, st_ref):
    @pl.when(pl.program_id(1) == 0)
    def _():
        st_ref[...] = jnp.zeros_like(st_ref)

    tril = _tril_ones(GLA_CHUNK)
    n_sub = GLA_CHUNK // GLA_SUB
    masks = []
    for i in range(n_sub):
        n_keys = GLA_SUB * (i + 1)
        rr = lax.broadcasted_iota(jnp.int32, (GLA_SUB, n_keys), 0) + GLA_SUB * i
        cc = lax.broadcasted_iota(jnp.int32, (GLA_SUB, n_keys), 1)
        masks.append(cc <= rr)

    def chunk_body(c, carry):
        r0 = pl.multiple_of(c * GLA_CHUNK, GLA_CHUNK)
        rows = pl.ds(r0, GLA_CHUNK)
        b_all = jnp.dot(tril, la_ref[rows, :], preferred_element_type=F32, precision=lax.Precision.HIGHEST)
        for h in range(GLA_HEADS):
            ks = slice(h * GLA_DK_PAD, (h + 1) * GLA_DK_PAD)
            vs = slice(h * GLA_DV_PAD, (h + 1) * GLA_DV_PAD)
            q = q_ref[rows, ks].astype(F32)
            k = k_ref[rows, ks].astype(F32)
            v = v_ref[rows, vs]
            b = b_all[:, ks]
            b_last = b[GLA_CHUNK - 1:GLA_CHUNK, :]
            st = st_ref[h]
            o_inter = _dot_nt((q * jnp.exp(b)).astype(BF16), st.astype(BF16))
            o_parts = []
            for i in range(n_sub):
                lo, hi = GLA_SUB * i, GLA_SUB * (i + 1)
                b_i = b[lo:hi, :]
                qd = q[lo:hi, :] * jnp.exp(b_i - b[lo - 1:lo, :]) if i else q[lo:hi, :] * jnp.exp(b_i)
                kd = k[:hi, :] * jnp.exp(b[lo - 1:lo, :] - b[:hi, :]) if i else k[:hi, :] * jnp.exp(-b[:hi, :])
                a = jnp.where(masks[i], _dot_nt(qd.astype(BF16), kd.astype(BF16)), 0.0)
                o_parts.append(_dot(a.astype(BF16), v[:hi, :]))
            o = o_inter + jnp.concatenate(o_parts, axis=0)
            kdec = (k * jnp.exp(b_last - b)).astype(BF16)
            st_ref[h] = st * jnp.exp(b_last) + _dot_tn(v, kdec)
            ms = jnp.sum(o * o, axis=-1, keepdims=True) * (1.0 / GLA_DV)
            y = o * lax.rsqrt(ms + EPS) * on_ref[...] * _silu(r_ref[rows, vs].astype(F32))
            o_ref[rows, vs] = y.astype(BF16)
        return carry

    lax.fori_loop(0, chunks, chunk_body, 0)


def _gla(q, k, v, r, la, onorm, bsz, seq, tm):
    t = q.shape[0]
    tiles_per_seq = seq // tm
    kw, vw = GLA_HEADS * GLA_DK_PAD, GLA_HEADS * GLA_DV_PAD
    row = lambda n: pl.BlockSpec((tm, n), lambda b, i: (b * tiles_per_seq + i, 0))
    return pl.pallas_call(
        functools.partial(_gla_kernel, tm // GLA_CHUNK),
        grid=(bsz, tiles_per_seq),
        in_specs=[row(kw), row(kw), row(vw), row(vw), row(kw), pl.BlockSpec((1, GLA_DV_PAD), lambda b, i: (0, 0))],
        out_specs=row(vw),
        out_shape=jax.ShapeDtypeStruct((t, vw), BF16),
        scratch_shapes=[pltpu.VMEM((GLA_HEADS, GLA_DV_PAD, GLA_DK_PAD), F32)],
        compiler_params=_params(2),
        name="gla_scan",
    )(q, k, v, r, la, onorm)


def _fox_kernel(tq, tk, q_ref, k_ref, v_ref, c_ref, ct_ref, o_ref):
    hp, i = pl.program_id(1), pl.program_id(2)
    lane = lax.broadcasted_iota(jnp.int32, (tq, LANES), 1)
    first = lane < FOX_HEAD_DIM
    q = q_ref[...]
    c_tile = c_ref[...]
    zero = jnp.zeros_like(q)
    qs = (jnp.where(first, q, zero), jnp.where(first, zero, q))
    cqs = tuple(jnp.sum(jnp.where(lane == 2 * hp + hh, c_tile, 0.0), axis=-1, keepdims=True) for hh in range(2))
    q_pos = i * tq + lax.broadcasted_iota(jnp.int32, (tq, tk), 0)
    k_off = lax.broadcasted_iota(jnp.int32, (tq, tk), 1)

    def step(j, carry, masked):
        cols = pl.ds(pl.multiple_of(j * tk, tk), tk)
        kt, vt = k_ref[cols, :], v_ref[cols, :]
        new = []
        for hh in range(2):
            m, l, acc = carry[hh]
            s = _dot_nt(qs[hh], kt) + (cqs[hh] - ct_ref[pl.ds(2 * hp + hh, 1), cols])
            if masked:
                s = jnp.where(k_off + j * tk <= q_pos, s, -jnp.inf)
            m_new = jnp.maximum(m, jnp.max(s, axis=-1, keepdims=True))
            alpha = jnp.exp(m - m_new)
            p = jnp.exp(s - m_new)
            l = alpha * l + jnp.sum(p, axis=-1, keepdims=True)
            acc = alpha * acc + _dot(p.astype(BF16), vt)
            new.append((m_new, l, acc))
        return tuple(new)

    init = (jnp.full((tq, 1), -jnp.inf, F32), jnp.zeros((tq, 1), F32), jnp.zeros((tq, LANES), F32))
    n_before = (i * tq) // tk
    carry = lax.fori_loop(0, n_before, functools.partial(step, masked=False), (init, init))
    (_, l0, acc0), (_, l1, acc1) = step(n_before, carry, masked=True)
    o_ref[...] = jnp.where(first, acc0 / l0, acc1 / l1).astype(BF16)


def _fox(q, k, v, c, ct, bsz, seq, tq, tk):
    t = q.shape[0]
    nq = seq // tq
    pairs = FOX_HEADS // 2
    return pl.pallas_call(
        functools.partial(_fox_kernel, tq, tk),
        grid=(bsz, pairs, nq),
        in_specs=[pl.BlockSpec((tq, LANES), lambda b, p, i: (b * nq + i, p)),
                  pl.BlockSpec((seq, LANES), lambda b, p, i: (b, p)),
                  pl.BlockSpec((seq, LANES), lambda b, p, i: (b, p)),
                  pl.BlockSpec((tq, LANES), lambda b, p, i: (b * nq + i, 0)),
                  pl.BlockSpec((None, ct.shape[1], seq), lambda b, p, i: (b, 0, 0))],
        out_specs=pl.BlockSpec((tq, LANES), lambda b, p, i: (b * nq + i, p)),
        out_shape=jax.ShapeDtypeStruct((t, MAIN_WIDTH), BF16),
        compiler_params=_params(3),
        name="fox_attention",
    )(q, k, v, c, ct)


def _out_kernel(x_ref, main_ref, mq_ref, mk_ref, mv_ref, wo_main_ref, wo_mem_ref, o_ref):
    mq, mk, mv = mq_ref[...], mk_ref[...], mv_ref[...]
    heads = []
    for h in range(MEM_HEADS):
        hs = slice(h * MEM_HEAD_DIM, (h + 1) * MEM_HEAD_DIM)
        s = _dot_nt(mq[:, hs], mk[:, hs]) * (MEM_HEAD_DIM ** -0.5)
        p = jnp.exp(s - jnp.max(s, axis=-1, keepdims=True))
        l = jnp.sum(p, axis=-1, keepdims=True)
        heads.append(_dot(p.astype(BF16), mv[:, hs]) / l)
    mem_o = jnp.concatenate(heads, axis=-1).astype(BF16)
    o_ref[...] = x_ref[...] + _dot(main_ref[...], wo_main_ref[...]) + _dot(mem_o, wo_mem_ref[...])


def _out_proj(x, main, mq, mem_k, mem_v, wo_main, wo_mem, seq, tm):
    t = x.shape[0]
    tiles_per_seq = seq // tm
    n_mem = mem_k.shape[1]
    row = lambda n: pl.BlockSpec((tm, n), lambda i: (i, 0))
    mem = pl.BlockSpec((None, n_mem, MEM_WIDTH), lambda i: (i // tiles_per_seq, 0, 0))
    return pl.pallas_call(
        _out_kernel,
        grid=(t // tm,),
        in_specs=[row(D_MODEL), row(main.shape[1]), row(MEM_WIDTH), mem, mem, _full(wo_main.shape),
                  _full(wo_mem.shape)],
        out_specs=row(D_MODEL),
        out_shape=jax.ShapeDtypeStruct((t, D_MODEL), F32),
        compiler_params=_params(1),
        name="out_proj_mem_attn",
    )(x, main, mq, mem_k, mem_v, wo_main, wo_mem)


def _swiglu_acc(hn, wg_ref, wu_ref, wd_ref, acc_ref, row_scale=None):
    for j in range(D_FF // FF_CHUNK):
        cs = slice(j * FF_CHUNK, (j + 1) * FF_CHUNK)
        a = _silu(_dot(hn, wg_ref[:, cs])) * _dot(hn, wu_ref[:, cs])
        if row_scale is not None:
            a = a * row_scale
        acc_ref[...] += _dot(a.astype(BF16), wd_ref[cs, :])


def _ffn_kernel(x_ref, g_ref, wg_ref, wu_ref, wd_ref, o_ref):
    x = x_ref[...]
    o_ref[...] = x
    _swiglu_acc(_rms_normed(x, g_ref[...]).astype(BF16), wg_ref, wu_ref, wd_ref, o_ref)


def _ffn(x, gain, wg, wu, wd, tm):
    t = x.shape[0]
    row = pl.BlockSpec((tm, D_MODEL), lambda i: (i, 0))
    return pl.pallas_call(
        _ffn_kernel,
        grid=(t // tm,),
        in_specs=[row, _full((1, D_MODEL)), _full(wg.shape), _full(wu.shape), _full(wd.shape)],
        out_specs=row,
        out_shape=jax.ShapeDtypeStruct((t, D_MODEL), F32),
        compiler_params=_params(1),
        name="dense_swiglu",
    )(x, gain, wg, wu, wd)


PACKED = D_MODEL // 2
ROUTE_E1, ROUTE_E2, ROUTE_W1, ROUTE_W2, ROUTE_RANK1, ROUTE_RANK2 = range(6)
HI_MASK = 0xFFFF0000


def _pack_rows(h):
    hb = h.astype(BF16).astype(F32)
    lo = pltpu.bitcast(hb[:, :PACKED], jnp.uint32) >> 16
    hi = pltpu.bitcast(hb[:, PACKED:], jnp.uint32) & jnp.uint32(HI_MASK)
    return hi | lo


def _unpack_rows(w):
    lo = pltpu.bitcast(w << 16, F32).astype(BF16)
    hi = pltpu.bitcast(w & jnp.uint32(HI_MASK), F32).astype(BF16)
    return jnp.concatenate([lo, hi], axis=1)


def _router_kernel(x_ref, g_ref, wr_ref, hn_ref, route_ref, counts_ref, carry_ref):
    @pl.when(pl.program_id(0) == 0)
    def _():
        carry_ref[...] = jnp.zeros_like(carry_ref)

    hn = _rms_normed(x_ref[...], g_ref[...])
    hn_ref[...] = _pack_rows(hn)
    logits = jnp.dot(hn, wr_ref[...], preferred_element_type=F32, precision=lax.Precision.HIGHEST)
    tm = logits.shape[0]
    lane = lax.broadcasted_iota(jnp.int32, logits.shape, 1)
    logits = jnp.where(lane < N_EXPERTS, logits, -jnp.inf)
    m1 = jnp.max(logits, axis=-1, keepdims=True)
    i1 = jnp.min(jnp.where(logits == m1, lane, LANES), axis=-1, keepdims=True)
    rest = jnp.where(lane == i1, -jnp.inf, logits)
    m2 = jnp.max(rest, axis=-1, keepdims=True)
    i2 = jnp.min(jnp.where(rest == m2, lane, LANES), axis=-1, keepdims=True)
    e2 = jnp.exp(m2 - m1)
    w1 = 1.0 / (1.0 + e2)
    chosen = jnp.where((lane == i1) | (lane == i2), 1.0, 0.0)
    rr = lax.broadcasted_iota(jnp.int32, (tm, tm), 0)
    cc = lax.broadcasted_iota(jnp.int32, (tm, tm), 1)
    before = jnp.where(cc < rr, 1.0, 0.0).astype(BF16)
    base = carry_ref[...] + _dot(before, chosen.astype(BF16))
    rank1 = jnp.sum(jnp.where(lane == i1, base, 0.0), axis=-1, keepdims=True)
    rank2 = jnp.sum(jnp.where(lane == i2, base, 0.0), axis=-1, keepdims=True)
    carry_ref[...] += jnp.sum(chosen, axis=0, keepdims=True)
    counts_ref[...] = carry_ref[...]
    rec = jnp.zeros_like(logits)
    for slot, val in ((ROUTE_E1, i1.astype(F32)), (ROUTE_E2, i2.astype(F32)), (ROUTE_W1, w1), (ROUTE_W2, e2 * w1),
                      (ROUTE_RANK1, rank1), (ROUTE_RANK2, rank2)):
        rec = jnp.where(lane == slot, val, rec)
    route_ref[...] = rec


def _router(x, gain, wr, tm):
    t = x.shape[0]
    row = lambda n: pl.BlockSpec((tm, n), lambda i: (i, 0))
    return pl.pallas_call(
        _router_kernel,
        grid=(t // tm,),
        in_specs=[row(D_MODEL), _full((1, D_MODEL)), _full(wr.shape)],
        out_specs=[row(PACKED), row(LANES), _full((1, LANES))],
        out_shape=[jax.ShapeDtypeStruct((t, PACKED), jnp.uint32), jax.ShapeDtypeStruct((t, LANES), F32),
                   jax.ShapeDtypeStruct((1, LANES), F32)],
        scratch_shapes=[pltpu.VMEM((1, LANES), F32)],
        compiler_params=_params(1),
        name="moe_router",
    )(x, gain, wr)


def _row_copy(src, src_row, dst, dst_row, sem):
    return pltpu.make_async_copy(src.at[pl.ds(src_row, 1)], dst.at[pl.ds(dst_row, 1)], sem)


def _dispatch_kernel(tm, cnt_ref, pad_ref, off_ref, pos1_ref, pos2_ref, hn_ref, xs_ref, zero_ref, sem, pad_sem):
    @pl.when(pl.program_id(0) == 0)
    def _():
        zero_ref[...] = jnp.zeros_like(zero_ref)
        for e in range(N_EXPERTS + 1):
            lo, hi = off_ref[e] + cnt_ref[e], off_ref[e] + pad_ref[e]

            def fill(r, c):
                _row_copy(zero_ref, 0, xs_ref, r, pad_sem).start()
                return c

            def drain(r, c):
                _row_copy(zero_ref, 0, xs_ref, r, pad_sem).wait()
                return c

            lax.fori_loop(lo, hi, fill, 0)
            lax.fori_loop(lo, hi, drain, 0)

    def issue(r, c):
        _row_copy(hn_ref, r, xs_ref, pos1_ref[0, r], sem).start()
        _row_copy(hn_ref, r, xs_ref, pos2_ref[0, r], sem).start()
        return c

    lax.fori_loop(0, tm, issue, 0, unroll=8)
    for _ in range(2):
        pltpu.make_async_copy(hn_ref, xs_ref.at[pl.ds(0, tm)], sem).wait()


def _dispatch(hn, pos1, pos2, counts, padded, offsets, n_rows, tm):
    t = hn.shape[0]
    nt = t // tm
    smem_row = pl.BlockSpec((None, 1, tm), lambda i, *_: (i, 0, 0), memory_space=pltpu.SMEM)
    return pl.pallas_call(
        functools.partial(_dispatch_kernel, tm),
        grid_spec=pltpu.PrefetchScalarGridSpec(
            num_scalar_prefetch=3,
            grid=(nt,),
            in_specs=[smem_row, smem_row, pl.BlockSpec((tm, PACKED), lambda i, *_: (i, 0))],
            out_specs=pl.BlockSpec(memory_space=pl.ANY),
            scratch_shapes=[pltpu.VMEM((8, PACKED), jnp.uint32), pltpu.SemaphoreType.DMA, pltpu.SemaphoreType.DMA],
        ),
        out_shape=jax.ShapeDtypeStruct((n_rows, PACKED), jnp.uint32),
        compiler_params=_params(1),
        name="moe_dispatch",
    )(counts, padded, offsets, pos1.reshape(nt, 1, tm), pos2.reshape(nt, 1, tm), hn)


def _experts_kernel(tile_expert_ref, n_used_ref, xs_ref, wg_ref, wu_ref, wd_ref, o_ref):
    o_ref[...] = jnp.zeros_like(o_ref)

    @pl.when(pl.program_id(0) < n_used_ref[0])
    def _():
        _swiglu_acc(_unpack_rows(xs_ref[...]), wg_ref, wu_ref, wd_ref, o_ref)


def _experts(xs, tile_expert, n_used, wg, wu, wd, tm):
    n_rows = xs.shape[0]
    wspec = lambda shape: pl.BlockSpec(
        (None,) + shape, lambda i, te, nu: (te[jnp.minimum(i, nu[0] - 1)], 0, 0))
    return pl.pallas_call(
        _experts_kernel,
        grid_spec=pltpu.PrefetchScalarGridSpec(
            num_scalar_prefetch=2,
            grid=(n_rows // tm,),
            in_specs=[pl.BlockSpec((tm, PACKED), lambda i, te, nu: (i, 0)),
                      wspec((D_MODEL, D_FF)), wspec((D_MODEL, D_FF)), wspec((D_FF, D_MODEL))],
            out_specs=pl.BlockSpec((tm, D_MODEL), lambda i, te, nu: (i, 0)),
        ),
        out_shape=jax.ShapeDtypeStruct((n_rows, D_MODEL), F32),
        compiler_params=_params(1),
        name="moe_experts",
    )(tile_expert, n_used, xs, wg, wu, wd)


def _combine_kernel(tm, pos1_ref, pos2_ref, x_ref, route_ref, gf_ref, y_ref, o_ref, buf_ref, sem):
    def issue(r, c):
        _row_copy(y_ref, pos1_ref[0, r], buf_ref.at[0], r, sem).start()
        _row_copy(y_ref, pos2_ref[0, r], buf_ref.at[1], r, sem).start()
        return c

    lax.fori_loop(0, tm, issue, 0, unroll=8)
    route = route_ref[...]
    lane = lax.broadcasted_iota(jnp.int32, route.shape, 1)
    w1 = jnp.sum(jnp.where(lane == ROUTE_W1, route, 0.0), axis=-1, keepdims=True)
    w2 = jnp.sum(jnp.where(lane == ROUTE_W2, route, 0.0), axis=-1, keepdims=True)
    for k in range(2):
        pltpu.make_async_copy(y_ref.at[pl.ds(0, tm)], buf_ref.at[k], sem).wait()
    o_ref[...] = _rms_normed(x_ref[...] + w1 * buf_ref[0] + w2 * buf_ref[1], gf_ref[...])


def _combine(x, route, pos1, pos2, y, gain_final, tm):
    t = x.shape[0]
    nt = t // tm
    smem_row = pl.BlockSpec((None, 1, tm), lambda i: (i, 0, 0), memory_space=pltpu.SMEM)
    row = lambda n: pl.BlockSpec((tm, n), lambda i: (i, 0))
    return pl.pallas_call(
        functools.partial(_combine_kernel, tm),
        grid=(nt,),
        in_specs=[smem_row, smem_row, row(D_MODEL), row(LANES), _full((1, D_MODEL)),
                  pl.BlockSpec(memory_space=pl.ANY)],
        out_specs=row(D_MODEL),
        out_shape=jax.ShapeDtypeStruct((t, D_MODEL), F32),
        scratch_shapes=[pltpu.VMEM((2, tm, D_MODEL), F32), pltpu.SemaphoreType.DMA],
        compiler_params=_params(1),
        name="moe_combine",
    )(pos1.reshape(nt, 1, tm), pos2.reshape(nt, 1, tm), x, route, gain_final, y)


def _moe(x, gain, gain_final, wr, wg, wu, wd, tm):
    t = x.shape[0]
    hn, route, counts = _router(x, gain, wr, tm)
    col = lambda c: route[:, c].astype(jnp.int32)
    counts = counts[0, :N_EXPERTS].astype(jnp.int32)
    padded = (counts + tm - 1) // tm * tm
    ends = jnp.cumsum(padded)
    offsets = ends - padded
    pos1 = offsets[col(ROUTE_E1)] + col(ROUTE_RANK1)
    pos2 = offsets[col(ROUTE_E2)] + col(ROUTE_RANK2)
    n_rows = 2 * t + N_EXPERTS * tm
    tile_start = jnp.arange(n_rows // tm, dtype=jnp.int32) * tm
    tile_expert = jnp.minimum(jnp.sum(tile_start[:, None] >= ends[None, :], axis=1), N_EXPERTS - 1).astype(jnp.int32)
    n_used = (ends[-1:] // tm).astype(jnp.int32)
    tail = lambda a, v: jnp.concatenate([a, v.astype(jnp.int32)])
    xs = _dispatch(hn, pos1, pos2, tail(counts, jnp.zeros((1,))), tail(padded, n_rows - ends[-1:]),
                   tail(offsets, ends[-1:]), n_rows, tm)
    y = _experts(xs, tile_expert, n_used, wg, wu, wd, tm)
    return _combine(x, route, pos1, pos2, y, gain_final, tm // 2)


def _pad_heads(w, heads, width, padded):
    lead = w.shape[:-1]
    w = w.reshape(lead + (heads, width))
    w = jnp.pad(w, [(0, 0)] * len(lead) + [(0, 0), (0, padded - width)])
    return w.reshape(lead + (heads * padded,))


def _pad_cols(w, n):
    return jnp.pad(w, [(0, 0)] * (w.ndim - 1) + [(0, n - w.shape[-1])])


def kernel(x, mem, norm_mix, norm_mem, norm_ffn, norm_kv, norm_final, w_in_a, w_gla_gate2, b_gla_gate2,
           gla_onorm, w_in_b, w_kv, b_forget, w_mem_kv, w_out, w_ff_gate, w_ff_up, w_ff_down, w_router,
           w_moe_gate, w_moe_up, w_moe_down):
    bsz, seq, _ = x.shape
    n_mem = mem.shape[1]
    t = bsz * seq
    tm = 512
    xt = x.reshape(t, D_MODEL)
    memt = mem.reshape(bsz * n_mem, D_MODEL)
    gain = lambda g: g.reshape(1, D_MODEL)
    kw = GLA_HEADS * GLA_DK

    def mem_kv(layer):
        wk, wv = w_mem_kv[layer, :, :MEM_WIDTH], w_mem_kv[layer, :, MEM_WIDTH:]
        mk, mv = _norm_proj(memt, gain(norm_mem[layer]), [wk.astype(BF16), wv.astype(BF16)], (1.0, 1.0),
                            n_mem, "mem_kv_proj")
        return mk.reshape(bsz, n_mem, MEM_WIDTH), mv.reshape(bsz, n_mem, MEM_WIDTH)

    wa = w_in_a[0]
    wq = _pad_heads(wa[:, :kw], GLA_HEADS, GLA_DK, GLA_DK_PAD).astype(BF16)
    wk = _pad_heads(wa[:, kw:2 * kw], GLA_HEADS, GLA_DK, GLA_DK_PAD).astype(BF16)
    off = 2 * kw
    wv = _pad_heads(wa[:, off:off + MAIN_WIDTH], GLA_HEADS, GLA_DV, GLA_DV_PAD).astype(BF16)
    off += MAIN_WIDTH
    wr = _pad_heads(wa[:, off:off + MAIN_WIDTH], GLA_HEADS, GLA_DV, GLA_DV_PAD).astype(BF16)
    off += MAIN_WIDTH
    wg1 = _pad_cols(wa[:, off:off + GLA_GATE_RANK], LANES).astype(BF16)
    off += GLA_GATE_RANK
    wmq = wa[:, off:off + MEM_WIDTH].astype(BF16)
    wg2 = jnp.pad(_pad_heads(w_gla_gate2[0], GLA_HEADS, GLA_DK, GLA_DK_PAD), ((0, LANES - GLA_GATE_RANK), (0, 0)))
    bg2 = _pad_heads(b_gla_gate2[0], GLA_HEADS, GLA_DK, GLA_DK_PAD).reshape(1, -1)
    onorm = _pad_cols(gla_onorm[0], GLA_DV_PAD).reshape(1, GLA_DV_PAD)

    q, k, v, r, mq, la = _in_a(xt, gain(norm_mix[0]), wq, wk, wv, wr, wmq, wg1, wg2, bg2, tm)
    main = _gla(q, k, v, r, la, onorm, bsz, seq, tm)
    mk, mv = mem_kv(0)
    wo = w_out[0]
    wo_main = jnp.pad(wo[:MAIN_WIDTH].reshape(GLA_HEADS, GLA_DV, D_MODEL),
                      ((0, 0), (0, GLA_DV_PAD - GLA_DV), (0, 0))).reshape(GLA_HEADS * GLA_DV_PAD, D_MODEL)
    xt = _out_proj(xt, main, mq, mk, mv, wo_main.astype(BF16), wo[MAIN_WIDTH:].astype(BF16), seq, tm)
    xt = _ffn(xt, gain(norm_ffn[0]), w_ff_gate[0].astype(BF16), w_ff_up[0].astype(BF16),
              w_ff_down[0].astype(BF16), tm)

    wf = _pad_cols(w_kv[:, 2 * MAIN_WIDTH:], LANES).astype(BF16)
    bf = _pad_cols(b_forget, LANES).reshape(1, LANES)
    k_sh, v_sh, c, ct = _kv_proj(xt, gain(norm_kv), w_kv[:, :MAIN_WIDTH].astype(BF16),
                                 w_kv[:, MAIN_WIDTH:2 * MAIN_WIDTH].astype(BF16), wf, bf, bsz, seq, tm)

    wb = w_in_b[0]
    qf, mq = _norm_proj(xt, gain(norm_mix[1]), [wb[:, :MAIN_WIDTH].astype(BF16), wb[:, MAIN_WIDTH:].astype(BF16)],
                        (FOX_HEAD_DIM ** -0.5, 1.0), tm, "in_proj_fox")
    main = _fox(qf, k_sh, v_sh, c, ct, bsz, seq, 256, 1024)
    mk, mv = mem_kv(1)
    wo = w_out[1]
    xt = _out_proj(xt, main, mq, mk, mv, wo[:MAIN_WIDTH].astype(BF16), wo[MAIN_WIDTH:].astype(BF16), seq, tm)
    out = _moe(xt, gain(norm_ffn[1]), gain(norm_final), _pad_cols(w_router[0], LANES), w_moe_gate[0].astype(BF16),
               w_moe_up[0].astype(BF16), w_moe_down[0].astype(BF16), tm)
    return out.reshape(bsz, seq, D_MODEL)
```

```python
import functools

import jax
import jax.numpy as jnp
from jax import lax
from jax.experimental import pallas as pl
from jax.experimental.pallas import tpu as pltpu

D_MODEL = 1024
EPS = 1e-6
GLA_CHUNK = 64
GLA_SUB = 16
MEM_HEADS = 4
MEM_HEAD_DIM = 64
MEM_WIDTH = MEM_HEADS * MEM_HEAD_DIM
MAIN_WIDTH = D_MODEL - MEM_WIDTH
GLA_HEADS = 4
GLA_DK = MAIN_WIDTH // 2 // GLA_HEADS
GLA_DV = MAIN_WIDTH // GLA_HEADS
GLA_DK_PAD = 128
GLA_DV_PAD = 256
GLA_GATE_RANK = 16
GLA_GATE_TAU = 16.0
FOX_HEADS = 12
FOX_HEAD_DIM = 64
D_FF = 2816
N_EXPERTS = 8
LANES = 128
FF_CHUNK = 256
VMEM_LIMIT = 56 * 1024 * 1024

BF16 = jnp.bfloat16
F32 = jnp.float32


def _params(n_axes, vmem=VMEM_LIMIT):
    return pltpu.CompilerParams(dimension_semantics=("arbitrary",) * n_axes, vmem_limit_bytes=vmem)


def _rms_normed(x, gain):
    ms = jnp.mean(x * x, axis=-1, keepdims=True)
    return x * lax.rsqrt(ms + EPS) * gain


def _log_sigmoid(z):
    return jnp.minimum(z, 0.0) - jnp.log(1.0 + jnp.exp(-jnp.abs(z)))


def _silu(z):
    return z / (1.0 + jnp.exp(-z))


def _dot(a, b):
    return jnp.dot(a, b, preferred_element_type=F32)


def _dot_nt(a, b):
    return lax.dot_general(a, b, (((1,), (1,)), ((), ())), preferred_element_type=F32)


def _dot_tn(a, b):
    return lax.dot_general(a, b, (((0,), (0,)), ((), ())), preferred_element_type=F32)


def _tril_ones(n):
    r = lax.broadcasted_iota(jnp.int32, (n, n), 0)
    c = lax.broadcasted_iota(jnp.int32, (n, n), 1)
    return (c <= r).astype(F32)


def _full(shape):
    return pl.BlockSpec(shape, lambda *_: (0,) * len(shape))


def _in_a_kernel(x_ref, g_ref, wq_ref, wk_ref, wv_ref, wr_ref, wmq_ref, wg1_ref, wg2_ref, bg2_ref,
                 q_ref, k_ref, v_ref, r_ref, mq_ref, la_ref):
    hn = _rms_normed(x_ref[...], g_ref[...]).astype(BF16)
    q_ref[...] = (_dot(hn, wq_ref[...]) * (GLA_DK ** -0.5)).astype(BF16)
    k_ref[...] = _dot(hn, wk_ref[...]).astype(BF16)
    v_ref[...] = _dot(hn, wv_ref[...]).astype(BF16)
    r_ref[...] = _dot(hn, wr_ref[...]).astype(BF16)
    mq_ref[...] = _dot(hn, wmq_ref[...]).astype(BF16)
    glr = _dot(hn, wg1_ref[...])
    z = jnp.dot(glr, wg2_ref[...], preferred_element_type=F32, precision=lax.Precision.HIGHEST)
    la_ref[...] = _log_sigmoid(z + bg2_ref[...]) * (1.0 / GLA_GATE_TAU)


def _in_a(x, gain, wq, wk, wv, wr, wmq, wg1, wg2, bg2, tm):
    t = x.shape[0]
    row = lambda n: pl.BlockSpec((tm, n), lambda i: (i, 0))
    kw, vw = GLA_HEADS * GLA_DK_PAD, GLA_HEADS * GLA_DV_PAD
    return pl.pallas_call(
        _in_a_kernel,
        grid=(t // tm,),
        in_specs=[row(D_MODEL), _full((1, D_MODEL)), _full(wq.shape), _full(wk.shape), _full(wv.shape),
                  _full(wr.shape), _full(wmq.shape), _full(wg1.shape), _full(wg2.shape), _full(bg2.shape)],
        out_specs=[row(kw), row(kw), row(vw), row(vw), row(MEM_WIDTH), row(kw)],
        out_shape=[jax.ShapeDtypeStruct((t, kw), BF16), jax.ShapeDtypeStruct((t, kw), BF16),
                   jax.ShapeDtypeStruct((t, vw), BF16), jax.ShapeDtypeStruct((t, vw), BF16),
                   jax.ShapeDtypeStruct((t, MEM_WIDTH), BF16), jax.ShapeDtypeStruct((t, kw), F32)],
        compiler_params=_params(1),
        name="in_proj_gla",
    )(x, gain, wq, wk, wv, wr, wmq, wg1, wg2, bg2)


def _norm_proj_kernel(n_out, scales, x_ref, g_ref, *refs):
    w_refs, o_refs = refs[:n_out], refs[n_out:]
    hn = _rms_normed(x_ref[...], g_ref[...]).astype(BF16)
    for w_ref, o_ref, s in zip(w_refs, o_refs, scales):
        o = _dot(hn, w_ref[...])
        if s != 1.0:
            o = o * s
        o_ref[...] = o.astype(o_ref.dtype)


def _norm_proj(x, gain, weights, scales, tm, name):
    t = x.shape[0]
    row = lambda n: pl.BlockSpec((tm, n), lambda i: (i, 0))
    return pl.pallas_call(
        functools.partial(_norm_proj_kernel, len(weights), scales),
        grid=(t // tm,),
        in_specs=[row(D_MODEL), _full((1, D_MODEL))] + [_full(w.shape) for w in weights],
        out_specs=[row(w.shape[1]) for w in weights],
        out_shape=[jax.ShapeDtypeStruct((t, w.shape[1]), BF16) for w in weights],
        compiler_params=_params(1),
        name=name,
    )(x, gain, *weights)


FOX_BLOCK = LANES
FOX_WIDTH = FOX_HEADS * FOX_BLOCK
LOG2E = 1.4426950408889634


def _bf16_terms(v):
    hi = v.astype(BF16).astype(F32)
    mid = (v - hi).astype(BF16).astype(F32)
    return hi, mid, v - hi - mid


def _fox_bias_lanes(c, query_side):
    rows = c.shape[0]
    lane = lax.broadcasted_iota(jnp.int32, (rows, LANES), 1)
    blocks = []
    for h in range(FOX_HEADS):
        c2 = jnp.sum(jnp.where(lane == h, c, 0.0), axis=-1, keepdims=True) * LOG2E
        terms = _bf16_terms(c2 if query_side else -c2)
        base = FOX_HEAD_DIM if query_side else FOX_HEAD_DIM + 3
        ones = FOX_HEAD_DIM + 3 if query_side else FOX_HEAD_DIM
        blk = jnp.where((lane >= ones) & (lane < ones + 3), 1.0, 0.0)
        for n, term in enumerate(terms):
            blk = jnp.where(lane == base + n, term, blk)
        blocks.append(blk)
    return jnp.concatenate(blocks, axis=-1)


def _kv_kernel(tiles_per_seq, x_ref, g_ref, wk_ref, wv_ref, wf_ref, bf_ref, k_ref, v_ref, c_ref, carry_ref):
    @pl.when(pl.program_id(0) % tiles_per_seq == 0)
    def _():
        carry_ref[...] = jnp.zeros_like(carry_ref)

    hn = _rms_normed(x_ref[...], g_ref[...]).astype(BF16)
    log_f = _log_sigmoid(_dot(hn, wf_ref[...]) + bf_ref[...])
    tm = log_f.shape[0]
    c = carry_ref[...] + jnp.dot(_tril_ones(tm), log_f, preferred_element_type=F32,
                                 precision=lax.Precision.HIGHEST)
    carry_ref[...] = c[tm - 1:tm, :]
    c_ref[...] = c
    k_ref[...] = (_dot(hn, wk_ref[...]) + _fox_bias_lanes(c, query_side=False)).astype(BF16)
    lane = lax.broadcasted_iota(jnp.int32, (tm, FOX_WIDTH), 1)
    v_ref[...] = jnp.where(lane % FOX_BLOCK == FOX_HEAD_DIM, 1.0, _dot(hn, wv_ref[...])).astype(BF16)


def _kv_proj(x, gain, wk, wv, wf, bf, seq, tm):
    t = x.shape[0]
    row = lambda n: pl.BlockSpec((tm, n), lambda i: (i, 0))
    return pl.pallas_call(
        functools.partial(_kv_kernel, seq // tm),
        grid=(t // tm,),
        in_specs=[row(D_MODEL), _full((1, D_MODEL)), _full(wk.shape), _full(wv.shape), _full(wf.shape),
                  _full(bf.shape)],
        out_specs=[row(FOX_WIDTH), row(FOX_WIDTH), row(LANES)],
        out_shape=[jax.ShapeDtypeStruct((t, FOX_WIDTH), BF16), jax.ShapeDtypeStruct((t, FOX_WIDTH), BF16),
                   jax.ShapeDtypeStruct((t, LANES), F32)],
        scratch_shapes=[pltpu.VMEM((1, LANES), F32)],
        compiler_params=_params(1),
        name="kv_proj_fox",
    )(x, gain, wk, wv, wf, bf)


def _in_b_kernel(x_ref, g_ref, c_ref, wq_ref, wmq_ref, q_ref, mq_ref):
    hn = _rms_normed(x_ref[...], g_ref[...]).astype(BF16)
    q = _dot(hn, wq_ref[...]) * (FOX_HEAD_DIM ** -0.5 * LOG2E)
    q_ref[...] = (q + _fox_bias_lanes(c_ref[...], query_side=True)).astype(BF16)
    mq_ref[...] = _dot(hn, wmq_ref[...]).astype(BF16)


def _in_b(x, gain, c, wq, wmq, tm):
    t = x.shape[0]
    row = lambda n: pl.BlockSpec((tm, n), lambda i: (i, 0))
    return pl.pallas_call(
        _in_b_kernel,
        grid=(t // tm,),
        in_specs=[row(D_MODEL), _full((1, D_MODEL)), row(LANES), _full(wq.shape), _full(wmq.shape)],
        out_specs=[row(FOX_WIDTH), row(MEM_WIDTH)],
        out_shape=[jax.ShapeDtypeStruct((t, FOX_WIDTH), BF16), jax.ShapeDtypeStruct((t, MEM_WIDTH), BF16)],
        compiler_params=_params(1),
        name="in_proj_fox",
    )(x, gain, c, wq, wmq)


def _gla_kernel(chunks, q_ref, k_ref, v_ref, r_ref, la_ref, on_ref, o_ref, st_ref):
    @pl.when(pl.program_id(1) == 0)
    def _():
        st_ref[...] = jnp.zeros_like(st_ref)

    tril = _tril_ones(GLA_CHUNK)
    n_sub = GLA_CHUNK // GLA_SUB
    masks = []
    for i in range(n_sub):
        n_keys = GLA_SUB * (i + 1)
        rr = lax.broadcasted_iota(jnp.int32, (GLA_SUB, n_keys), 0) + GLA_SUB * i
        cc = lax.broadcasted_iota(jnp.int32, (GLA_SUB, n_keys), 1)
        masks.append(cc <= rr)

    def chunk_body(c, carry):
        r0 = pl.multiple_of(c * GLA_CHUNK, GLA_CHUNK)
        rows = pl.ds(r0, GLA_CHUNK)
        b_all = jnp.dot(tril, la_ref[rows, :], preferred_element_type=F32, precision=lax.Precision.HIGHEST)
        for h in range(GLA_HEADS):
            ks = slice(h * GLA_DK_PAD, (h + 1) * GLA_DK_PAD)
            vs = slice(h * GLA_DV_PAD, (h + 1) * GLA_DV_PAD)
            q = q_ref[rows, ks].astype(F32)
            k = k_ref[rows, ks].astype(F32)
            v = v_ref[rows, vs]
            b = b_all[:, ks]
            b_last = b[GLA_CHUNK - 1:GLA_CHUNK, :]
            st = st_ref[h]
            o_inter = _dot_nt((q * jnp.exp(b)).astype(BF16), st.astype(BF16))
            o_parts = []
            for i in range(n_sub):
                lo, hi = GLA_SUB * i, GLA_SUB * (i + 1)
                b_i = b[lo:hi, :]
                qd = q[lo:hi, :] * jnp.exp(b_i - b[lo - 1:lo, :]) if i else q[lo:hi, :] * jnp.exp(b_i)
                kd = k[:hi, :] * jnp.exp(b[lo - 1:lo, :] - b[:hi, :]) if i else k[:hi, :] * jnp.exp(-b[:hi, :])
                a = jnp.where(masks[i], _dot_nt(qd.astype(BF16), kd.astype(BF16)), 0.0)
                o_parts.append(_dot(a.astype(BF16), v[:hi, :]))
            o = o_inter + jnp.concatenate(o_parts, axis=0)
            kdec = (k * jnp.exp(b_last - b)).astype(BF16)
            st_ref[h] = st * jnp.exp(b_last) + _dot_tn(v, kdec)
            ms = jnp.sum(o * o, axis=-1, keepdims=True) * (1.0 / GLA_DV)
            y = o * lax.rsqrt(ms + EPS) * on_ref[...] * _silu(r_ref[rows, vs].astype(F32))
            o_ref[rows, vs] = y.astype(BF16)
        return carry

    lax.fori_loop(0, chunks, chunk_body, 0, unroll=2)


def _gla(q, k, v, r, la, onorm, bsz, seq, tm):
    t = q.shape[0]
    tiles_per_seq = seq // tm
    kw, vw = GLA_HEADS * GLA_DK_PAD, GLA_HEADS * GLA_DV_PAD
    row = lambda n: pl.BlockSpec((tm, n), lambda b, i: (b * tiles_per_seq + i, 0))
    return pl.pallas_call(
        functools.partial(_gla_kernel, tm // GLA_CHUNK),
        grid=(bsz, tiles_per_seq),
        in_specs=[row(kw), row(kw), row(vw), row(vw), row(kw), pl.BlockSpec((1, GLA_DV_PAD), lambda b, i: (0, 0))],
        out_specs=row(vw),
        out_shape=jax.ShapeDtypeStruct((t, vw), BF16),
        scratch_shapes=[pltpu.VMEM((GLA_HEADS, GLA_DV_PAD, GLA_DK_PAD), F32)],
        compiler_params=_params(2),
        name="gla_scan",
    )(q, k, v, r, la, onorm)


def _fox_kernel(tq, tk, q_ref, k_ref, v_ref, o_ref):
    i = pl.program_id(2)
    q = q_ref[...]
    qs = (q[:, :FOX_BLOCK], q[:, FOX_BLOCK:])
    q_pos = i * tq + lax.broadcasted_iota(jnp.int32, (tq, tk), 0)
    k_off = lax.broadcasted_iota(jnp.int32, (tq, tk), 1)

    blks = (slice(0, FOX_BLOCK), slice(FOX_BLOCK, 2 * FOX_BLOCK))
    key_rows = lambda j: pl.ds(pl.multiple_of(j * tk, tk), tk)

    def step(j, state, masked):
        new = []
        for hh in range(2):
            m, acc = state[hh]
            s = _dot_nt(qs[hh], k_ref[key_rows(j), blks[hh]])
            if masked:
                s = jnp.where(k_off + j * tk <= q_pos, s, -jnp.inf)
            m_new = jnp.maximum(m, jnp.max(s, axis=-1, keepdims=True))
            p = jnp.exp2(s - m_new).astype(BF16)
            acc = jnp.exp2(m - m_new) * acc + _dot(p, v_ref[key_rows(j), blks[hh]])
            new.append((m_new, acc))
        return tuple(new)

    init = (jnp.full((tq, 1), -jnp.inf, F32), jnp.zeros((tq, FOX_BLOCK), F32))
    n_before = (i * tq) // tk
    state = lax.fori_loop(0, n_before, functools.partial(step, masked=False), (init, init))
    (_, acc0), (_, acc1) = step(n_before, state, masked=True)
    lane = lax.broadcasted_iota(jnp.int32, (tq, FOX_BLOCK), 1)
    row_sum = lambda acc: jnp.sum(jnp.where(lane == FOX_HEAD_DIM, acc, 0.0), axis=-1, keepdims=True)
    o0, o1 = acc0 / row_sum(acc0), acc1 / row_sum(acc1)
    o_ref[...] = jnp.where(lane < FOX_HEAD_DIM, o0, pltpu.roll(o1, FOX_HEAD_DIM, axis=1)).astype(BF16)


def _fox(q, k, v, bsz, seq, tq, tk):
    t = q.shape[0]
    nq = seq // tq
    pairs = FOX_HEADS // 2
    return pl.pallas_call(
        functools.partial(_fox_kernel, tq, tk),
        grid=(bsz, pairs, nq),
        in_specs=[pl.BlockSpec((tq, 2 * FOX_BLOCK), lambda b, p, i: (b * nq + i, p)),
                  pl.BlockSpec((seq, 2 * FOX_BLOCK), lambda b, p, i: (b, p)),
                  pl.BlockSpec((seq, 2 * FOX_BLOCK), lambda b, p, i: (b, p))],
        out_specs=pl.BlockSpec((tq, LANES), lambda b, p, i: (b * nq + i, p)),
        out_shape=jax.ShapeDtypeStruct((t, MAIN_WIDTH), BF16),
        compiler_params=_params(3),
        name="fox_attention",
    )(q, k, v)


def _out_kernel(x_ref, main_ref, mq_ref, mk_ref, mv_ref, wo_main_ref, wo_mem_ref, o_ref):
    mq, mk, mv = mq_ref[...], mk_ref[...], mv_ref[...]
    heads = []
    for h in range(MEM_HEADS):
        hs = slice(h * MEM_HEAD_DIM, (h + 1) * MEM_HEAD_DIM)
        s = _dot_nt(mq[:, hs], mk[:, hs]) * (MEM_HEAD_DIM ** -0.5)
        p = jnp.exp(s - jnp.max(s, axis=-1, keepdims=True))
        l = jnp.sum(p, axis=-1, keepdims=True)
        heads.append(_dot(p.astype(BF16), mv[:, hs]) / l)
    mem_o = jnp.concatenate(heads, axis=-1).astype(BF16)
    o_ref[...] = x_ref[...] + _dot(main_ref[...], wo_main_ref[...]) + _dot(mem_o, wo_mem_ref[...])


def _out_proj(x, main, mq, mem_k, mem_v, wo_main, wo_mem, seq, tm):
    t = x.shape[0]
    tiles_per_seq = seq // tm
    n_mem = mem_k.shape[1]
    row = lambda n: pl.BlockSpec((tm, n), lambda i: (i, 0))
    mem = pl.BlockSpec((None, n_mem, MEM_WIDTH), lambda i: (i // tiles_per_seq, 0, 0))
    return pl.pallas_call(
        _out_kernel,
        grid=(t // tm,),
        in_specs=[row(D_MODEL), row(main.shape[1]), row(MEM_WIDTH), mem, mem, _full(wo_main.shape),
                  _full(wo_mem.shape)],
        out_specs=row(D_MODEL),
        out_shape=jax.ShapeDtypeStruct((t, D_MODEL), F32),
        compiler_params=_params(1),
        name="out_proj_mem_attn",
    )(x, main, mq, mem_k, mem_v, wo_main, wo_mem)


def _swiglu_acc(hn, wg_ref, wu_ref, wd_ref, acc_ref, row_scale=None):
    for j in range(D_FF // FF_CHUNK):
        cs = slice(j * FF_CHUNK, (j + 1) * FF_CHUNK)
        a = _silu(_dot(hn, wg_ref[:, cs])) * _dot(hn, wu_ref[:, cs])
        if row_scale is not None:
            a = a * row_scale
        acc_ref[...] += _dot(a.astype(BF16), wd_ref[cs, :])


def _ffn_kernel(x_ref, g_ref, wg_ref, wu_ref, wd_ref, o_ref):
    x = x_ref[...]
    o_ref[...] = x
    _swiglu_acc(_rms_normed(x, g_ref[...]).astype(BF16), wg_ref, wu_ref, wd_ref, o_ref)


def _ffn(x, gain, wg, wu, wd, tm):
    t = x.shape[0]
    row = pl.BlockSpec((tm, D_MODEL), lambda i: (i, 0))
    return pl.pallas_call(
        _ffn_kernel,
        grid=(t // tm,),
        in_specs=[row, _full((1, D_MODEL)), _full(wg.shape), _full(wu.shape), _full(wd.shape)],
        out_specs=row,
        out_shape=jax.ShapeDtypeStruct((t, D_MODEL), F32),
        compiler_params=_params(1),
        name="dense_swiglu",
    )(x, gain, wg, wu, wd)


ROUTE_E1, ROUTE_E2, ROUTE_W1, ROUTE_W2, ROUTE_RANK1, ROUTE_RANK2 = range(6)


def _router_kernel(x_ref, g_ref, wr_ref, hn_ref, route_ref, counts_ref, carry_ref):
    @pl.when(pl.program_id(0) == 0)
    def _():
        carry_ref[...] = jnp.zeros_like(carry_ref)

    hn = _rms_normed(x_ref[...], g_ref[...])
    hn_ref[...] = hn
    logits = jnp.dot(hn, wr_ref[...], preferred_element_type=F32, precision=lax.Precision.HIGHEST)
    tm = logits.shape[0]
    lane = lax.broadcasted_iota(jnp.int32, logits.shape, 1)
    logits = jnp.where(lane < N_EXPERTS, logits, -jnp.inf)
    m1 = jnp.max(logits, axis=-1, keepdims=True)
    i1 = jnp.min(jnp.where(logits == m1, lane, LANES), axis=-1, keepdims=True)
    rest = jnp.where(lane == i1, -jnp.inf, logits)
    m2 = jnp.max(rest, axis=-1, keepdims=True)
    i2 = jnp.min(jnp.where(rest == m2, lane, LANES), axis=-1, keepdims=True)
    e2 = jnp.exp(m2 - m1)
    w1 = 1.0 / (1.0 + e2)
    chosen = jnp.where((lane == i1) | (lane == i2), 1.0, 0.0)
    rr = lax.broadcasted_iota(jnp.int32, (tm, tm), 0)
    cc = lax.broadcasted_iota(jnp.int32, (tm, tm), 1)
    before = jnp.where(cc < rr, 1.0, 0.0).astype(BF16)
    base = carry_ref[...] + _dot(before, chosen.astype(BF16))
    rank1 = jnp.sum(jnp.where(lane == i1, base, 0.0), axis=-1, keepdims=True)
    rank2 = jnp.sum(jnp.where(lane == i2, base, 0.0), axis=-1, keepdims=True)
    carry_ref[...] += jnp.sum(chosen, axis=0, keepdims=True)
    counts_ref[...] = carry_ref[...]
    rec = jnp.zeros_like(logits)
    for slot, val in ((ROUTE_E1, i1.astype(F32)), (ROUTE_E2, i2.astype(F32)), (ROUTE_W1, w1), (ROUTE_W2, e2 * w1),
                      (ROUTE_RANK1, rank1), (ROUTE_RANK2, rank2)):
        rec = jnp.where(lane == slot, val, rec)
    route_ref[...] = rec


def _router(x, gain, wr, tm):
    t = x.shape[0]
    row = lambda n: pl.BlockSpec((tm, n), lambda i: (i, 0))
    return pl.pallas_call(
        _router_kernel,
        grid=(t // tm,),
        in_specs=[row(D_MODEL), _full((1, D_MODEL)), _full(wr.shape)],
        out_specs=[row(D_MODEL), row(LANES), _full((1, LANES))],
        out_shape=[jax.ShapeDtypeStruct((t, D_MODEL), F32), jax.ShapeDtypeStruct((t, LANES), F32),
                   jax.ShapeDtypeStruct((1, LANES), F32)],
        scratch_shapes=[pltpu.VMEM((1, LANES), F32)],
        compiler_params=_params(1),
        name="moe_router",
    )(x, gain, wr)


def _row_copy(src, src_row, dst, dst_row, sem):
    return pltpu.make_async_copy(src.at[pl.ds(src_row, 1)], dst.at[pl.ds(dst_row, 1)], sem)


def _dispatch_kernel(tm, cnt_ref, pad_ref, off_ref, pos1_ref, pos2_ref, hn_ref, xs_ref, zero_ref, sem, pad_sem):
    @pl.when(pl.program_id(0) == 0)
    def _():
        zero_ref[...] = jnp.zeros_like(zero_ref)
        for e in range(N_EXPERTS + 1):
            lo, hi = off_ref[e] + cnt_ref[e], off_ref[e] + pad_ref[e]

            def fill(r, c):
                _row_copy(zero_ref, 0, xs_ref, r, pad_sem).start()
                return c

            def drain(r, c):
                _row_copy(zero_ref, 0, xs_ref, r, pad_sem).wait()
                return c

            lax.fori_loop(lo, hi, fill, 0)
            lax.fori_loop(lo, hi, drain, 0)

    def issue(r, c):
        _row_copy(hn_ref, r, xs_ref, pos1_ref[0, r], sem).start()
        _row_copy(hn_ref, r, xs_ref, pos2_ref[0, r], sem).start()
        return c

    lax.fori_loop(0, tm, issue, 0, unroll=8)
    for _ in range(2):
        pltpu.make_async_copy(hn_ref, xs_ref.at[pl.ds(0, tm)], sem).wait()


def _dispatch(hn, pos1, pos2, counts, padded, offsets, n_rows, tm):
    t = hn.shape[0]
    nt = t // tm
    smem_row = pl.BlockSpec((None, 1, tm), lambda i, *_: (i, 0, 0), memory_space=pltpu.SMEM)
    return pl.pallas_call(
        functools.partial(_dispatch_kernel, tm),
        grid_spec=pltpu.PrefetchScalarGridSpec(
            num_scalar_prefetch=3,
            grid=(nt,),
            in_specs=[smem_row, smem_row, pl.BlockSpec((tm, D_MODEL), lambda i, *_: (i, 0))],
            out_specs=pl.BlockSpec(memory_space=pl.ANY),
            scratch_shapes=[pltpu.VMEM((8, D_MODEL), F32), pltpu.SemaphoreType.DMA, pltpu.SemaphoreType.DMA],
        ),
        out_shape=jax.ShapeDtypeStruct((n_rows, D_MODEL), F32),
        compiler_params=_params(1),
        name="moe_dispatch",
    )(counts, padded, offsets, pos1.reshape(nt, 1, tm), pos2.reshape(nt, 1, tm), hn)


def _experts_kernel(tile_expert_ref, n_used_ref, xs_ref, wg_ref, wu_ref, wd_ref, o_ref):
    o_ref[...] = jnp.zeros_like(o_ref)

    @pl.when(pl.program_id(0) < n_used_ref[0])
    def _():
        _swiglu_acc(xs_ref[...].astype(BF16), wg_ref, wu_ref, wd_ref, o_ref)


def _experts(xs, tile_expert, n_used, wg, wu, wd, tm):
    n_rows = xs.shape[0]
    wspec = lambda shape: pl.BlockSpec(
        (None,) + shape, lambda i, te, nu: (te[jnp.minimum(i, nu[0] - 1)], 0, 0))
    return pl.pallas_call(
        _experts_kernel,
        grid_spec=pltpu.PrefetchScalarGridSpec(
            num_scalar_prefetch=2,
            grid=(n_rows // tm,),
            in_specs=[pl.BlockSpec((tm, D_MODEL), lambda i, te, nu: (i, 0)),
                      wspec((D_MODEL, D_FF)), wspec((D_MODEL, D_FF)), wspec((D_FF, D_MODEL))],
            out_specs=pl.BlockSpec((tm, D_MODEL), lambda i, te, nu: (i, 0)),
        ),
        out_shape=jax.ShapeDtypeStruct((n_rows, D_MODEL), F32),
        compiler_params=_params(1),
        name="moe_experts",
    )(tile_expert, n_used, xs, wg, wu, wd)


def _combine_kernel(tm, pos1_ref, pos2_ref, x_ref, route_ref, gf_ref, y_ref, o_ref, buf_ref, sem):
    def issue(r, c):
        _row_copy(y_ref, pos1_ref[0, r], buf_ref.at[0], r, sem).start()
        _row_copy(y_ref, pos2_ref[0, r], buf_ref.at[1], r, sem).start()
        return c

    lax.fori_loop(0, tm, issue, 0, unroll=8)
    route = route_ref[...]
    lane = lax.broadcasted_iota(jnp.int32, route.shape, 1)
    w1 = jnp.sum(jnp.where(lane == ROUTE_W1, route, 0.0), axis=-1, keepdims=True)
    w2 = jnp.sum(jnp.where(lane == ROUTE_W2, route, 0.0), axis=-1, keepdims=True)
    for k in range(2):
        pltpu.make_async_copy(y_ref.at[pl.ds(0, tm)], buf_ref.at[k], sem).wait()
    o_ref[...] = _rms_normed(x_ref[...] + w1 * buf_ref[0] + w2 * buf_ref[1], gf_ref[...])


def _combine(x, route, pos1, pos2, y, gain_final, tm):
    t = x.shape[0]
    nt = t // tm
    smem_row = pl.BlockSpec((None, 1, tm), lambda i: (i, 0, 0), memory_space=pltpu.SMEM)
    row = lambda n: pl.BlockSpec((tm, n), lambda i: (i, 0))
    return pl.pallas_call(
        functools.partial(_combine_kernel, tm),
        grid=(nt,),
        in_specs=[smem_row, smem_row, row(D_MODEL), row(LANES), _full((1, D_MODEL)),
                  pl.BlockSpec(memory_space=pl.ANY)],
        out_specs=row(D_MODEL),
        out_shape=jax.ShapeDtypeStruct((t, D_MODEL), F32),
        scratch_shapes=[pltpu.VMEM((2, tm, D_MODEL), F32), pltpu.SemaphoreType.DMA],
        compiler_params=_params(1),
        name="moe_combine",
    )(pos1.reshape(nt, 1, tm), pos2.reshape(nt, 1, tm), x, route, gain_final, y)


def _moe(x, gain, gain_final, wr, wg, wu, wd, tm):
    t = x.shape[0]
    hn, route, counts = _router(x, gain, wr, tm)
    col = lambda c: route[:, c].astype(jnp.int32)
    counts = counts[0, :N_EXPERTS].astype(jnp.int32)
    padded = (counts + tm - 1) // tm * tm
    ends = jnp.cumsum(padded)
    offsets = ends - padded
    pos1 = offsets[col(ROUTE_E1)] + col(ROUTE_RANK1)
    pos2 = offsets[col(ROUTE_E2)] + col(ROUTE_RANK2)
    n_rows = 2 * t + N_EXPERTS * tm
    tile_start = jnp.arange(n_rows // tm, dtype=jnp.int32) * tm
    tile_expert = jnp.minimum(jnp.sum(tile_start[:, None] >= ends[None, :], axis=1), N_EXPERTS - 1).astype(jnp.int32)
    n_used = (ends[-1:] // tm).astype(jnp.int32)
    tail = lambda a, v: jnp.concatenate([a, v.astype(jnp.int32)])
    xs = _dispatch(hn, pos1, pos2, tail(counts, jnp.zeros((1,))), tail(padded, n_rows - ends[-1:]),
                   tail(offsets, ends[-1:]), n_rows, tm)
    y = _experts(xs, tile_expert, n_used, wg, wu, wd, tm)
    return _combine(x, route, pos1, pos2, y, gain_final, tm // 2)


def _pad_heads(w, heads, width, padded):
    lead = w.shape[:-1]
    w = w.reshape(lead + (heads, width))
    w = jnp.pad(w, [(0, 0)] * len(lead) + [(0, 0), (0, padded - width)])
    return w.reshape(lead + (heads * padded,))


def _pad_cols(w, n):
    return jnp.pad(w, [(0, 0)] * (w.ndim - 1) + [(0, n - w.shape[-1])])


def kernel(x, mem, norm_mix, norm_mem, norm_ffn, norm_kv, norm_final, w_in_a, w_gla_gate2, b_gla_gate2,
           gla_onorm, w_in_b, w_kv, b_forget, w_mem_kv, w_out, w_ff_gate, w_ff_up, w_ff_down, w_router,
           w_moe_gate, w_moe_up, w_moe_down):
    bsz, seq, _ = x.shape
    n_mem = mem.shape[1]
    t = bsz * seq
    tm = 512
    xt = x.reshape(t, D_MODEL)
    memt = mem.reshape(bsz * n_mem, D_MODEL)
    gain = lambda g: g.reshape(1, D_MODEL)
    kw = GLA_HEADS * GLA_DK

    def mem_kv(layer):
        wk, wv = w_mem_kv[layer, :, :MEM_WIDTH], w_mem_kv[layer, :, MEM_WIDTH:]
        mk, mv = _norm_proj(memt, gain(norm_mem[layer]), [wk.astype(BF16), wv.astype(BF16)], (1.0, 1.0),
                            n_mem, "mem_kv_proj")
        return mk.reshape(bsz, n_mem, MEM_WIDTH), mv.reshape(bsz, n_mem, MEM_WIDTH)

    wa = w_in_a[0]
    wq = _pad_heads(wa[:, :kw], GLA_HEADS, GLA_DK, GLA_DK_PAD).astype(BF16)
    wk = _pad_heads(wa[:, kw:2 * kw], GLA_HEADS, GLA_DK, GLA_DK_PAD).astype(BF16)
    off = 2 * kw
    wv = _pad_heads(wa[:, off:off + MAIN_WIDTH], GLA_HEADS, GLA_DV, GLA_DV_PAD).astype(BF16)
    off += MAIN_WIDTH
    wr = _pad_heads(wa[:, off:off + MAIN_WIDTH], GLA_HEADS, GLA_DV, GLA_DV_PAD).astype(BF16)
    off += MAIN_WIDTH
    wg1 = _pad_cols(wa[:, off:off + GLA_GATE_RANK], LANES).astype(BF16)
    off += GLA_GATE_RANK
    wmq = wa[:, off:off + MEM_WIDTH].astype(BF16)
    wg2 = jnp.pad(_pad_heads(w_gla_gate2[0], GLA_HEADS, GLA_DK, GLA_DK_PAD), ((0, LANES - GLA_GATE_RANK), (0, 0)))
    bg2 = _pad_heads(b_gla_gate2[0], GLA_HEADS, GLA_DK, GLA_DK_PAD).reshape(1, -1)
    onorm = _pad_cols(gla_onorm[0], GLA_DV_PAD).reshape(1, GLA_DV_PAD)

    q, k, v, r, mq, la = _in_a(xt, gain(norm_mix[0]), wq, wk, wv, wr, wmq, wg1, wg2, bg2, tm)
    main = _gla(q, k, v, r, la, onorm, bsz, seq, tm)
    mk, mv = mem_kv(0)
    wo = w_out[0]
    wo_main = jnp.pad(wo[:MAIN_WIDTH].reshape(GLA_HEADS, GLA_DV, D_MODEL),
                      ((0, 0), (0, GLA_DV_PAD - GLA_DV), (0, 0))).reshape(GLA_HEADS * GLA_DV_PAD, D_MODEL)
    xt = _out_proj(xt, main, mq, mk, mv, wo_main.astype(BF16), wo[MAIN_WIDTH:].astype(BF16), seq, tm)
    xt = _ffn(xt, gain(norm_ffn[0]), w_ff_gate[0].astype(BF16), w_ff_up[0].astype(BF16),
              w_ff_down[0].astype(BF16), tm)

    wf = _pad_cols(w_kv[:, 2 * MAIN_WIDTH:], LANES).astype(BF16)
    bf = _pad_cols(b_forget, LANES).reshape(1, LANES)
    fox_blocks = lambda w: _pad_heads(w, FOX_HEADS, FOX_HEAD_DIM, FOX_BLOCK).astype(BF16)
    k_sh, v_sh, c = _kv_proj(xt, gain(norm_kv), fox_blocks(w_kv[:, :MAIN_WIDTH]),
                             fox_blocks(w_kv[:, MAIN_WIDTH:2 * MAIN_WIDTH]), wf, bf, seq, tm)

    wb = w_in_b[0]
    qf, mq = _in_b(xt, gain(norm_mix[1]), c, fox_blocks(wb[:, :MAIN_WIDTH]), wb[:, MAIN_WIDTH:].astype(BF16), tm)
    main = _fox(qf, k_sh, v_sh, bsz, seq, 1024, 1024)
    mk, mv = mem_kv(1)
    wo = w_out[1]
    xt = _out_proj(xt, main, mq, mk, mv, wo[:MAIN_WIDTH].astype(BF16), wo[MAIN_WIDTH:].astype(BF16), seq, tm)
    out = _moe(xt, gain(norm_ffn[1]), gain(norm_final), _pad_cols(w_router[0], LANES), w_moe_gate[0].astype(BF16),
               w_moe_up[0].astype(BF16), w_moe_down[0].astype(BF16), tm)
    return out.reshape(bsz, seq, D_MODEL)
```

```python
import functools

import jax
import jax.numpy as jnp
from jax import lax
from jax.experimental import pallas as pl
from jax.experimental.pallas import tpu as pltpu

D_MODEL = 1024
EPS = 1e-6
GLA_CHUNK = 64
GLA_SUB = 16
MEM_HEADS = 4
MEM_HEAD_DIM = 64
MEM_WIDTH = MEM_HEADS * MEM_HEAD_DIM
MAIN_WIDTH = D_MODEL - MEM_WIDTH
GLA_HEADS = 4
GLA_DK = MAIN_WIDTH // 2 // GLA_HEADS
GLA_DV = MAIN_WIDTH // GLA_HEADS
GLA_DK_PAD = 128
GLA_DV_PAD = 256
GLA_GATE_RANK = 16
GLA_GATE_TAU = 16.0
FOX_HEADS = 12
FOX_HEAD_DIM = 64
D_FF = 2816
N_EXPERTS = 8
LANES = 128
FF_CHUNK = 256
VMEM_LIMIT = 56 * 1024 * 1024

BF16 = jnp.bfloat16
F32 = jnp.float32


def _params(n_axes, vmem=VMEM_LIMIT):
    return pltpu.CompilerParams(dimension_semantics=("arbitrary",) * n_axes, vmem_limit_bytes=vmem)


def _rms_normed(x, gain):
    ms = jnp.mean(x * x, axis=-1, keepdims=True)
    return x * lax.rsqrt(ms + EPS) * gain


def _log_sigmoid(z):
    return jnp.minimum(z, 0.0) - jnp.log(1.0 + jnp.exp(-jnp.abs(z)))


def _silu(z):
    return z / (1.0 + jnp.exp(-z))


def _dot(a, b):
    return jnp.dot(a, b, preferred_element_type=F32)


def _dot_nt(a, b):
    return lax.dot_general(a, b, (((1,), (1,)), ((), ())), preferred_element_type=F32)


def _dot_tn(a, b):
    return lax.dot_general(a, b, (((0,), (0,)), ((), ())), preferred_element_type=F32)


def _bf16_terms(v):
    hi = v.astype(BF16).astype(F32)
    mid = (v - hi).astype(BF16).astype(F32)
    return hi, mid, v - hi - mid


def _dot_3pass(a, b):
    a_hi, a_lo, _ = _bf16_terms(a)
    b_hi, b_lo, _ = _bf16_terms(b)
    a_hi, a_lo, b_hi, b_lo = (t.astype(BF16) for t in (a_hi, a_lo, b_hi, b_lo))
    return _dot(a_hi, b_hi) + _dot(a_hi, b_lo) + _dot(a_lo, b_hi)


def _cumsum_rows(x):
    n = x.shape[0]
    r = lax.broadcasted_iota(jnp.int32, (n, n), 0)
    c = lax.broadcasted_iota(jnp.int32, (n, n), 1)
    tril = jnp.where(c <= r, 1.0, 0.0).astype(BF16)
    hi, mid, lo = _bf16_terms(x)
    return _dot(tril, hi.astype(BF16)) + _dot(tril, mid.astype(BF16)) + _dot(tril, lo.astype(BF16))


def _full(shape):
    return pl.BlockSpec(shape, lambda *_: (0,) * len(shape))


def _in_a_kernel(x_ref, g_ref, wq_ref, wk_ref, wv_ref, wr_ref, wmq_ref, wg1_ref, wg2_ref, bg2_ref,
                 q_ref, k_ref, v_ref, r_ref, mq_ref, la_ref):
    hn = _rms_normed(x_ref[...], g_ref[...]).astype(BF16)
    q_ref[...] = (_dot(hn, wq_ref[...]) * (GLA_DK ** -0.5)).astype(BF16)
    k_ref[...] = _dot(hn, wk_ref[...]).astype(BF16)
    v_ref[...] = _dot(hn, wv_ref[...]).astype(BF16)
    r_ref[...] = _dot(hn, wr_ref[...]).astype(BF16)
    mq_ref[...] = _dot(hn, wmq_ref[...]).astype(BF16)
    glr = _dot(hn, wg1_ref[...])
    z = _dot_3pass(glr, wg2_ref[...])
    la_ref[...] = _log_sigmoid(z + bg2_ref[...]) * (1.0 / GLA_GATE_TAU)


def _in_a(x, gain, wq, wk, wv, wr, wmq, wg1, wg2, bg2, tm):
    t = x.shape[0]
    row = lambda n: pl.BlockSpec((tm, n), lambda i: (i, 0))
    kw, vw = GLA_HEADS * GLA_DK_PAD, GLA_HEADS * GLA_DV_PAD
    return pl.pallas_call(
        _in_a_kernel,
        grid=(t // tm,),
        in_specs=[row(D_MODEL), _full((1, D_MODEL)), _full(wq.shape), _full(wk.shape), _full(wv.shape),
                  _full(wr.shape), _full(wmq.shape), _full(wg1.shape), _full(wg2.shape), _full(bg2.shape)],
        out_specs=[row(kw), row(kw), row(vw), row(vw), row(MEM_WIDTH), row(kw)],
        out_shape=[jax.ShapeDtypeStruct((t, kw), BF16), jax.ShapeDtypeStruct((t, kw), BF16),
                   jax.ShapeDtypeStruct((t, vw), BF16), jax.ShapeDtypeStruct((t, vw), BF16),
                   jax.ShapeDtypeStruct((t, MEM_WIDTH), BF16), jax.ShapeDtypeStruct((t, kw), F32)],
        compiler_params=_params(1),
        name="in_proj_gla",
    )(x, gain, wq, wk, wv, wr, wmq, wg1, wg2, bg2)


def _norm_proj_kernel(n_out, scales, x_ref, g_ref, *refs):
    w_refs, o_refs = refs[:n_out], refs[n_out:]
    hn = _rms_normed(x_ref[...], g_ref[...]).astype(BF16)
    for w_ref, o_ref, s in zip(w_refs, o_refs, scales):
        o = _dot(hn, w_ref[...])
        if s != 1.0:
            o = o * s
        o_ref[...] = o.astype(o_ref.dtype)


def _norm_proj(x, gain, weights, scales, tm, name):
    t = x.shape[0]
    row = lambda n: pl.BlockSpec((tm, n), lambda i: (i, 0))
    return pl.pallas_call(
        functools.partial(_norm_proj_kernel, len(weights), scales),
        grid=(t // tm,),
        in_specs=[row(D_MODEL), _full((1, D_MODEL))] + [_full(w.shape) for w in weights],
        out_specs=[row(w.shape[1]) for w in weights],
        out_shape=[jax.ShapeDtypeStruct((t, w.shape[1]), BF16) for w in weights],
        compiler_params=_params(1),
        name=name,
    )(x, gain, *weights)


FOX_BLOCK = LANES
FOX_WIDTH = FOX_HEADS * FOX_BLOCK
LOG2E = 1.4426950408889634
FOX_DIAG_BANDS = 4


def _fox_bias_lanes(c, query_side):
    rows = c.shape[0]
    lane = lax.broadcasted_iota(jnp.int32, (rows, LANES), 1)
    blocks = []
    for h in range(FOX_HEADS):
        c2 = jnp.sum(jnp.where(lane == h, c, 0.0), axis=-1, keepdims=True) * LOG2E
        terms = _bf16_terms(c2 if query_side else -c2)
        base = FOX_HEAD_DIM if query_side else FOX_HEAD_DIM + 3
        ones = FOX_HEAD_DIM + 3 if query_side else FOX_HEAD_DIM
        blk = jnp.where((lane >= ones) & (lane < ones + 3), 1.0, 0.0)
        for n, term in enumerate(terms):
            blk = jnp.where(lane == base + n, term, blk)
        blocks.append(blk)
    return blocks


def _fox_blocks(dense, spare):
    lane = lax.broadcasted_iota(jnp.int32, (dense.shape[0], FOX_BLOCK), 1)
    blocks = []
    for h in range(FOX_HEADS):
        pair = dense[:, (h // 2) * FOX_BLOCK:(h // 2 + 1) * FOX_BLOCK]
        own = pair if h % 2 == 0 else pltpu.roll(pair, FOX_HEAD_DIM, axis=1)
        blocks.append(jnp.where(lane < FOX_HEAD_DIM, own, spare[h]))
    return jnp.concatenate(blocks, axis=-1)


def _kv_kernel(tiles_per_seq, x_ref, g_ref, wk_ref, wv_ref, wf_ref, bf_ref, k_ref, v_ref, c_ref, carry_ref):
    @pl.when(pl.program_id(0) % tiles_per_seq == 0)
    def _():
        carry_ref[...] = jnp.zeros_like(carry_ref)

    hn = _rms_normed(x_ref[...], g_ref[...]).astype(BF16)
    log_f = _log_sigmoid(_dot(hn, wf_ref[...]) + bf_ref[...])
    tm = log_f.shape[0]
    c = carry_ref[...] + _cumsum_rows(log_f)
    carry_ref[...] = c[tm - 1:tm, :]
    c_ref[...] = c
    k_ref[...] = _fox_blocks(_dot(hn, wk_ref[...]), _fox_bias_lanes(c, query_side=False)).astype(BF16)
    lane = lax.broadcasted_iota(jnp.int32, (tm, FOX_BLOCK), 1)
    row_sum_lane = jnp.where(lane == FOX_HEAD_DIM, 1.0, 0.0)
    v_ref[...] = _fox_blocks(_dot(hn, wv_ref[...]), [row_sum_lane] * FOX_HEADS).astype(BF16)


def _kv_proj(x, gain, wk, wv, wf, bf, seq, tm):
    t = x.shape[0]
    row = lambda n: pl.BlockSpec((tm, n), lambda i: (i, 0))
    return pl.pallas_call(
        functools.partial(_kv_kernel, seq // tm),
        grid=(t // tm,),
        in_specs=[row(D_MODEL), _full((1, D_MODEL)), _full(wk.shape), _full(wv.shape), _full(wf.shape),
                  _full(bf.shape)],
        out_specs=[row(FOX_WIDTH), row(FOX_WIDTH), row(LANES)],
        out_shape=[jax.ShapeDtypeStruct((t, FOX_WIDTH), BF16), jax.ShapeDtypeStruct((t, FOX_WIDTH), BF16),
                   jax.ShapeDtypeStruct((t, LANES), F32)],
        scratch_shapes=[pltpu.VMEM((1, LANES), F32)],
        compiler_params=_params(1),
        name="kv_proj_fox",
    )(x, gain, wk, wv, wf, bf)


def _in_b_kernel(x_ref, g_ref, c_ref, wq_ref, wmq_ref, q_ref, mq_ref):
    hn = _rms_normed(x_ref[...], g_ref[...]).astype(BF16)
    q = _dot(hn, wq_ref[...]) * (FOX_HEAD_DIM ** -0.5 * LOG2E)
    q_ref[...] = _fox_blocks(q, _fox_bias_lanes(c_ref[...], query_side=True)).astype(BF16)
    mq_ref[...] = _dot(hn, wmq_ref[...]).astype(BF16)


def _in_b(x, gain, c, wq, wmq, tm):
    t = x.shape[0]
    row = lambda n: pl.BlockSpec((tm, n), lambda i: (i, 0))
    return pl.pallas_call(
        _in_b_kernel,
        grid=(t // tm,),
        in_specs=[row(D_MODEL), _full((1, D_MODEL)), row(LANES), _full(wq.shape), _full(wmq.shape)],
        out_specs=[row(FOX_WIDTH), row(MEM_WIDTH)],
        out_shape=[jax.ShapeDtypeStruct((t, FOX_WIDTH), BF16), jax.ShapeDtypeStruct((t, MEM_WIDTH), BF16)],
        compiler_params=_params(1),
        name="in_proj_fox",
    )(x, gain, c, wq, wmq)


def _gla_kernel(chunks, q_ref, k_ref, v_ref, r_ref, la_ref, on_ref, o_ref, *st_refs):
    @pl.when(pl.program_id(1) == 0)
    def _():
        for st_ref in st_refs:
            st_ref[...] = jnp.zeros_like(st_ref)

    n_sub = GLA_CHUNK // GLA_SUB
    masks = []
    for i in range(n_sub):
        n_keys = GLA_SUB * (i + 1)
        rr = lax.broadcasted_iota(jnp.int32, (GLA_SUB, n_keys), 0) + GLA_SUB * i
        cc = lax.broadcasted_iota(jnp.int32, (GLA_SUB, n_keys), 1)
        masks.append(cc <= rr)

    def chunk_body(c, carry):
        r0 = pl.multiple_of(c * GLA_CHUNK, GLA_CHUNK)
        rows = pl.ds(r0, GLA_CHUNK)
        b_all = _cumsum_rows(la_ref[rows, :])
        vals, new_states, o_inter, scores = [], [], [], []
        for h in range(GLA_HEADS):
            ks = slice(h * GLA_DK_PAD, (h + 1) * GLA_DK_PAD)
            vs = slice(h * GLA_DV_PAD, (h + 1) * GLA_DV_PAD)
            q = q_ref[rows, ks].astype(F32)
            k = k_ref[rows, ks].astype(F32)
            v = v_ref[rows, vs]
            b = b_all[:, ks]
            b_last = b[GLA_CHUNK - 1:GLA_CHUNK, :]
            st = st_refs[h][...]
            o_inter.append(_dot_nt((q * jnp.exp(b)).astype(BF16), st.astype(BF16)))
            kdec = (k * jnp.exp(b_last - b)).astype(BF16)
            new_states.append(st * jnp.exp(b_last) + _dot_tn(v, kdec))
            head_scores = []
            for i in range(n_sub):
                lo, hi = GLA_SUB * i, GLA_SUB * (i + 1)
                b_i = b[lo:hi, :]
                qd = q[lo:hi, :] * jnp.exp(b_i - b[lo - 1:lo, :]) if i else q[lo:hi, :] * jnp.exp(b_i)
                kd = k[:hi, :] * jnp.exp(b[lo - 1:lo, :] - b[:hi, :]) if i else k[:hi, :] * jnp.exp(-b[:hi, :])
                head_scores.append(_dot_nt(qd.astype(BF16), kd.astype(BF16)))
            scores.append(head_scores)
            vals.append(v)
        outs = []
        for h in range(GLA_HEADS):
            vs = slice(h * GLA_DV_PAD, (h + 1) * GLA_DV_PAD)
            o_parts = [_dot(jnp.where(masks[i], scores[h][i], 0.0).astype(BF16), vals[h][:GLA_SUB * (i + 1), :])
                       for i in range(n_sub)]
            o = o_inter[h] + jnp.concatenate(o_parts, axis=0)
            ms = jnp.sum(o * o, axis=-1, keepdims=True) * (1.0 / GLA_DV)
            y = o * lax.rsqrt(ms + EPS) * on_ref[...] * _silu(r_ref[rows, vs].astype(F32))
            outs.append(y.astype(BF16))
        for h in range(GLA_HEADS):
            st_refs[h][...] = new_states[h]
            o_ref[rows, h * GLA_DV_PAD:(h + 1) * GLA_DV_PAD] = outs[h]
        return carry

    lax.fori_loop(0, chunks, chunk_body, 0, unroll=2)


def _gla(q, k, v, r, la, onorm, bsz, seq, tm):
    t = q.shape[0]
    tiles_per_seq = seq // tm
    kw, vw = GLA_HEADS * GLA_DK_PAD, GLA_HEADS * GLA_DV_PAD
    row = lambda n: pl.BlockSpec((tm, n), lambda b, i: (b * tiles_per_seq + i, 0))
    return pl.pallas_call(
        functools.partial(_gla_kernel, tm // GLA_CHUNK),
        grid=(bsz, tiles_per_seq),
        in_specs=[row(kw), row(kw), row(vw), row(vw), row(kw), pl.BlockSpec((1, GLA_DV_PAD), lambda b, i: (0, 0))],
        out_specs=row(vw),
        out_shape=jax.ShapeDtypeStruct((t, vw), BF16),
        scratch_shapes=[pltpu.VMEM((GLA_DV_PAD, GLA_DK_PAD), F32)] * GLA_HEADS,
        compiler_params=_params(2),
        name="gla_scan",
    )(q, k, v, r, la, onorm)


def _fox_kernel(tile, q_ref, k_ref, v_ref, o_ref):
    i = pl.program_id(2)
    q = q_ref[...]
    qs = (q[:, :FOX_BLOCK], q[:, FOX_BLOCK:])
    blks = (slice(0, FOX_BLOCK), slice(FOX_BLOCK, 2 * FOX_BLOCK))

    def absorb(m, acc, s, v):
        m_new = jnp.maximum(m, jnp.max(s, axis=-1, keepdims=True))
        p = jnp.exp2(s - m_new).astype(BF16)
        return m_new, jnp.exp2(m - m_new) * acc + _dot(p, v)

    def step(j, state):
        keys = pl.ds(pl.multiple_of(j * tile, tile), tile)
        return tuple(absorb(*state[hh], _dot_nt(qs[hh], k_ref[keys, blks[hh]]), v_ref[keys, blks[hh]])
                     for hh in range(2))

    init = (jnp.full((tile, 1), -jnp.inf, F32), jnp.zeros((tile, FOX_BLOCK), F32))
    state = lax.fori_loop(0, i, step, (init, init))

    band = tile // FOX_DIAG_BANDS
    start = pl.multiple_of(i * tile, tile)
    bands = [(hh, r) for hh in range(2) for r in range(FOX_DIAG_BANDS)]
    band_rows = lambda r: slice(r * band, (r + 1) * band)
    band_keys = lambda r: pl.ds(start, (r + 1) * band)
    scores = {(hh, r): _dot_nt(qs[hh][band_rows(r)], k_ref[band_keys(r), blks[hh]]) for hh, r in bands}
    accs = {}
    for hh, r in bands:
        m, acc = state[hh]
        s = scores[hh, r]
        row = lax.broadcasted_iota(jnp.int32, s.shape, 0) + r * band
        col = lax.broadcasted_iota(jnp.int32, s.shape, 1)
        s = jnp.where(col <= row, s, -jnp.inf)
        _, accs[hh, r] = absorb(m[band_rows(r)], acc[band_rows(r)], s, v_ref[band_keys(r), blks[hh]])
    acc0, acc1 = (jnp.concatenate([accs[hh, r] for r in range(FOX_DIAG_BANDS)], axis=0) for hh in range(2))
    lane = lax.broadcasted_iota(jnp.int32, (tile, FOX_BLOCK), 1)
    row_sum = lambda acc: jnp.sum(jnp.where(lane == FOX_HEAD_DIM, acc, 0.0), axis=-1, keepdims=True)
    o0, o1 = acc0 / row_sum(acc0), acc1 / row_sum(acc1)
    o_ref[...] = jnp.where(lane < FOX_HEAD_DIM, o0, pltpu.roll(o1, FOX_HEAD_DIM, axis=1)).astype(BF16)


def _fox(q, k, v, bsz, seq, tile):
    t = q.shape[0]
    nq = seq // tile
    pairs = FOX_HEADS // 2
    return pl.pallas_call(
        functools.partial(_fox_kernel, tile),
        grid=(bsz, pairs, nq),
        in_specs=[pl.BlockSpec((tile, 2 * FOX_BLOCK), lambda b, p, i: (b * nq + i, p)),
                  pl.BlockSpec((seq, 2 * FOX_BLOCK), lambda b, p, i: (b, p)),
                  pl.BlockSpec((seq, 2 * FOX_BLOCK), lambda b, p, i: (b, p))],
        out_specs=pl.BlockSpec((tile, LANES), lambda b, p, i: (b * nq + i, p)),
        out_shape=jax.ShapeDtypeStruct((t, MAIN_WIDTH), BF16),
        compiler_params=_params(3),
        name="fox_attention",
    )(q, k, v)


def _out_kernel(x_ref, main_ref, mq_ref, mk_ref, mv_ref, wo_main_ref, wo_mem_ref, o_ref):
    mq, mk, mv = mq_ref[...], mk_ref[...], mv_ref[...]
    head = lambda a, h: a[:, h * MEM_HEAD_DIM:(h + 1) * MEM_HEAD_DIM]
    scores = [_dot_nt(head(mq, h), head(mk, h)) * (MEM_HEAD_DIM ** -0.5) for h in range(MEM_HEADS)]
    base = x_ref[...] + _dot(main_ref[...], wo_main_ref[...])
    heads = []
    for h in range(MEM_HEADS):
        s = scores[h]
        p = jnp.exp(s - jnp.max(s, axis=-1, keepdims=True))
        l = jnp.sum(p, axis=-1, keepdims=True)
        heads.append(_dot(p.astype(BF16), head(mv, h)) / l)
    mem_o = jnp.concatenate(heads, axis=-1).astype(BF16)
    o_ref[...] = base + _dot(mem_o, wo_mem_ref[...])


def _out_proj(x, main, mq, mem_k, mem_v, wo_main, wo_mem, seq, tm):
    t = x.shape[0]
    tiles_per_seq = seq // tm
    n_mem = mem_k.shape[1]
    row = lambda n: pl.BlockSpec((tm, n), lambda i: (i, 0))
    mem = pl.BlockSpec((None, n_mem, MEM_WIDTH), lambda i: (i // tiles_per_seq, 0, 0))
    return pl.pallas_call(
        _out_kernel,
        grid=(t // tm,),
        in_specs=[row(D_MODEL), row(main.shape[1]), row(MEM_WIDTH), mem, mem, _full(wo_main.shape),
                  _full(wo_mem.shape)],
        out_specs=row(D_MODEL),
        out_shape=jax.ShapeDtypeStruct((t, D_MODEL), F32),
        compiler_params=_params(1),
        name="out_proj_mem_attn",
    )(x, main, mq, mem_k, mem_v, wo_main, wo_mem)


def _swiglu_acc(hn, wg_ref, wu_ref, wd_ref, acc_ref, row_scale=None):
    for j in range(D_FF // FF_CHUNK):
        cs = slice(j * FF_CHUNK, (j + 1) * FF_CHUNK)
        a = _silu(_dot(hn, wg_ref[:, cs])) * _dot(hn, wu_ref[:, cs])
        if row_scale is not None:
            a = a * row_scale
        acc_ref[...] += _dot(a.astype(BF16), wd_ref[cs, :])


def _ffn_kernel(x_ref, g_ref, wg_ref, wu_ref, wd_ref, o_ref):
    x = x_ref[...]
    o_ref[...] = x
    _swiglu_acc(_rms_normed(x, g_ref[...]).astype(BF16), wg_ref, wu_ref, wd_ref, o_ref)


def _ffn(x, gain, wg, wu, wd, tm):
    t = x.shape[0]
    row = pl.BlockSpec((tm, D_MODEL), lambda i: (i, 0))
    return pl.pallas_call(
        _ffn_kernel,
        grid=(t // tm,),
        in_specs=[row, _full((1, D_MODEL)), _full(wg.shape), _full(wu.shape), _full(wd.shape)],
        out_specs=row,
        out_shape=jax.ShapeDtypeStruct((t, D_MODEL), F32),
        compiler_params=_params(1),
        name="dense_swiglu",
    )(x, gain, wg, wu, wd)


ROUTE_E1, ROUTE_E2, ROUTE_W1, ROUTE_W2, ROUTE_RANK1, ROUTE_RANK2 = range(6)


def _router_kernel(x_ref, g_ref, wr_ref, hn_ref, route_ref, counts_ref, carry_ref):
    @pl.when(pl.program_id(0) == 0)
    def _():
        carry_ref[...] = jnp.zeros_like(carry_ref)

    hn = _rms_normed(x_ref[...], g_ref[...])
    hn_ref[...] = hn
    logits = _dot_3pass(hn, wr_ref[...])
    tm = logits.shape[0]
    lane = lax.broadcasted_iota(jnp.int32, logits.shape, 1)
    logits = jnp.where(lane < N_EXPERTS, logits, -jnp.inf)
    m1 = jnp.max(logits, axis=-1, keepdims=True)
    i1 = jnp.min(jnp.where(logits == m1, lane, LANES), axis=-1, keepdims=True)
    rest = jnp.where(lane == i1, -jnp.inf, logits)
    m2 = jnp.max(rest, axis=-1, keepdims=True)
    i2 = jnp.min(jnp.where(rest == m2, lane, LANES), axis=-1, keepdims=True)
    e2 = jnp.exp(m2 - m1)
    w1 = 1.0 / (1.0 + e2)
    chosen = jnp.where((lane == i1) | (lane == i2), 1.0, 0.0)
    rr = lax.broadcasted_iota(jnp.int32, (tm, tm), 0)
    cc = lax.broadcasted_iota(jnp.int32, (tm, tm), 1)
    before = jnp.where(cc < rr, 1.0, 0.0).astype(BF16)
    base = carry_ref[...] + _dot(before, chosen.astype(BF16))
    rank1 = jnp.sum(jnp.where(lane == i1, base, 0.0), axis=-1, keepdims=True)
    rank2 = jnp.sum(jnp.where(lane == i2, base, 0.0), axis=-1, keepdims=True)
    carry_ref[...] += jnp.sum(chosen, axis=0, keepdims=True)
    counts_ref[...] = carry_ref[...]
    rec = jnp.zeros_like(logits)
    for slot, val in ((ROUTE_E1, i1.astype(F32)), (ROUTE_E2, i2.astype(F32)), (ROUTE_W1, w1), (ROUTE_W2, e2 * w1),
                      (ROUTE_RANK1, rank1), (ROUTE_RANK2, rank2)):
        rec = jnp.where(lane == slot, val, rec)
    route_ref[...] = rec


def _router(x, gain, wr, tm):
    t = x.shape[0]
    row = lambda n: pl.BlockSpec((tm, n), lambda i: (i, 0))
    return pl.pallas_call(
        _router_kernel,
        grid=(t // tm,),
        in_specs=[row(D_MODEL), _full((1, D_MODEL)), _full(wr.shape)],
        out_specs=[row(D_MODEL), row(LANES), _full((1, LANES))],
        out_shape=[jax.ShapeDtypeStruct((t, D_MODEL), F32), jax.ShapeDtypeStruct((t, LANES), F32),
                   jax.ShapeDtypeStruct((1, LANES), F32)],
        scratch_shapes=[pltpu.VMEM((1, LANES), F32)],
        compiler_params=_params(1),
        name="moe_router",
    )(x, gain, wr)


def _row_copy(src, src_row, dst, dst_row, sem):
    return pltpu.make_async_copy(src.at[pl.ds(src_row, 1)], dst.at[pl.ds(dst_row, 1)], sem)


def _dispatch_kernel(tm, cnt_ref, pad_ref, off_ref, pos1_ref, pos2_ref, hn_ref, xs_ref, zero_ref, sem, pad_sem):
    @pl.when(pl.program_id(0) == 0)
    def _():
        zero_ref[...] = jnp.zeros_like(zero_ref)
        for e in range(N_EXPERTS + 1):
            lo, hi = off_ref[e] + cnt_ref[e], off_ref[e] + pad_ref[e]

            def fill(r, c):
                _row_copy(zero_ref, 0, xs_ref, r, pad_sem).start()
                return c

            def drain(r, c):
                _row_copy(zero_ref, 0, xs_ref, r, pad_sem).wait()
                return c

            lax.fori_loop(lo, hi, fill, 0)
            lax.fori_loop(lo, hi, drain, 0)

    def issue(r, c):
        _row_copy(hn_ref, r, xs_ref, pos1_ref[0, r], sem).start(priority=0)
        _row_copy(hn_ref, r, xs_ref, pos2_ref[0, r], sem).start(priority=1)
        return c

    lax.fori_loop(0, tm, issue, 0, unroll=8)
    for _ in range(2):
        pltpu.make_async_copy(hn_ref, xs_ref.at[pl.ds(0, tm)], sem).wait()


def _dispatch(hn, pos1, pos2, counts, padded, offsets, n_rows, tm):
    t = hn.shape[0]
    nt = t // tm
    smem_row = pl.BlockSpec((None, 1, tm), lambda i, *_: (i, 0, 0), memory_space=pltpu.SMEM)
    return pl.pallas_call(
        functools.partial(_dispatch_kernel, tm),
        grid_spec=pltpu.PrefetchScalarGridSpec(
            num_scalar_prefetch=3,
            grid=(nt,),
            in_specs=[smem_row, smem_row, pl.BlockSpec((tm, D_MODEL), lambda i, *_: (i, 0))],
            out_specs=pl.BlockSpec(memory_space=pl.ANY),
            scratch_shapes=[pltpu.VMEM((8, D_MODEL), F32), pltpu.SemaphoreType.DMA, pltpu.SemaphoreType.DMA],
        ),
        out_shape=jax.ShapeDtypeStruct((n_rows, D_MODEL), F32),
        compiler_params=_params(1),
        name="moe_dispatch",
    )(counts, padded, offsets, pos1.reshape(nt, 1, tm), pos2.reshape(nt, 1, tm), hn)


def _experts_kernel(tile_expert_ref, n_used_ref, xs_ref, wg_ref, wu_ref, wd_ref, o_ref):
    o_ref[...] = jnp.zeros_like(o_ref)

    @pl.when(pl.program_id(0) < n_used_ref[0])
    def _():
        _swiglu_acc(xs_ref[...].astype(BF16), wg_ref, wu_ref, wd_ref, o_ref)


def _experts(xs, tile_expert, n_used, wg, wu, wd, tm):
    n_rows = xs.shape[0]
    wspec = lambda shape: pl.BlockSpec(
        (None,) + shape, lambda i, te, nu: (te[jnp.minimum(i, nu[0] - 1)], 0, 0))
    return pl.pallas_call(
        _experts_kernel,
        grid_spec=pltpu.PrefetchScalarGridSpec(
            num_scalar_prefetch=2,
            grid=(n_rows // tm,),
            in_specs=[pl.BlockSpec((tm, D_MODEL), lambda i, te, nu: (i, 0)),
                      wspec((D_MODEL, D_FF)), wspec((D_MODEL, D_FF)), wspec((D_FF, D_MODEL))],
            out_specs=pl.BlockSpec((tm, D_MODEL), lambda i, te, nu: (i, 0)),
        ),
        out_shape=jax.ShapeDtypeStruct((n_rows, D_MODEL), F32),
        compiler_params=_params(1),
        name="moe_experts",
    )(tile_expert, n_used, xs, wg, wu, wd)


def _combine_kernel(tm, n_tiles, pos1_ref, pos2_ref, next1_ref, next2_ref, x_ref, route_ref, gf_ref, y_ref,
                    o_ref, buf_ref, sems):
    i = pl.program_id(0)
    slot = i % 2

    def gather(p1_ref, p2_ref, slot):
        def issue(r, c):
            _row_copy(y_ref, p1_ref[0, r], buf_ref.at[slot, 0], r, sems.at[slot]).start(priority=0)
            _row_copy(y_ref, p2_ref[0, r], buf_ref.at[slot, 1], r, sems.at[slot]).start(priority=1)
            return c

        lax.fori_loop(0, tm, issue, 0, unroll=8)

    @pl.when(i == 0)
    def _():
        gather(pos1_ref, pos2_ref, 0)

    @pl.when(i + 1 < n_tiles)
    def _():
        gather(next1_ref, next2_ref, 1 - slot)

    route = route_ref[...]
    lane = lax.broadcasted_iota(jnp.int32, route.shape, 1)
    w1 = jnp.sum(jnp.where(lane == ROUTE_W1, route, 0.0), axis=-1, keepdims=True)
    w2 = jnp.sum(jnp.where(lane == ROUTE_W2, route, 0.0), axis=-1, keepdims=True)
    for k in range(2):
        pltpu.make_async_copy(y_ref.at[pl.ds(0, tm)], buf_ref.at[slot, k], sems.at[slot]).wait()
    o_ref[...] = _rms_normed(x_ref[...] + w1 * buf_ref[slot, 0] + w2 * buf_ref[slot, 1], gf_ref[...])


def _combine(x, route, pos1, pos2, y, gain_final, tm):
    t = x.shape[0]
    nt = t // tm
    smem_row = lambda ahead: pl.BlockSpec((None, 1, tm), lambda i: (jnp.minimum(i + ahead, nt - 1), 0, 0),
                                          memory_space=pltpu.SMEM)
    row = lambda n: pl.BlockSpec((tm, n), lambda i: (i, 0))
    pos1, pos2 = pos1.reshape(nt, 1, tm), pos2.reshape(nt, 1, tm)
    return pl.pallas_call(
        functools.partial(_combine_kernel, tm, nt),
        grid=(nt,),
        in_specs=[smem_row(0), smem_row(0), smem_row(1), smem_row(1), row(D_MODEL), row(LANES),
                  _full((1, D_MODEL)), pl.BlockSpec(memory_space=pl.ANY)],
        out_specs=row(D_MODEL),
        out_shape=jax.ShapeDtypeStruct((t, D_MODEL), F32),
        scratch_shapes=[pltpu.VMEM((2, 2, tm, D_MODEL), F32), pltpu.SemaphoreType.DMA((2,))],
        compiler_params=_params(1),
        name="moe_combine",
    )(pos1, pos2, pos1, pos2, x, route, gain_final, y)


def _moe(x, gain, gain_final, wr, wg, wu, wd, tm):
    t = x.shape[0]
    hn, route, counts = _router(x, gain, wr, tm)
    col = lambda c: route[:, c].astype(jnp.int32)
    counts = counts[0, :N_EXPERTS].astype(jnp.int32)
    padded = (counts + tm - 1) // tm * tm
    ends = jnp.cumsum(padded)
    offsets = ends - padded
    pos1 = offsets[col(ROUTE_E1)] + col(ROUTE_RANK1)
    pos2 = offsets[col(ROUTE_E2)] + col(ROUTE_RANK2)
    n_rows = 2 * t + N_EXPERTS * tm
    tile_start = jnp.arange(n_rows // tm, dtype=jnp.int32) * tm
    tile_expert = jnp.minimum(jnp.sum(tile_start[:, None] >= ends[None, :], axis=1), N_EXPERTS - 1).astype(jnp.int32)
    n_used = (ends[-1:] // tm).astype(jnp.int32)
    tail = lambda a, v: jnp.concatenate([a, v.astype(jnp.int32)])
    xs = _dispatch(hn, pos1, pos2, tail(counts, jnp.zeros((1,))), tail(padded, n_rows - ends[-1:]),
                   tail(offsets, ends[-1:]), n_rows, tm)
    y = _experts(xs, tile_expert, n_used, wg, wu, wd, tm)
    return _combine(x, route, pos1, pos2, y, gain_final, tm // 2)


def _pad_heads(w, heads, width, padded):
    lead = w.shape[:-1]
    w = w.reshape(lead + (heads, width))
    w = jnp.pad(w, [(0, 0)] * len(lead) + [(0, 0), (0, padded - width)])
    return w.reshape(lead + (heads * padded,))


def _pad_cols(w, n):
    return jnp.pad(w, [(0, 0)] * (w.ndim - 1) + [(0, n - w.shape[-1])])


def kernel(x, mem, norm_mix, norm_mem, norm_ffn, norm_kv, norm_final, w_in_a, w_gla_gate2, b_gla_gate2,
           gla_onorm, w_in_b, w_kv, b_forget, w_mem_kv, w_out, w_ff_gate, w_ff_up, w_ff_down, w_router,
           w_moe_gate, w_moe_up, w_moe_down):
    bsz, seq, _ = x.shape
    n_mem = mem.shape[1]
    t = bsz * seq
    tm = 512
    xt = x.reshape(t, D_MODEL)
    memt = mem.reshape(bsz * n_mem, D_MODEL)
    gain = lambda g: g.reshape(1, D_MODEL)
    kw = GLA_HEADS * GLA_DK

    def mem_kv(layer):
        wk, wv = w_mem_kv[layer, :, :MEM_WIDTH], w_mem_kv[layer, :, MEM_WIDTH:]
        mk, mv = _norm_proj(memt, gain(norm_mem[layer]), [wk.astype(BF16), wv.astype(BF16)], (1.0, 1.0),
                            n_mem, "mem_kv_proj")
        return mk.reshape(bsz, n_mem, MEM_WIDTH), mv.reshape(bsz, n_mem, MEM_WIDTH)

    wa = w_in_a[0]
    wq = _pad_heads(wa[:, :kw], GLA_HEADS, GLA_DK, GLA_DK_PAD).astype(BF16)
    wk = _pad_heads(wa[:, kw:2 * kw], GLA_HEADS, GLA_DK, GLA_DK_PAD).astype(BF16)
    off = 2 * kw
    wv = _pad_heads(wa[:, off:off + MAIN_WIDTH], GLA_HEADS, GLA_DV, GLA_DV_PAD).astype(BF16)
    off += MAIN_WIDTH
    wr = _pad_heads(wa[:, off:off + MAIN_WIDTH], GLA_HEADS, GLA_DV, GLA_DV_PAD).astype(BF16)
    off += MAIN_WIDTH
    wg1 = _pad_cols(wa[:, off:off + GLA_GATE_RANK], LANES).astype(BF16)
    off += GLA_GATE_RANK
    wmq = wa[:, off:off + MEM_WIDTH].astype(BF16)
    wg2 = jnp.pad(_pad_heads(w_gla_gate2[0], GLA_HEADS, GLA_DK, GLA_DK_PAD), ((0, LANES - GLA_GATE_RANK), (0, 0)))
    bg2 = _pad_heads(b_gla_gate2[0], GLA_HEADS, GLA_DK, GLA_DK_PAD).reshape(1, -1)
    onorm = _pad_cols(gla_onorm[0], GLA_DV_PAD).reshape(1, GLA_DV_PAD)

    q, k, v, r, mq, la = _in_a(xt, gain(norm_mix[0]), wq, wk, wv, wr, wmq, wg1, wg2, bg2, tm)
    main = _gla(q, k, v, r, la, onorm, bsz, seq, tm)
    mk, mv = mem_kv(0)
    wo = w_out[0]
    wo_main = jnp.pad(wo[:MAIN_WIDTH].reshape(GLA_HEADS, GLA_DV, D_MODEL),
                      ((0, 0), (0, GLA_DV_PAD - GLA_DV), (0, 0))).reshape(GLA_HEADS * GLA_DV_PAD, D_MODEL)
    xt = _out_proj(xt, main, mq, mk, mv, wo_main.astype(BF16), wo[MAIN_WIDTH:].astype(BF16), seq, tm)
    xt = _ffn(xt, gain(norm_ffn[0]), w_ff_gate[0].astype(BF16), w_ff_up[0].astype(BF16),
              w_ff_down[0].astype(BF16), tm)

    wf = _pad_cols(w_kv[:, 2 * MAIN_WIDTH:], LANES).astype(BF16)
    bf = _pad_cols(b_forget, LANES).reshape(1, LANES)
    k_sh, v_sh, c = _kv_proj(xt, gain(norm_kv), w_kv[:, :MAIN_WIDTH].astype(BF16),
                             w_kv[:, MAIN_WIDTH:2 * MAIN_WIDTH].astype(BF16), wf, bf, seq, tm)

    wb = w_in_b[0]
    qf, mq = _in_b(xt, gain(norm_mix[1]), c, wb[:, :MAIN_WIDTH].astype(BF16), wb[:, MAIN_WIDTH:].astype(BF16), tm)
    main = _fox(qf, k_sh, v_sh, bsz, seq, 1024)
    mk, mv = mem_kv(1)
    wo = w_out[1]
    xt = _out_proj(xt, main, mq, mk, mv, wo[:MAIN_WIDTH].astype(BF16), wo[MAIN_WIDTH:].astype(BF16), seq, tm)
    out = _moe(xt, gain(norm_ffn[1]), gain(norm_final), _pad_cols(w_router[0], LANES), w_moe_gate[0].astype(BF16),
               w_moe_up[0].astype(BF16), w_moe_down[0].astype(BF16), tm)
    return out.reshape(bsz, seq, D_MODEL)
```

```python
import functools

import jax
import jax.numpy as jnp
import numpy as np
from jax import lax
from jax.experimental import pallas as pl
from jax.experimental.pallas import tpu as pltpu

D_MODEL = 1024
EPS = 1e-6
GLA_CHUNK = 64
GLA_SUB = 16
MEM_HEADS = 4
MEM_HEAD_DIM = 64
MEM_WIDTH = MEM_HEADS * MEM_HEAD_DIM
MAIN_WIDTH = D_MODEL - MEM_WIDTH
GLA_HEADS = 4
GLA_DK = MAIN_WIDTH // 2 // GLA_HEADS
GLA_DV = MAIN_WIDTH // GLA_HEADS
GLA_DK_PAD = 128
GLA_DV_PAD = 256
GLA_GATE_RANK = 16
GLA_GATE_TAU = 16.0
FOX_HEADS = 12
FOX_HEAD_DIM = 64
D_FF = 2816
N_EXPERTS = 8
LANES = 128
FF_CHUNK = 256
VMEM_LIMIT = 56 * 1024 * 1024

BF16 = jnp.bfloat16
F32 = jnp.float32


def _params(n_axes, vmem=VMEM_LIMIT):
    return pltpu.CompilerParams(dimension_semantics=("arbitrary",) * n_axes, vmem_limit_bytes=vmem)


def _rms_normed(x, gain):
    ms = jnp.mean(x * x, axis=-1, keepdims=True)
    return x * lax.rsqrt(ms + EPS) * gain


def _log_sigmoid(z):
    return jnp.minimum(z, 0.0) - jnp.log(1.0 + jnp.exp(-jnp.abs(z)))


def _silu(z):
    return z / (1.0 + jnp.exp(-z))


def _dot(a, b):
    return jnp.dot(a, b, preferred_element_type=F32)


def _dot_nt(a, b):
    return lax.dot_general(a, b, (((1,), (1,)), ((), ())), preferred_element_type=F32)


def _dot_tn(a, b):
    return lax.dot_general(a, b, (((0,), (0,)), ((), ())), preferred_element_type=F32)


def _bf16_terms(v):
    hi = v.astype(BF16).astype(F32)
    mid = (v - hi).astype(BF16).astype(F32)
    return hi, mid, v - hi - mid


def _dot_3pass(a, b):
    a_hi, a_lo, _ = _bf16_terms(a)
    b_hi, b_lo, _ = _bf16_terms(b)
    a_hi, a_lo, b_hi, b_lo = (t.astype(BF16) for t in (a_hi, a_lo, b_hi, b_lo))
    return _dot(a_hi, b_hi) + _dot(a_hi, b_lo) + _dot(a_lo, b_hi)


def _cumsum_rows(x):
    n = x.shape[0]
    r = lax.broadcasted_iota(jnp.int32, (n, n), 0)
    c = lax.broadcasted_iota(jnp.int32, (n, n), 1)
    tril = jnp.where(c <= r, 1.0, 0.0).astype(BF16)
    hi, mid, lo = _bf16_terms(x)
    return _dot(tril, hi.astype(BF16)) + _dot(tril, mid.astype(BF16)) + _dot(tril, lo.astype(BF16))


def _full(shape):
    return pl.BlockSpec(shape, lambda *_: (0,) * len(shape))


def _in_a_kernel(x_ref, g_ref, wq_ref, wk_ref, wv_ref, wr_ref, wmq_ref, wg1_ref, wg2_ref, bg2_ref,
                 q_ref, k_ref, v_ref, r_ref, mq_ref, la_ref):
    hn = _rms_normed(x_ref[...], g_ref[...]).astype(BF16)
    q_ref[...] = (_dot(hn, wq_ref[...]) * (GLA_DK ** -0.5)).astype(BF16)
    k_ref[...] = _dot(hn, wk_ref[...]).astype(BF16)
    v_ref[...] = _dot(hn, wv_ref[...]).astype(BF16)
    r_ref[...] = _dot(hn, wr_ref[...]).astype(BF16)
    mq_ref[...] = _dot(hn, wmq_ref[...]).astype(BF16)
    glr = _dot(hn, wg1_ref[...])
    z = _dot_3pass(glr, wg2_ref[...])
    la_ref[...] = _log_sigmoid(z + bg2_ref[...]) * (1.0 / GLA_GATE_TAU)


def _in_a(x, gain, wq, wk, wv, wr, wmq, wg1, wg2, bg2, tm):
    t = x.shape[0]
    row = lambda n: pl.BlockSpec((tm, n), lambda i: (i, 0))
    kw, vw = GLA_HEADS * GLA_DK_PAD, GLA_HEADS * GLA_DV_PAD
    return pl.pallas_call(
        _in_a_kernel,
        grid=(t // tm,),
        in_specs=[row(D_MODEL), _full((1, D_MODEL)), _full(wq.shape), _full(wk.shape), _full(wv.shape),
                  _full(wr.shape), _full(wmq.shape), _full(wg1.shape), _full(wg2.shape), _full(bg2.shape)],
        out_specs=[row(kw), row(kw), row(vw), row(vw), row(MEM_WIDTH), row(kw)],
        out_shape=[jax.ShapeDtypeStruct((t, kw), BF16), jax.ShapeDtypeStruct((t, kw), BF16),
                   jax.ShapeDtypeStruct((t, vw), BF16), jax.ShapeDtypeStruct((t, vw), BF16),
                   jax.ShapeDtypeStruct((t, MEM_WIDTH), BF16), jax.ShapeDtypeStruct((t, kw), F32)],
        compiler_params=_params(1),
        name="in_proj_gla",
    )(x, gain, wq, wk, wv, wr, wmq, wg1, wg2, bg2)


def _norm_proj_kernel(n_out, scales, x_ref, g_ref, *refs):
    w_refs, o_refs = refs[:n_out], refs[n_out:]
    hn = _rms_normed(x_ref[...], g_ref[...]).astype(BF16)
    for w_ref, o_ref, s in zip(w_refs, o_refs, scales):
        o = _dot(hn, w_ref[...])
        if s != 1.0:
            o = o * s
        o_ref[...] = o.astype(o_ref.dtype)


def _norm_proj(x, gain, weights, scales, tm, name):
    t = x.shape[0]
    row = lambda n: pl.BlockSpec((tm, n), lambda i: (i, 0))
    return pl.pallas_call(
        functools.partial(_norm_proj_kernel, len(weights), scales),
        grid=(t // tm,),
        in_specs=[row(D_MODEL), _full((1, D_MODEL))] + [_full(w.shape) for w in weights],
        out_specs=[row(w.shape[1]) for w in weights],
        out_shape=[jax.ShapeDtypeStruct((t, w.shape[1]), BF16) for w in weights],
        compiler_params=_params(1),
        name=name,
    )(x, gain, *weights)


FOX_BLOCK = LANES
FOX_WIDTH = FOX_HEADS * FOX_BLOCK
LOG2E = 1.4426950408889634
FOX_DIAG_BANDS = 4


FOX_ONE_LANE = 3 * FOX_HEADS


def _fox_bias_matrix(query_side):
    e = np.zeros((LANES, FOX_WIDTH), np.float32)
    for h in range(FOX_HEADS):
        base = h * FOX_BLOCK + FOX_HEAD_DIM
        for n in range(3):
            if query_side:
                e[n * FOX_HEADS + h, base + n] = 1.0
                e[FOX_ONE_LANE, base + 3 + n] = 1.0
            else:
                e[FOX_ONE_LANE, base + n] = 1.0
                e[n * FOX_HEADS + h, base + 3 + n] = -1.0
    return jnp.asarray(e, BF16)


def _fox_bias_lanes(c, place_ref):
    hi, mid, lo = _bf16_terms(c * LOG2E)
    lane = lax.broadcasted_iota(jnp.int32, c.shape, 1)
    packed = jnp.where(lane == FOX_ONE_LANE, 1.0, 0.0)
    for n, term in ((2, lo), (1, mid), (0, hi)):
        shifted = pltpu.roll(term, n * FOX_HEADS, axis=1) if n else term
        packed = jnp.where((lane >= n * FOX_HEADS) & (lane < (n + 1) * FOX_HEADS), shifted, packed)
    placed = _dot(packed.astype(BF16), place_ref[...])
    return [placed[:, h * FOX_BLOCK:(h + 1) * FOX_BLOCK] for h in range(FOX_HEADS)]


def _fox_blocks(dense, spare):
    lane = lax.broadcasted_iota(jnp.int32, (dense.shape[0], FOX_BLOCK), 1)
    blocks = []
    for h in range(FOX_HEADS):
        pair = dense[:, (h // 2) * FOX_BLOCK:(h // 2 + 1) * FOX_BLOCK]
        own = pair if h % 2 == 0 else pltpu.roll(pair, FOX_HEAD_DIM, axis=1)
        blocks.append(jnp.where(lane < FOX_HEAD_DIM, own, spare[h]))
    return jnp.concatenate(blocks, axis=-1)


def _kv_kernel(tiles_per_seq, x_ref, g_ref, wk_ref, wv_ref, wf_ref, bf_ref, place_ref, k_ref, v_ref, c_ref,
               carry_ref):
    @pl.when(pl.program_id(0) % tiles_per_seq == 0)
    def _():
        carry_ref[...] = jnp.zeros_like(carry_ref)

    hn = _rms_normed(x_ref[...], g_ref[...]).astype(BF16)
    log_f = _log_sigmoid(_dot(hn, wf_ref[...]) + bf_ref[...])
    tm = log_f.shape[0]
    c = carry_ref[...] + _cumsum_rows(log_f)
    carry_ref[...] = c[tm - 1:tm, :]
    c_ref[...] = c
    k_ref[...] = _fox_blocks(_dot(hn, wk_ref[...]), _fox_bias_lanes(c, place_ref)).astype(BF16)
    lane = lax.broadcasted_iota(jnp.int32, (tm, FOX_BLOCK), 1)
    row_sum_lane = jnp.where(lane == FOX_HEAD_DIM, 1.0, 0.0)
    v_ref[...] = _fox_blocks(_dot(hn, wv_ref[...]), [row_sum_lane] * FOX_HEADS).astype(BF16)


def _kv_proj(x, gain, wk, wv, wf, bf, seq, tm):
    t = x.shape[0]
    place = _fox_bias_matrix(query_side=False)
    row = lambda n: pl.BlockSpec((tm, n), lambda i: (i, 0))
    return pl.pallas_call(
        functools.partial(_kv_kernel, seq // tm),
        grid=(t // tm,),
        in_specs=[row(D_MODEL), _full((1, D_MODEL)), _full(wk.shape), _full(wv.shape), _full(wf.shape),
                  _full(bf.shape), _full(place.shape)],
        out_specs=[row(FOX_WIDTH), row(FOX_WIDTH), row(LANES)],
        out_shape=[jax.ShapeDtypeStruct((t, FOX_WIDTH), BF16), jax.ShapeDtypeStruct((t, FOX_WIDTH), BF16),
                   jax.ShapeDtypeStruct((t, LANES), F32)],
        scratch_shapes=[pltpu.VMEM((1, LANES), F32)],
        compiler_params=_params(1),
        name="kv_proj_fox",
    )(x, gain, wk, wv, wf, bf, place)


def _in_b_kernel(x_ref, g_ref, c_ref, wq_ref, wmq_ref, place_ref, q_ref, mq_ref):
    hn = _rms_normed(x_ref[...], g_ref[...]).astype(BF16)
    q = _dot(hn, wq_ref[...]) * (FOX_HEAD_DIM ** -0.5 * LOG2E)
    q_ref[...] = _fox_blocks(q, _fox_bias_lanes(c_ref[...], place_ref)).astype(BF16)
    mq_ref[...] = _dot(hn, wmq_ref[...]).astype(BF16)


def _in_b(x, gain, c, wq, wmq, tm):
    t = x.shape[0]
    place = _fox_bias_matrix(query_side=True)
    row = lambda n: pl.BlockSpec((tm, n), lambda i: (i, 0))
    return pl.pallas_call(
        _in_b_kernel,
        grid=(t // tm,),
        in_specs=[row(D_MODEL), _full((1, D_MODEL)), row(LANES), _full(wq.shape), _full(wmq.shape),
                  _full(place.shape)],
        out_specs=[row(FOX_WIDTH), row(MEM_WIDTH)],
        out_shape=[jax.ShapeDtypeStruct((t, FOX_WIDTH), BF16), jax.ShapeDtypeStruct((t, MEM_WIDTH), BF16)],
        compiler_params=_params(1),
        name="in_proj_fox",
    )(x, gain, c, wq, wmq, place)


def _gla_kernel(chunks, q_ref, k_ref, v_ref, r_ref, la_ref, on_ref, o_ref, *st_refs):
    @pl.when(pl.program_id(1) == 0)
    def _():
        for st_ref in st_refs:
            st_ref[...] = jnp.zeros_like(st_ref)

    n_sub = GLA_CHUNK // GLA_SUB
    masks = []
    for i in range(n_sub):
        n_keys = GLA_SUB * (i + 1)
        rr = lax.broadcasted_iota(jnp.int32, (GLA_SUB, n_keys), 0) + GLA_SUB * i
        cc = lax.broadcasted_iota(jnp.int32, (GLA_SUB, n_keys), 1)
        masks.append(cc <= rr)

    def chunk_body(c, carry):
        r0 = pl.multiple_of(c * GLA_CHUNK, GLA_CHUNK)
        rows = pl.ds(r0, GLA_CHUNK)
        b_all = _cumsum_rows(la_ref[rows, :])
        vals, new_states, o_inter, scores = [], [], [], []
        for h in range(GLA_HEADS):
            ks = slice(h * GLA_DK_PAD, (h + 1) * GLA_DK_PAD)
            vs = slice(h * GLA_DV_PAD, (h + 1) * GLA_DV_PAD)
            q = q_ref[rows, ks].astype(F32)
            k = k_ref[rows, ks].astype(F32)
            v = v_ref[rows, vs]
            b = b_all[:, ks]
            b_last = b[GLA_CHUNK - 1:GLA_CHUNK, :]
            st = st_refs[h][...]
            o_inter.append(_dot_nt((q * jnp.exp(b)).astype(BF16), st.astype(BF16)))
            kdec = (k * jnp.exp(b_last - b)).astype(BF16)
            new_states.append(st * jnp.exp(b_last) + _dot_tn(v, kdec))
            head_scores = []
            for i in range(n_sub):
                lo, hi = GLA_SUB * i, GLA_SUB * (i + 1)
                b_i = b[lo:hi, :]
                qd = q[lo:hi, :] * jnp.exp(b_i - b[lo - 1:lo, :]) if i else q[lo:hi, :] * jnp.exp(b_i)
                kd = k[:hi, :] * jnp.exp(b[lo - 1:lo, :] - b[:hi, :]) if i else k[:hi, :] * jnp.exp(-b[:hi, :])
                head_scores.append(_dot_nt(qd.astype(BF16), kd.astype(BF16)))
            scores.append(head_scores)
            vals.append(v)
        outs = []
        for h in range(GLA_HEADS):
            vs = slice(h * GLA_DV_PAD, (h + 1) * GLA_DV_PAD)
            o_parts = [_dot(jnp.where(masks[i], scores[h][i], 0.0).astype(BF16), vals[h][:GLA_SUB * (i + 1), :])
                       for i in range(n_sub)]
            o = o_inter[h] + jnp.concatenate(o_parts, axis=0)
            ms = jnp.sum(o * o, axis=-1, keepdims=True) * (1.0 / GLA_DV)
            y = o * lax.rsqrt(ms + EPS) * on_ref[...] * _silu(r_ref[rows, vs].astype(F32))
            outs.append(y.astype(BF16))
        for h in range(GLA_HEADS):
            st_refs[h][...] = new_states[h]
            o_ref[rows, h * GLA_DV_PAD:(h + 1) * GLA_DV_PAD] = outs[h]
        return carry

    lax.fori_loop(0, chunks, chunk_body, 0, unroll=4)


def _gla(q, k, v, r, la, onorm, bsz, seq, tm):
    t = q.shape[0]
    tiles_per_seq = seq // tm
    kw, vw = GLA_HEADS * GLA_DK_PAD, GLA_HEADS * GLA_DV_PAD
    row = lambda n: pl.BlockSpec((tm, n), lambda b, i: (b * tiles_per_seq + i, 0))
    return pl.pallas_call(
        functools.partial(_gla_kernel, tm // GLA_CHUNK),
        grid=(bsz, tiles_per_seq),
        in_specs=[row(kw), row(kw), row(vw), row(vw), row(kw), pl.BlockSpec((1, GLA_DV_PAD), lambda b, i: (0, 0))],
        out_specs=row(vw),
        out_shape=jax.ShapeDtypeStruct((t, vw), BF16),
        scratch_shapes=[pltpu.VMEM((GLA_DV_PAD, GLA_DK_PAD), F32)] * GLA_HEADS,
        compiler_params=_params(2),
        name="gla_scan",
    )(q, k, v, r, la, onorm)


def _fox_kernel(tile, q_ref, k_ref, v_ref, o_ref):
    i = pl.program_id(2)
    q = q_ref[...]
    qs = (q[:, :FOX_BLOCK], q[:, FOX_BLOCK:])
    blks = (slice(0, FOX_BLOCK), slice(FOX_BLOCK, 2 * FOX_BLOCK))

    def absorb(m, acc, s, v):
        m_new = jnp.maximum(m, jnp.max(s, axis=-1, keepdims=True))
        p = jnp.exp2(s - m_new).astype(BF16)
        return m_new, jnp.exp2(m - m_new) * acc + _dot(p, v)

    def step(j, state):
        keys = pl.ds(pl.multiple_of(j * tile, tile), tile)
        return tuple(absorb(*state[hh], _dot_nt(qs[hh], k_ref[keys, blks[hh]]), v_ref[keys, blks[hh]])
                     for hh in range(2))

    init = (jnp.full((tile, 1), -jnp.inf, F32), jnp.zeros((tile, FOX_BLOCK), F32))
    state = lax.fori_loop(0, i, step, (init, init))

    band = tile // FOX_DIAG_BANDS
    start = pl.multiple_of(i * tile, tile)
    bands = [(hh, r) for hh in range(2) for r in range(FOX_DIAG_BANDS)]
    band_rows = lambda r: slice(r * band, (r + 1) * band)
    band_keys = lambda r: pl.ds(start, (r + 1) * band)
    scores = {(hh, r): _dot_nt(qs[hh][band_rows(r)], k_ref[band_keys(r), blks[hh]]) for hh, r in bands}
    accs = {}
    for hh, r in bands:
        m, acc = state[hh]
        s = scores[hh, r]
        row = lax.broadcasted_iota(jnp.int32, s.shape, 0) + r * band
        col = lax.broadcasted_iota(jnp.int32, s.shape, 1)
        s = jnp.where(col <= row, s, -jnp.inf)
        _, accs[hh, r] = absorb(m[band_rows(r)], acc[band_rows(r)], s, v_ref[band_keys(r), blks[hh]])
    acc0, acc1 = (jnp.concatenate([accs[hh, r] for r in range(FOX_DIAG_BANDS)], axis=0) for hh in range(2))
    lane = lax.broadcasted_iota(jnp.int32, (tile, FOX_BLOCK), 1)
    row_sum = lambda acc: jnp.sum(jnp.where(lane == FOX_HEAD_DIM, acc, 0.0), axis=-1, keepdims=True)
    o0, o1 = acc0 / row_sum(acc0), acc1 / row_sum(acc1)
    o_ref[...] = jnp.where(lane < FOX_HEAD_DIM, o0, pltpu.roll(o1, FOX_HEAD_DIM, axis=1)).astype(BF16)


def _fox(q, k, v, bsz, seq, tile):
    t = q.shape[0]
    nq = seq // tile
    pairs = FOX_HEADS // 2
    return pl.pallas_call(
        functools.partial(_fox_kernel, tile),
        grid=(bsz, pairs, nq),
        in_specs=[pl.BlockSpec((tile, 2 * FOX_BLOCK), lambda b, p, i: (b * nq + i, p)),
                  pl.BlockSpec((seq, 2 * FOX_BLOCK), lambda b, p, i: (b, p)),
                  pl.BlockSpec((seq, 2 * FOX_BLOCK), lambda b, p, i: (b, p))],
        out_specs=pl.BlockSpec((tile, LANES), lambda b, p, i: (b * nq + i, p)),
        out_shape=jax.ShapeDtypeStruct((t, MAIN_WIDTH), BF16),
        compiler_params=_params(3),
        name="fox_attention",
    )(q, k, v)


def _out_kernel(x_ref, main_ref, mq_ref, mk_ref, mv_ref, wo_main_ref, wo_mem_ref, o_ref):
    mq, mk, mv = mq_ref[...], mk_ref[...], mv_ref[...]
    head = lambda a, h: a[:, h * MEM_HEAD_DIM:(h + 1) * MEM_HEAD_DIM]
    scores = [_dot_nt(head(mq, h), head(mk, h)) * (MEM_HEAD_DIM ** -0.5) for h in range(MEM_HEADS)]
    base = x_ref[...] + _dot(main_ref[...], wo_main_ref[...])
    heads = []
    for h in range(MEM_HEADS):
        s = scores[h]
        p = jnp.exp(s - jnp.max(s, axis=-1, keepdims=True))
        l = jnp.sum(p, axis=-1, keepdims=True)
        heads.append(_dot(p.astype(BF16), head(mv, h)) / l)
    mem_o = jnp.concatenate(heads, axis=-1).astype(BF16)
    o_ref[...] = base + _dot(mem_o, wo_mem_ref[...])


def _out_proj(x, main, mq, mem_k, mem_v, wo_main, wo_mem, seq, tm):
    t = x.shape[0]
    tiles_per_seq = seq // tm
    n_mem = mem_k.shape[1]
    row = lambda n: pl.BlockSpec((tm, n), lambda i: (i, 0))
    mem = pl.BlockSpec((None, n_mem, MEM_WIDTH), lambda i: (i // tiles_per_seq, 0, 0))
    return pl.pallas_call(
        _out_kernel,
        grid=(t // tm,),
        in_specs=[row(D_MODEL), row(main.shape[1]), row(MEM_WIDTH), mem, mem, _full(wo_main.shape),
                  _full(wo_mem.shape)],
        out_specs=row(D_MODEL),
        out_shape=jax.ShapeDtypeStruct((t, D_MODEL), F32),
        compiler_params=_params(1),
        name="out_proj_mem_attn",
    )(x, main, mq, mem_k, mem_v, wo_main, wo_mem)


def _swiglu_acc(hn, wg_ref, wu_ref, wd_ref, acc_ref, row_scale=None):
    for j in range(D_FF // FF_CHUNK):
        cs = slice(j * FF_CHUNK, (j + 1) * FF_CHUNK)
        a = _silu(_dot(hn, wg_ref[:, cs])) * _dot(hn, wu_ref[:, cs])
        if row_scale is not None:
            a = a * row_scale
        acc_ref[...] += _dot(a.astype(BF16), wd_ref[cs, :])


def _ffn_kernel(x_ref, g_ref, wg_ref, wu_ref, wd_ref, o_ref):
    x = x_ref[...]
    o_ref[...] = x
    _swiglu_acc(_rms_normed(x, g_ref[...]).astype(BF16), wg_ref, wu_ref, wd_ref, o_ref)


def _ffn(x, gain, wg, wu, wd, tm):
    t = x.shape[0]
    row = pl.BlockSpec((tm, D_MODEL), lambda i: (i, 0))
    return pl.pallas_call(
        _ffn_kernel,
        grid=(t // tm,),
        in_specs=[row, _full((1, D_MODEL)), _full(wg.shape), _full(wu.shape), _full(wd.shape)],
        out_specs=row,
        out_shape=jax.ShapeDtypeStruct((t, D_MODEL), F32),
        compiler_params=_params(1),
        name="dense_swiglu",
    )(x, gain, wg, wu, wd)


ROW_TILE = D_MODEL // LANES
ROUTE_E1, ROUTE_E2, ROUTE_W1, ROUTE_W2, ROUTE_RANK1, ROUTE_RANK2 = range(6)


def _rows_to_tiles(dst_ref, rows):
    n = rows.shape[0]
    for j in range(ROW_TILE):
        dst_ref[pl.ds(j, n, stride=ROW_TILE), :] = rows[:, j * LANES:(j + 1) * LANES]


def _tiles_to_rows(src_ref, n):
    return jnp.concatenate([src_ref[pl.ds(j, n, stride=ROW_TILE), :] for j in range(ROW_TILE)], axis=1)


def _row_tile(ref, row):
    return ref.at[pl.ds(pl.multiple_of(row * ROW_TILE, ROW_TILE), ROW_TILE)]


def _router_kernel(x_ref, g_ref, wr_ref, route_ref, fields_ref, counts_ref, carry_ref):
    @pl.when(pl.program_id(0) == 0)
    def _():
        carry_ref[...] = jnp.zeros_like(carry_ref)

    hn = _rms_normed(x_ref[...], g_ref[...])
    logits = _dot_3pass(hn, wr_ref[...])
    tm = logits.shape[0]
    lane = lax.broadcasted_iota(jnp.int32, logits.shape, 1)
    logits = jnp.where(lane < N_EXPERTS, logits, -jnp.inf)
    m1 = jnp.max(logits, axis=-1, keepdims=True)
    i1 = jnp.min(jnp.where(logits == m1, lane, LANES), axis=-1, keepdims=True)
    rest = jnp.where(lane == i1, -jnp.inf, logits)
    m2 = jnp.max(rest, axis=-1, keepdims=True)
    i2 = jnp.min(jnp.where(rest == m2, lane, LANES), axis=-1, keepdims=True)
    e2 = jnp.exp(m2 - m1)
    w1 = 1.0 / (1.0 + e2)
    chosen = jnp.where((lane == i1) | (lane == i2), 1.0, 0.0)
    rr = lax.broadcasted_iota(jnp.int32, (tm, tm), 0)
    cc = lax.broadcasted_iota(jnp.int32, (tm, tm), 1)
    before = jnp.where(cc < rr, 1.0, 0.0).astype(BF16)
    base = carry_ref[...] + _dot(before, chosen.astype(BF16))
    rank1 = jnp.sum(jnp.where(lane == i1, base, 0.0), axis=-1, keepdims=True)
    rank2 = jnp.sum(jnp.where(lane == i2, base, 0.0), axis=-1, keepdims=True)
    carry_ref[...] += jnp.sum(chosen, axis=0, keepdims=True)
    counts_ref[...] = carry_ref[...]
    rec = jnp.zeros_like(logits)
    for slot, val in ((ROUTE_E1, i1.astype(F32)), (ROUTE_E2, i2.astype(F32)), (ROUTE_W1, w1), (ROUTE_W2, e2 * w1),
                      (ROUTE_RANK1, rank1), (ROUTE_RANK2, rank2)):
        rec = jnp.where(lane == slot, val, rec)
    route_ref[...] = rec
    fields_ref[...] = rec.T[:fields_ref.shape[0], :]


def _router(x, gain, wr, tm):
    t = x.shape[0]
    row = lambda n: pl.BlockSpec((tm, n), lambda i: (i, 0))
    n_fields = 8
    return pl.pallas_call(
        _router_kernel,
        grid=(t // tm,),
        in_specs=[row(D_MODEL), _full((1, D_MODEL)), _full(wr.shape)],
        out_specs=[row(LANES), pl.BlockSpec((None, n_fields, tm), lambda i: (i, 0, 0)), _full((1, LANES))],
        out_shape=[jax.ShapeDtypeStruct((t, LANES), F32), jax.ShapeDtypeStruct((t // tm, n_fields, tm), F32),
                   jax.ShapeDtypeStruct((1, LANES), F32)],
        scratch_shapes=[pltpu.VMEM((1, LANES), F32)],
        compiler_params=_params(1),
        name="moe_router",
    )(x, gain, wr)


def _dispatch_kernel(tm, cnt_ref, pad_ref, off_ref, pos1_ref, pos2_ref, x_ref, g_ref, xs_ref, stage_ref, zero_ref,
                     sem, pad_sem):
    @pl.when(pl.program_id(0) == 0)
    def _():
        zero_ref[...] = jnp.zeros_like(zero_ref)
        for e in range(N_EXPERTS + 1):
            lo, hi = off_ref[e] + cnt_ref[e], off_ref[e] + pad_ref[e]

            def fill(r, c):
                pltpu.make_async_copy(zero_ref, _row_tile(xs_ref, r), pad_sem).start()
                return c

            def drain(r, c):
                pltpu.make_async_copy(zero_ref, _row_tile(xs_ref, r), pad_sem).wait()
                return c

            lax.fori_loop(lo, hi, fill, 0)
            lax.fori_loop(lo, hi, drain, 0)

    _rows_to_tiles(stage_ref, _rms_normed(x_ref[...], g_ref[...]))

    def issue(r, c):
        src = _row_tile(stage_ref, r)
        pltpu.make_async_copy(src, _row_tile(xs_ref, pos1_ref[0, r]), sem).start(priority=0)
        pltpu.make_async_copy(src, _row_tile(xs_ref, pos2_ref[0, r]), sem).start(priority=1)
        return c

    lax.fori_loop(0, tm, issue, 0, unroll=8)
    for _ in range(2):
        pltpu.make_async_copy(stage_ref, xs_ref.at[pl.ds(0, tm * ROW_TILE)], sem).wait()


def _dispatch(x, gain, pos1, pos2, counts, padded, offsets, n_rows, tm):
    t = x.shape[0]
    nt = t // tm
    smem_row = pl.BlockSpec((None, 1, tm), lambda i, *_: (i, 0, 0), memory_space=pltpu.SMEM)
    return pl.pallas_call(
        functools.partial(_dispatch_kernel, tm),
        grid_spec=pltpu.PrefetchScalarGridSpec(
            num_scalar_prefetch=3,
            grid=(nt,),
            in_specs=[smem_row, smem_row, pl.BlockSpec((tm, D_MODEL), lambda i, *_: (i, 0)),
                      pl.BlockSpec((1, D_MODEL), lambda i, *_: (0, 0))],
            out_specs=pl.BlockSpec(memory_space=pl.ANY),
            scratch_shapes=[pltpu.VMEM((tm * ROW_TILE, LANES), F32), pltpu.VMEM((ROW_TILE, LANES), F32),
                            pltpu.SemaphoreType.DMA, pltpu.SemaphoreType.DMA],
        ),
        out_shape=jax.ShapeDtypeStruct((n_rows * ROW_TILE, LANES), F32),
        compiler_params=_params(1),
        name="moe_dispatch",
    )(counts, padded, offsets, pos1.reshape(nt, 1, tm), pos2.reshape(nt, 1, tm), x, gain)


def _experts_kernel(tm, tile_expert_ref, n_used_ref, xs_ref, wg_ref, wu_ref, wd_ref, o_ref, acc_ref):
    acc_ref[...] = jnp.zeros_like(acc_ref)

    @pl.when(pl.program_id(0) < n_used_ref[0])
    def _():
        _swiglu_acc(_tiles_to_rows(xs_ref, tm).astype(BF16), wg_ref, wu_ref, wd_ref, acc_ref)

    _rows_to_tiles(o_ref, acc_ref[...])


def _experts(xs, tile_expert, n_used, wg, wu, wd, tm):
    n_tiles = xs.shape[0] // (tm * ROW_TILE)
    wspec = lambda shape: pl.BlockSpec(
        (None,) + shape, lambda i, te, nu: (te[jnp.minimum(i, nu[0] - 1)], 0, 0))
    rows = pl.BlockSpec((tm * ROW_TILE, LANES), lambda i, te, nu: (i, 0))
    return pl.pallas_call(
        functools.partial(_experts_kernel, tm),
        grid_spec=pltpu.PrefetchScalarGridSpec(
            num_scalar_prefetch=2,
            grid=(n_tiles,),
            in_specs=[rows, wspec((D_MODEL, D_FF)), wspec((D_MODEL, D_FF)), wspec((D_FF, D_MODEL))],
            out_specs=rows,
            scratch_shapes=[pltpu.VMEM((tm, D_MODEL), F32)],
        ),
        out_shape=jax.ShapeDtypeStruct(xs.shape, F32),
        compiler_params=_params(1),
        name="moe_experts",
    )(tile_expert, n_used, xs, wg, wu, wd)


def _combine_kernel(tm, n_tiles, pos1_ref, pos2_ref, next1_ref, next2_ref, x_ref, route_ref, gf_ref, y_ref,
                    o_ref, buf_ref, sems):
    i = pl.program_id(0)
    slot = i % 2

    def gather(p1_ref, p2_ref, slot):
        def issue(r, c):
            pltpu.make_async_copy(_row_tile(y_ref, p1_ref[0, r]), _row_tile(buf_ref.at[slot, 0], r),
                                  sems.at[slot]).start(priority=0)
            pltpu.make_async_copy(_row_tile(y_ref, p2_ref[0, r]), _row_tile(buf_ref.at[slot, 1], r),
                                  sems.at[slot]).start(priority=1)
            return c

        lax.fori_loop(0, tm, issue, 0, unroll=8)

    @pl.when(i == 0)
    def _():
        gather(pos1_ref, pos2_ref, 0)

    @pl.when(i + 1 < n_tiles)
    def _():
        gather(next1_ref, next2_ref, 1 - slot)

    route = route_ref[...]
    lane = lax.broadcasted_iota(jnp.int32, route.shape, 1)
    w1 = jnp.sum(jnp.where(lane == ROUTE_W1, route, 0.0), axis=-1, keepdims=True)
    w2 = jnp.sum(jnp.where(lane == ROUTE_W2, route, 0.0), axis=-1, keepdims=True)
    for k in range(2):
        pltpu.make_async_copy(y_ref.at[pl.ds(0, tm * ROW_TILE)], buf_ref.at[slot, k], sems.at[slot]).wait()
    y1, y2 = (_tiles_to_rows(buf_ref.at[slot, k], tm) for k in range(2))
    o_ref[...] = _rms_normed(x_ref[...] + w1 * y1 + w2 * y2, gf_ref[...])


def _combine(x, route, pos1, pos2, y, gain_final, tm):
    t = x.shape[0]
    nt = t // tm
    smem_row = lambda ahead: pl.BlockSpec((None, 1, tm), lambda i: (jnp.minimum(i + ahead, nt - 1), 0, 0),
                                          memory_space=pltpu.SMEM)
    row = lambda n: pl.BlockSpec((tm, n), lambda i: (i, 0))
    pos1, pos2 = pos1.reshape(nt, 1, tm), pos2.reshape(nt, 1, tm)
    return pl.pallas_call(
        functools.partial(_combine_kernel, tm, nt),
        grid=(nt,),
        in_specs=[smem_row(0), smem_row(0), smem_row(1), smem_row(1), row(D_MODEL), row(LANES),
                  _full((1, D_MODEL)), pl.BlockSpec(memory_space=pl.ANY)],
        out_specs=row(D_MODEL),
        out_shape=jax.ShapeDtypeStruct((t, D_MODEL), F32),
        scratch_shapes=[pltpu.VMEM((2, 2, tm * ROW_TILE, LANES), F32), pltpu.SemaphoreType.DMA((2,))],
        compiler_params=_params(1),
        name="moe_combine",
    )(pos1, pos2, pos1, pos2, x, route, gain_final, y)


def _moe(x, gain, gain_final, wr, wg, wu, wd, tm):
    t = x.shape[0]
    route, fields, counts = _router(x, gain, wr, tm)
    col = lambda c: fields[:, c, :].reshape(t).astype(jnp.int32)
    counts = counts[0, :N_EXPERTS].astype(jnp.int32)
    padded = (counts + tm - 1) // tm * tm
    ends = jnp.cumsum(padded)
    offsets = ends - padded
    pos1 = offsets[col(ROUTE_E1)] + col(ROUTE_RANK1)
    pos2 = offsets[col(ROUTE_E2)] + col(ROUTE_RANK2)
    n_rows = 2 * t + N_EXPERTS * tm
    tile_start = jnp.arange(n_rows // tm, dtype=jnp.int32) * tm
    tile_expert = jnp.minimum(jnp.sum(tile_start[:, None] >= ends[None, :], axis=1), N_EXPERTS - 1).astype(jnp.int32)
    n_used = (ends[-1:] // tm).astype(jnp.int32)
    tail = lambda a, v: jnp.concatenate([a, v.astype(jnp.int32)])
    xs = _dispatch(x, gain, pos1, pos2, tail(counts, jnp.zeros((1,))), tail(padded, n_rows - ends[-1:]),
                   tail(offsets, ends[-1:]), n_rows, tm)
    y = _experts(xs, tile_expert, n_used, wg, wu, wd, tm)
    return _combine(x, route, pos1, pos2, y, gain_final, tm // 2)


def _pad_heads(w, heads, width, padded):
    lead = w.shape[:-1]
    w = w.reshape(lead + (heads, width))
    w = jnp.pad(w, [(0, 0)] * len(lead) + [(0, 0), (0, padded - width)])
    return w.reshape(lead + (heads * padded,))


def _pad_cols(w, n):
    return jnp.pad(w, [(0, 0)] * (w.ndim - 1) + [(0, n - w.shape[-1])])


def kernel(x, mem, norm_mix, norm_mem, norm_ffn, norm_kv, norm_final, w_in_a, w_gla_gate2, b_gla_gate2,
           gla_onorm, w_in_b, w_kv, b_forget, w_mem_kv, w_out, w_ff_gate, w_ff_up, w_ff_down, w_router,
           w_moe_gate, w_moe_up, w_moe_down):
    bsz, seq, _ = x.shape
    n_mem = mem.shape[1]
    t = bsz * seq
    tm = 512
    xt = x.reshape(t, D_MODEL)
    memt = mem.reshape(bsz * n_mem, D_MODEL)
    gain = lambda g: g.reshape(1, D_MODEL)
    kw = GLA_HEADS * GLA_DK

    def mem_kv(layer):
        wk, wv = w_mem_kv[layer, :, :MEM_WIDTH], w_mem_kv[layer, :, MEM_WIDTH:]
        mk, mv = _norm_proj(memt, gain(norm_mem[layer]), [wk.astype(BF16), wv.astype(BF16)], (1.0, 1.0),
                            n_mem, "mem_kv_proj")
        return mk.reshape(bsz, n_mem, MEM_WIDTH), mv.reshape(bsz, n_mem, MEM_WIDTH)

    wa = w_in_a[0]
    wq = _pad_heads(wa[:, :kw], GLA_HEADS, GLA_DK, GLA_DK_PAD).astype(BF16)
    wk = _pad_heads(wa[:, kw:2 * kw], GLA_HEADS, GLA_DK, GLA_DK_PAD).astype(BF16)
    off = 2 * kw
    wv = _pad_heads(wa[:, off:off + MAIN_WIDTH], GLA_HEADS, GLA_DV, GLA_DV_PAD).astype(BF16)
    off += MAIN_WIDTH
    wr = _pad_heads(wa[:, off:off + MAIN_WIDTH], GLA_HEADS, GLA_DV, GLA_DV_PAD).astype(BF16)
    off += MAIN_WIDTH
    wg1 = _pad_cols(wa[:, off:off + GLA_GATE_RANK], LANES).astype(BF16)
    off += GLA_GATE_RANK
    wmq = wa[:, off:off + MEM_WIDTH].astype(BF16)
    wg2 = jnp.pad(_pad_heads(w_gla_gate2[0], GLA_HEADS, GLA_DK, GLA_DK_PAD), ((0, LANES - GLA_GATE_RANK), (0, 0)))
    bg2 = _pad_heads(b_gla_gate2[0], GLA_HEADS, GLA_DK, GLA_DK_PAD).reshape(1, -1)
    onorm = _pad_cols(gla_onorm[0], GLA_DV_PAD).reshape(1, GLA_DV_PAD)

    q, k, v, r, mq, la = _in_a(xt, gain(norm_mix[0]), wq, wk, wv, wr, wmq, wg1, wg2, bg2, tm)
    main = _gla(q, k, v, r, la, onorm, bsz, seq, tm)
    mk, mv = mem_kv(0)
    wo = w_out[0]
    wo_main = jnp.pad(wo[:MAIN_WIDTH].reshape(GLA_HEADS, GLA_DV, D_MODEL),
                      ((0, 0), (0, GLA_DV_PAD - GLA_DV), (0, 0))).reshape(GLA_HEADS * GLA_DV_PAD, D_MODEL)
    xt = _out_proj(xt, main, mq, mk, mv, wo_main.astype(BF16), wo[MAIN_WIDTH:].astype(BF16), seq, tm)
    xt = _ffn(xt, gain(norm_ffn[0]), w_ff_gate[0].astype(BF16), w_ff_up[0].astype(BF16),
              w_ff_down[0].astype(BF16), tm)

    wf = _pad_cols(w_kv[:, 2 * MAIN_WIDTH:], LANES).astype(BF16)
    bf = _pad_cols(b_forget, LANES).reshape(1, LANES)
    k_sh, v_sh, c = _kv_proj(xt, gain(norm_kv), w_kv[:, :MAIN_WIDTH].astype(BF16),
                             w_kv[:, MAIN_WIDTH:2 * MAIN_WIDTH].astype(BF16), wf, bf, seq, tm)

    wb = w_in_b[0]
    qf, mq = _in_b(xt, gain(norm_mix[1]), c, wb[:, :MAIN_WIDTH].astype(BF16), wb[:, MAIN_WIDTH:].astype(BF16), tm)
    main = _fox(qf, k_sh, v_sh, bsz, seq, 1024)
    mk, mv = mem_kv(1)
    wo = w_out[1]
    xt = _out_proj(xt, main, mq, mk, mv, wo[:MAIN_WIDTH].astype(BF16), wo[MAIN_WIDTH:].astype(BF16), seq, tm)
    out = _moe(xt, gain(norm_ffn[1]), gain(norm_final), _pad_cols(w_router[0], LANES), w_moe_gate[0].astype(BF16),
               w_moe_up[0].astype(BF16), w_moe_down[0].astype(BF16), tm)
    return out.reshape(bsz, seq, D_MODEL)
```

```python
import functools

import jax
import jax.numpy as jnp
import numpy as np
from jax import lax
from jax.experimental import pallas as pl
from jax.experimental.pallas import tpu as pltpu

D_MODEL = 1024
EPS = 1e-6
GLA_CHUNK = 64
GLA_SUB = 16
MEM_HEADS = 4
MEM_HEAD_DIM = 64
MEM_WIDTH = MEM_HEADS * MEM_HEAD_DIM
MAIN_WIDTH = D_MODEL - MEM_WIDTH
GLA_HEADS = 4
GLA_DK = MAIN_WIDTH // 2 // GLA_HEADS
GLA_DV = MAIN_WIDTH // GLA_HEADS
GLA_DK_PAD = 128
GLA_DV_PAD = 256
GLA_GATE_RANK = 16
GLA_GATE_TAU = 16.0
FOX_HEADS = 12
FOX_HEAD_DIM = 64
D_FF = 2816
N_EXPERTS = 8
LANES = 128
FF_CHUNK = 256
VMEM_LIMIT = 56 * 1024 * 1024

BF16 = jnp.bfloat16
F32 = jnp.float32


def _params(n_axes, vmem=VMEM_LIMIT):
    return pltpu.CompilerParams(dimension_semantics=("arbitrary",) * n_axes, vmem_limit_bytes=vmem)


def _rms_normed(x, gain):
    ms = jnp.mean(x * x, axis=-1, keepdims=True)
    return x * lax.rsqrt(ms + EPS) * gain


def _log_sigmoid(z):
    return jnp.minimum(z, 0.0) - jnp.log(1.0 + jnp.exp(-jnp.abs(z)))


def _silu(z):
    return z / (1.0 + jnp.exp(-z))


def _dot(a, b):
    return jnp.dot(a, b, preferred_element_type=F32)


def _dot_nt(a, b):
    return lax.dot_general(a, b, (((1,), (1,)), ((), ())), preferred_element_type=F32)


def _dot_tn(a, b):
    return lax.dot_general(a, b, (((0,), (0,)), ((), ())), preferred_element_type=F32)


def _bf16_terms(v):
    hi = v.astype(BF16).astype(F32)
    mid = (v - hi).astype(BF16).astype(F32)
    return hi, mid, v - hi - mid


def _dot_3pass(a, b):
    a_hi, a_lo, _ = _bf16_terms(a)
    b_hi, b_lo, _ = _bf16_terms(b)
    a_hi, a_lo, b_hi, b_lo = (t.astype(BF16) for t in (a_hi, a_lo, b_hi, b_lo))
    return _dot(a_hi, b_hi) + _dot(a_hi, b_lo) + _dot(a_lo, b_hi)


def _cumsum_rows(x):
    n = x.shape[0]
    r = lax.broadcasted_iota(jnp.int32, (n, n), 0)
    c = lax.broadcasted_iota(jnp.int32, (n, n), 1)
    tril = jnp.where(c <= r, 1.0, 0.0).astype(BF16)
    hi, mid, lo = _bf16_terms(x)
    return _dot(tril, hi.astype(BF16)) + _dot(tril, mid.astype(BF16)) + _dot(tril, lo.astype(BF16))


def _full(shape):
    return pl.BlockSpec(shape, lambda *_: (0,) * len(shape))


def _in_a_kernel(x_ref, g_ref, wq_ref, wk_ref, wv_ref, wr_ref, wmq_ref, wg1_ref, wg2_ref, bg2_ref,
                 q_ref, k_ref, v_ref, r_ref, mq_ref, la_ref):
    hn = _rms_normed(x_ref[...], g_ref[...]).astype(BF16)
    q_ref[...] = (_dot(hn, wq_ref[...]) * (GLA_DK ** -0.5)).astype(BF16)
    k_ref[...] = _dot(hn, wk_ref[...]).astype(BF16)
    v_ref[...] = _dot(hn, wv_ref[...]).astype(BF16)
    r_ref[...] = _dot(hn, wr_ref[...]).astype(BF16)
    mq_ref[...] = _dot(hn, wmq_ref[...]).astype(BF16)
    glr = _dot(hn, wg1_ref[...])
    z = _dot_3pass(glr, wg2_ref[...])
    la_ref[...] = _log_sigmoid(z + bg2_ref[...]) * (1.0 / GLA_GATE_TAU)


def _in_a(x, gain, wq, wk, wv, wr, wmq, wg1, wg2, bg2, tm):
    t = x.shape[0]
    row = lambda n: pl.BlockSpec((tm, n), lambda i: (i, 0))
    kw, vw = GLA_HEADS * GLA_DK_PAD, GLA_HEADS * GLA_DV_PAD
    return pl.pallas_call(
        _in_a_kernel,
        grid=(t // tm,),
        in_specs=[row(D_MODEL), _full((1, D_MODEL)), _full(wq.shape), _full(wk.shape), _full(wv.shape),
                  _full(wr.shape), _full(wmq.shape), _full(wg1.shape), _full(wg2.shape), _full(bg2.shape)],
        out_specs=[row(kw), row(kw), row(vw), row(vw), row(MEM_WIDTH), row(kw)],
        out_shape=[jax.ShapeDtypeStruct((t, kw), BF16), jax.ShapeDtypeStruct((t, kw), BF16),
                   jax.ShapeDtypeStruct((t, vw), BF16), jax.ShapeDtypeStruct((t, vw), BF16),
                   jax.ShapeDtypeStruct((t, MEM_WIDTH), BF16), jax.ShapeDtypeStruct((t, kw), F32)],
        compiler_params=_params(1),
        name="in_proj_gla",
    )(x, gain, wq, wk, wv, wr, wmq, wg1, wg2, bg2)


def _norm_proj_kernel(n_out, scales, x_ref, g_ref, *refs):
    w_refs, o_refs = refs[:n_out], refs[n_out:]
    hn = _rms_normed(x_ref[...], g_ref[...]).astype(BF16)
    for w_ref, o_ref, s in zip(w_refs, o_refs, scales):
        o = _dot(hn, w_ref[...])
        if s != 1.0:
            o = o * s
        o_ref[...] = o.astype(o_ref.dtype)


def _norm_proj(x, gain, weights, scales, tm, name):
    t = x.shape[0]
    row = lambda n: pl.BlockSpec((tm, n), lambda i: (i, 0))
    return pl.pallas_call(
        functools.partial(_norm_proj_kernel, len(weights), scales),
        grid=(t // tm,),
        in_specs=[row(D_MODEL), _full((1, D_MODEL))] + [_full(w.shape) for w in weights],
        out_specs=[row(w.shape[1]) for w in weights],
        out_shape=[jax.ShapeDtypeStruct((t, w.shape[1]), BF16) for w in weights],
        compiler_params=_params(1),
        name=name,
    )(x, gain, *weights)


FOX_BLOCK = LANES
FOX_WIDTH = FOX_HEADS * FOX_BLOCK
LOG2E = 1.4426950408889634
FOX_DIAG_BANDS = 4


FOX_ONE_LANE = 3 * FOX_HEADS


def _fox_bias_matrix(query_side):
    e = np.zeros((LANES, FOX_WIDTH), np.float32)
    for h in range(FOX_HEADS):
        base = h * FOX_BLOCK + FOX_HEAD_DIM
        for n in range(3):
            if query_side:
                e[n * FOX_HEADS + h, base + n] = 1.0
                e[FOX_ONE_LANE, base + 3 + n] = 1.0
            else:
                e[FOX_ONE_LANE, base + n] = 1.0
                e[n * FOX_HEADS + h, base + 3 + n] = -1.0
    return jnp.asarray(e, BF16)


def _fox_bias_lanes(c, place_ref):
    hi, mid, lo = _bf16_terms(c * LOG2E)
    lane = lax.broadcasted_iota(jnp.int32, c.shape, 1)
    packed = jnp.where(lane == FOX_ONE_LANE, 1.0, 0.0)
    for n, term in ((2, lo), (1, mid), (0, hi)):
        shifted = pltpu.roll(term, n * FOX_HEADS, axis=1) if n else term
        packed = jnp.where((lane >= n * FOX_HEADS) & (lane < (n + 1) * FOX_HEADS), shifted, packed)
    placed = _dot(packed.astype(BF16), place_ref[...])
    return [placed[:, h * FOX_BLOCK:(h + 1) * FOX_BLOCK] for h in range(FOX_HEADS)]


def _fox_blocks(dense, spare):
    lane = lax.broadcasted_iota(jnp.int32, (dense.shape[0], FOX_BLOCK), 1)
    blocks = []
    for h in range(FOX_HEADS):
        pair = dense[:, (h // 2) * FOX_BLOCK:(h // 2 + 1) * FOX_BLOCK]
        own = pair if h % 2 == 0 else pltpu.roll(pair, FOX_HEAD_DIM, axis=1)
        blocks.append(jnp.where(lane < FOX_HEAD_DIM, own, spare[h]))
    return jnp.concatenate(blocks, axis=-1)


def _kv_kernel(tiles_per_seq, x_ref, g_ref, wk_ref, wv_ref, wf_ref, bf_ref, place_ref, k_ref, v_ref, c_ref,
               carry_ref):
    @pl.when(pl.program_id(0) % tiles_per_seq == 0)
    def _():
        carry_ref[...] = jnp.zeros_like(carry_ref)

    hn = _rms_normed(x_ref[...], g_ref[...]).astype(BF16)
    log_f = _log_sigmoid(_dot(hn, wf_ref[...]) + bf_ref[...])
    tm = log_f.shape[0]
    c = carry_ref[...] + _cumsum_rows(log_f)
    carry_ref[...] = c[tm - 1:tm, :]
    c_ref[...] = c
    k_ref[...] = _fox_blocks(_dot(hn, wk_ref[...]), _fox_bias_lanes(c, place_ref)).astype(BF16)
    lane = lax.broadcasted_iota(jnp.int32, (tm, FOX_BLOCK), 1)
    row_sum_lane = jnp.where(lane == FOX_HEAD_DIM, 1.0, 0.0)
    v_ref[...] = _fox_blocks(_dot(hn, wv_ref[...]), [row_sum_lane] * FOX_HEADS).astype(BF16)


def _kv_proj(x, gain, wk, wv, wf, bf, seq, tm):
    t = x.shape[0]
    place = _fox_bias_matrix(query_side=False)
    row = lambda n: pl.BlockSpec((tm, n), lambda i: (i, 0))
    return pl.pallas_call(
        functools.partial(_kv_kernel, seq // tm),
        grid=(t // tm,),
        in_specs=[row(D_MODEL), _full((1, D_MODEL)), _full(wk.shape), _full(wv.shape), _full(wf.shape),
                  _full(bf.shape), _full(place.shape)],
        out_specs=[row(FOX_WIDTH), row(FOX_WIDTH), row(LANES)],
        out_shape=[jax.ShapeDtypeStruct((t, FOX_WIDTH), BF16), jax.ShapeDtypeStruct((t, FOX_WIDTH), BF16),
                   jax.ShapeDtypeStruct((t, LANES), F32)],
        scratch_shapes=[pltpu.VMEM((1, LANES), F32)],
        compiler_params=_params(1),
        name="kv_proj_fox",
    )(x, gain, wk, wv, wf, bf, place)


def _in_b_kernel(x_ref, g_ref, c_ref, wq_ref, wmq_ref, place_ref, q_ref, mq_ref):
    hn = _rms_normed(x_ref[...], g_ref[...]).astype(BF16)
    q = _dot(hn, wq_ref[...]) * (FOX_HEAD_DIM ** -0.5 * LOG2E)
    q_ref[...] = _fox_blocks(q, _fox_bias_lanes(c_ref[...], place_ref)).astype(BF16)
    mq_ref[...] = _dot(hn, wmq_ref[...]).astype(BF16)


def _in_b(x, gain, c, wq, wmq, tm):
    t = x.shape[0]
    place = _fox_bias_matrix(query_side=True)
    row = lambda n: pl.BlockSpec((tm, n), lambda i: (i, 0))
    return pl.pallas_call(
        _in_b_kernel,
        grid=(t // tm,),
        in_specs=[row(D_MODEL), _full((1, D_MODEL)), row(LANES), _full(wq.shape), _full(wmq.shape),
                  _full(place.shape)],
        out_specs=[row(FOX_WIDTH), row(MEM_WIDTH)],
        out_shape=[jax.ShapeDtypeStruct((t, FOX_WIDTH), BF16), jax.ShapeDtypeStruct((t, MEM_WIDTH), BF16)],
        compiler_params=_params(1),
        name="in_proj_fox",
    )(x, gain, c, wq, wmq, place)


def _gla_kernel(chunks, n_seq, q_ref, k_ref, v_ref, r_ref, la_ref, on_ref, o_ref, *st_refs):
    @pl.when(pl.program_id(1) == 0)
    def _():
        for st_ref in st_refs:
            st_ref[...] = jnp.zeros_like(st_ref)

    n_sub = GLA_CHUNK // GLA_SUB
    masks = []
    for i in range(n_sub):
        n_keys = GLA_SUB * (i + 1)
        rr = lax.broadcasted_iota(jnp.int32, (GLA_SUB, n_keys), 0) + GLA_SUB * i
        cc = lax.broadcasted_iota(jnp.int32, (GLA_SUB, n_keys), 1)
        masks.append(cc <= rr)

    def chunk_body(c, carry):
        r0 = pl.multiple_of(c * GLA_CHUNK, GLA_CHUNK)
        rows = pl.ds(r0, GLA_CHUNK)
        streams = [(s, h) for s in range(n_seq) for h in range(GLA_HEADS)]
        b_all = [_cumsum_rows(la_ref[s, rows, :]) for s in range(n_seq)]
        vals, new_states, o_inter, scores = {}, {}, {}, {}
        for s, h in streams:
            ks = slice(h * GLA_DK_PAD, (h + 1) * GLA_DK_PAD)
            vs = slice(h * GLA_DV_PAD, (h + 1) * GLA_DV_PAD)
            q = q_ref[s, rows, ks].astype(F32)
            k = k_ref[s, rows, ks].astype(F32)
            v = v_ref[s, rows, vs]
            b = b_all[s][:, ks]
            b_last = b[GLA_CHUNK - 1:GLA_CHUNK, :]
            st = st_refs[s * GLA_HEADS + h][...]
            o_inter[s, h] = _dot_nt((q * jnp.exp(b)).astype(BF16), st.astype(BF16))
            kdec = (k * jnp.exp(b_last - b)).astype(BF16)
            new_states[s, h] = st * jnp.exp(b_last) + _dot_tn(v, kdec)
            head_scores = []
            for i in range(n_sub):
                lo, hi = GLA_SUB * i, GLA_SUB * (i + 1)
                b_i = b[lo:hi, :]
                qd = q[lo:hi, :] * jnp.exp(b_i - b[lo - 1:lo, :]) if i else q[lo:hi, :] * jnp.exp(b_i)
                kd = k[:hi, :] * jnp.exp(b[lo - 1:lo, :] - b[:hi, :]) if i else k[:hi, :] * jnp.exp(-b[:hi, :])
                head_scores.append(_dot_nt(qd.astype(BF16), kd.astype(BF16)))
            scores[s, h] = head_scores
            vals[s, h] = v
        outs = {}
        for s, h in streams:
            vs = slice(h * GLA_DV_PAD, (h + 1) * GLA_DV_PAD)
            o_parts = [_dot(jnp.where(masks[i], scores[s, h][i], 0.0).astype(BF16),
                            vals[s, h][:GLA_SUB * (i + 1), :]) for i in range(n_sub)]
            o = o_inter[s, h] + jnp.concatenate(o_parts, axis=0)
            ms = jnp.sum(o * o, axis=-1, keepdims=True) * (1.0 / GLA_DV)
            y = o * lax.rsqrt(ms + EPS) * on_ref[...] * _silu(r_ref[s, rows, vs].astype(F32))
            outs[s, h] = y.astype(BF16)
        for s, h in streams:
            st_refs[s * GLA_HEADS + h][...] = new_states[s, h]
            o_ref[s, rows, h * GLA_DV_PAD:(h + 1) * GLA_DV_PAD] = outs[s, h]
        return carry

    lax.fori_loop(0, chunks, chunk_body, 0, unroll=4)


def _gla(q, k, v, r, la, onorm, bsz, seq, tm):
    t = q.shape[0]
    n_seq = 2 if bsz % 2 == 0 else 1
    kw, vw = GLA_HEADS * GLA_DK_PAD, GLA_HEADS * GLA_DV_PAD
    rows = lambda n: pl.BlockSpec((n_seq, tm, n), lambda b, i: (b, i, 0))
    per_seq = lambda a: a.reshape(bsz, seq, a.shape[-1])
    out = pl.pallas_call(
        functools.partial(_gla_kernel, tm // GLA_CHUNK, n_seq),
        grid=(bsz // n_seq, seq // tm),
        in_specs=[rows(kw), rows(kw), rows(vw), rows(vw), rows(kw),
                  pl.BlockSpec((1, GLA_DV_PAD), lambda b, i: (0, 0))],
        out_specs=rows(vw),
        out_shape=jax.ShapeDtypeStruct((bsz, seq, vw), BF16),
        scratch_shapes=[pltpu.VMEM((GLA_DV_PAD, GLA_DK_PAD), F32)] * (GLA_HEADS * n_seq),
        compiler_params=_params(2),
        name="gla_scan",
    )(per_seq(q), per_seq(k), per_seq(v), per_seq(r), per_seq(la), onorm)
    return out.reshape(t, vw)


def _fox_kernel(tile, q_ref, k_ref, v_ref, o_ref):
    i = pl.program_id(2)
    q = q_ref[...]
    qs = (q[:, :FOX_BLOCK], q[:, FOX_BLOCK:])
    blks = (slice(0, FOX_BLOCK), slice(FOX_BLOCK, 2 * FOX_BLOCK))

    def absorb(m, acc, s, v):
        m_new = jnp.maximum(m, jnp.max(s, axis=-1, keepdims=True))
        p = jnp.exp2(s - m_new).astype(BF16)
        return m_new, jnp.exp2(m - m_new) * acc + _dot(p, v)

    def step(j, state):
        keys = pl.ds(pl.multiple_of(j * tile, tile), tile)
        return tuple(absorb(*state[hh], _dot_nt(qs[hh], k_ref[keys, blks[hh]]), v_ref[keys, blks[hh]])
                     for hh in range(2))

    init = (jnp.full((tile, 1), -jnp.inf, F32), jnp.zeros((tile, FOX_BLOCK), F32))
    state = lax.fori_loop(0, i, step, (init, init))

    band = tile // FOX_DIAG_BANDS
    start = pl.multiple_of(i * tile, tile)
    bands = [(hh, r) for r in range(FOX_DIAG_BANDS) for hh in range(2)]
    band_rows = lambda r: slice(r * band, (r + 1) * band)
    band_keys = lambda r: pl.ds(start, (r + 1) * band)
    scores = {(hh, r): _dot_nt(qs[hh][band_rows(r)], k_ref[band_keys(r), blks[hh]]) for hh, r in bands}
    causal = (lax.broadcasted_iota(jnp.int32, (band, band), 1) <= lax.broadcasted_iota(jnp.int32, (band, band), 0))
    lane = lax.broadcasted_iota(jnp.int32, (band, FOX_BLOCK), 1)
    for r in range(FOX_DIAG_BANDS):
        outs = []
        for hh in range(2):
            m, acc = state[hh]
            s = scores[hh, r]
            tail = jnp.where(causal, s[:, r * band:], -jnp.inf)
            s = jnp.concatenate([s[:, :r * band], tail], axis=1) if r else tail
            _, acc = absorb(m[band_rows(r)], acc[band_rows(r)], s, v_ref[band_keys(r), blks[hh]])
            outs.append(acc / jnp.sum(jnp.where(lane == FOX_HEAD_DIM, acc, 0.0), axis=-1, keepdims=True))
        o_ref[band_rows(r), :] = jnp.where(lane < FOX_HEAD_DIM, outs[0],
                                           pltpu.roll(outs[1], FOX_HEAD_DIM, axis=1)).astype(BF16)


def _fox(q, k, v, bsz, seq, tile):
    t = q.shape[0]
    nq = seq // tile
    pairs = FOX_HEADS // 2
    return pl.pallas_call(
        functools.partial(_fox_kernel, tile),
        grid=(bsz, pairs, nq),
        in_specs=[pl.BlockSpec((tile, 2 * FOX_BLOCK), lambda b, p, i: (b * nq + i, p)),
                  pl.BlockSpec((seq, 2 * FOX_BLOCK), lambda b, p, i: (b, p)),
                  pl.BlockSpec((seq, 2 * FOX_BLOCK), lambda b, p, i: (b, p))],
        out_specs=pl.BlockSpec((tile, LANES), lambda b, p, i: (b * nq + i, p)),
        out_shape=jax.ShapeDtypeStruct((t, MAIN_WIDTH), BF16),
        compiler_params=_params(3),
        name="fox_attention",
    )(q, k, v)


def _out_kernel(x_ref, main_ref, mq_ref, mk_ref, mv_ref, wo_main_ref, wo_mem_ref, o_ref):
    mq, mk, mv = mq_ref[...], mk_ref[...], mv_ref[...]
    head = lambda a, h: a[:, h * MEM_HEAD_DIM:(h + 1) * MEM_HEAD_DIM]
    scores = [_dot_nt(head(mq, h), head(mk, h)) * (MEM_HEAD_DIM ** -0.5) for h in range(MEM_HEADS)]
    base = x_ref[...] + _dot(main_ref[...], wo_main_ref[...])
    heads = []
    for h in range(MEM_HEADS):
        s = scores[h]
        p = jnp.exp(s - jnp.max(s, axis=-1, keepdims=True))
        l = jnp.sum(p, axis=-1, keepdims=True)
        heads.append(_dot(p.astype(BF16), head(mv, h)) / l)
    mem_o = jnp.concatenate(heads, axis=-1).astype(BF16)
    o_ref[...] = base + _dot(mem_o, wo_mem_ref[...])


def _out_proj(x, main, mq, mem_k, mem_v, wo_main, wo_mem, seq, tm):
    t = x.shape[0]
    tiles_per_seq = seq // tm
    n_mem = mem_k.shape[1]
    row = lambda n: pl.BlockSpec((tm, n), lambda i: (i, 0))
    mem = pl.BlockSpec((None, n_mem, MEM_WIDTH), lambda i: (i // tiles_per_seq, 0, 0))
    return pl.pallas_call(
        _out_kernel,
        grid=(t // tm,),
        in_specs=[row(D_MODEL), row(main.shape[1]), row(MEM_WIDTH), mem, mem, _full(wo_main.shape),
                  _full(wo_mem.shape)],
        out_specs=row(D_MODEL),
        out_shape=jax.ShapeDtypeStruct((t, D_MODEL), F32),
        compiler_params=_params(1),
        name="out_proj_mem_attn",
    )(x, main, mq, mem_k, mem_v, wo_main, wo_mem)


def _swiglu_acc(hn, wg_ref, wu_ref, wd_ref, acc_ref, row_scale=None):
    for j in range(D_FF // FF_CHUNK):
        cs = slice(j * FF_CHUNK, (j + 1) * FF_CHUNK)
        a = _silu(_dot(hn, wg_ref[:, cs])) * _dot(hn, wu_ref[:, cs])
        if row_scale is not None:
            a = a * row_scale
        acc_ref[...] += _dot(a.astype(BF16), wd_ref[cs, :])


def _ffn_kernel(x_ref, g_ref, wg_ref, wu_ref, wd_ref, o_ref):
    x = x_ref[...]
    o_ref[...] = x
    _swiglu_acc(_rms_normed(x, g_ref[...]).astype(BF16), wg_ref, wu_ref, wd_ref, o_ref)


def _ffn(x, gain, wg, wu, wd, tm):
    t = x.shape[0]
    row = pl.BlockSpec((tm, D_MODEL), lambda i: (i, 0))
    return pl.pallas_call(
        _ffn_kernel,
        grid=(t // tm,),
        in_specs=[row, _full((1, D_MODEL)), _full(wg.shape), _full(wu.shape), _full(wd.shape)],
        out_specs=row,
        out_shape=jax.ShapeDtypeStruct((t, D_MODEL), F32),
        compiler_params=_params(1),
        name="dense_swiglu",
    )(x, gain, wg, wu, wd)


ROW_TILE = D_MODEL // LANES
ROUTE_E1, ROUTE_E2, ROUTE_W1, ROUTE_W2, ROUTE_RANK1, ROUTE_RANK2 = range(6)


def _rows_to_tiles(dst_ref, rows):
    n = rows.shape[0]
    for j in range(ROW_TILE):
        dst_ref[pl.ds(j, n, stride=ROW_TILE), :] = rows[:, j * LANES:(j + 1) * LANES]


def _tiles_to_rows(src_ref, n):
    return jnp.concatenate([src_ref[pl.ds(j, n, stride=ROW_TILE), :] for j in range(ROW_TILE)], axis=1)


def _row_tile(ref, row):
    return ref.at[pl.ds(pl.multiple_of(row * ROW_TILE, ROW_TILE), ROW_TILE)]


def _router_kernel(x_ref, g_ref, wr_ref, route_ref, fields_ref, counts_ref, carry_ref):
    @pl.when(pl.program_id(0) == 0)
    def _():
        carry_ref[...] = jnp.zeros_like(carry_ref)

    hn = _rms_normed(x_ref[...], g_ref[...])
    logits = _dot_3pass(hn, wr_ref[...])
    tm = logits.shape[0]
    lane = lax.broadcasted_iota(jnp.int32, logits.shape, 1)
    logits = jnp.where(lane < N_EXPERTS, logits, -jnp.inf)
    m1 = jnp.max(logits, axis=-1, keepdims=True)
    i1 = jnp.min(jnp.where(logits == m1, lane, LANES), axis=-1, keepdims=True)
    rest = jnp.where(lane == i1, -jnp.inf, logits)
    m2 = jnp.max(rest, axis=-1, keepdims=True)
    i2 = jnp.min(jnp.where(rest == m2, lane, LANES), axis=-1, keepdims=True)
    e2 = jnp.exp(m2 - m1)
    w1 = 1.0 / (1.0 + e2)
    chosen = jnp.where((lane == i1) | (lane == i2), 1.0, 0.0)
    rr = lax.broadcasted_iota(jnp.int32, (tm, tm), 0)
    cc = lax.broadcasted_iota(jnp.int32, (tm, tm), 1)
    before = jnp.where(cc < rr, 1.0, 0.0).astype(BF16)
    base = carry_ref[...] + _dot(before, chosen.astype(BF16))
    rank1 = jnp.sum(jnp.where(lane == i1, base, 0.0), axis=-1, keepdims=True)
    rank2 = jnp.sum(jnp.where(lane == i2, base, 0.0), axis=-1, keepdims=True)
    carry_ref[...] += jnp.sum(chosen, axis=0, keepdims=True)
    counts_ref[...] = carry_ref[...]
    rec = jnp.zeros_like(logits)
    for slot, val in ((ROUTE_E1, i1.astype(F32)), (ROUTE_E2, i2.astype(F32)), (ROUTE_W1, w1), (ROUTE_W2, e2 * w1),
                      (ROUTE_RANK1, rank1), (ROUTE_RANK2, rank2)):
        rec = jnp.where(lane == slot, val, rec)
    route_ref[...] = rec
    fields_ref[...] = rec.T[:fields_ref.shape[0], :]


def _router(x, gain, wr, tm):
    t = x.shape[0]
    row = lambda n: pl.BlockSpec((tm, n), lambda i: (i, 0))
    n_fields = 8
    return pl.pallas_call(
        _router_kernel,
        grid=(t // tm,),
        in_specs=[row(D_MODEL), _full((1, D_MODEL)), _full(wr.shape)],
        out_specs=[row(LANES), pl.BlockSpec((None, n_fields, tm), lambda i: (i, 0, 0)), _full((1, LANES))],
        out_shape=[jax.ShapeDtypeStruct((t, LANES), F32), jax.ShapeDtypeStruct((t // tm, n_fields, tm), F32),
                   jax.ShapeDtypeStruct((1, LANES), F32)],
        scratch_shapes=[pltpu.VMEM((1, LANES), F32)],
        compiler_params=_params(1),
        name="moe_router",
    )(x, gain, wr)


def _dispatch_kernel(tm, cnt_ref, pad_ref, off_ref, pos1_ref, pos2_ref, x_ref, g_ref, xs_ref, stage_ref, zero_ref,
                     sem, pad_sem):
    @pl.when(pl.program_id(0) == 0)
    def _():
        zero_ref[...] = jnp.zeros_like(zero_ref)
        for e in range(N_EXPERTS + 1):
            lo, hi = off_ref[e] + cnt_ref[e], off_ref[e] + pad_ref[e]

            def fill(r, c):
                pltpu.make_async_copy(zero_ref, _row_tile(xs_ref, r), pad_sem).start()
                return c

            def drain(r, c):
                pltpu.make_async_copy(zero_ref, _row_tile(xs_ref, r), pad_sem).wait()
                return c

            lax.fori_loop(lo, hi, fill, 0)
            lax.fori_loop(lo, hi, drain, 0)

    _rows_to_tiles(stage_ref, _rms_normed(x_ref[...], g_ref[...]))

    def issue(r, c):
        src = _row_tile(stage_ref, r)
        pltpu.make_async_copy(src, _row_tile(xs_ref, pos1_ref[0, r]), sem).start(priority=0)
        pltpu.make_async_copy(src, _row_tile(xs_ref, pos2_ref[0, r]), sem).start(priority=1)
        return c

    lax.fori_loop(0, tm, issue, 0, unroll=8)
    for _ in range(2):
        pltpu.make_async_copy(stage_ref, xs_ref.at[pl.ds(0, tm * ROW_TILE)], sem).wait()


def _dispatch(x, gain, pos1, pos2, counts, padded, offsets, n_rows, tm):
    t = x.shape[0]
    nt = t // tm
    smem_row = pl.BlockSpec((None, 1, tm), lambda i, *_: (i, 0, 0), memory_space=pltpu.SMEM)
    return pl.pallas_call(
        functools.partial(_dispatch_kernel, tm),
        grid_spec=pltpu.PrefetchScalarGridSpec(
            num_scalar_prefetch=3,
            grid=(nt,),
            in_specs=[smem_row, smem_row, pl.BlockSpec((tm, D_MODEL), lambda i, *_: (i, 0)),
                      pl.BlockSpec((1, D_MODEL), lambda i, *_: (0, 0))],
            out_specs=pl.BlockSpec(memory_space=pl.ANY),
            scratch_shapes=[pltpu.VMEM((tm * ROW_TILE, LANES), F32), pltpu.VMEM((ROW_TILE, LANES), F32),
                            pltpu.SemaphoreType.DMA, pltpu.SemaphoreType.DMA],
        ),
        out_shape=jax.ShapeDtypeStruct((n_rows * ROW_TILE, LANES), F32),
        compiler_params=_params(1),
        name="moe_dispatch",
    )(counts, padded, offsets, pos1.reshape(nt, 1, tm), pos2.reshape(nt, 1, tm), x, gain)


def _experts_kernel(tm, tile_expert_ref, n_used_ref, xs_ref, wg_ref, wu_ref, wd_ref, o_ref, acc_ref):
    acc_ref[...] = jnp.zeros_like(acc_ref)

    @pl.when(pl.program_id(0) < n_used_ref[0])
    def _():
        _swiglu_acc(_tiles_to_rows(xs_ref, tm).astype(BF16), wg_ref, wu_ref, wd_ref, acc_ref)

    _rows_to_tiles(o_ref, acc_ref[...])


def _experts(xs, tile_expert, n_used, wg, wu, wd, tm):
    n_tiles = xs.shape[0] // (tm * ROW_TILE)
    wspec = lambda shape: pl.BlockSpec(
        (None,) + shape, lambda i, te, nu: (te[jnp.minimum(i, nu[0] - 1)], 0, 0))
    rows = pl.BlockSpec((tm * ROW_TILE, LANES), lambda i, te, nu: (i, 0))
    return pl.pallas_call(
        functools.partial(_experts_kernel, tm),
        grid_spec=pltpu.PrefetchScalarGridSpec(
            num_scalar_prefetch=2,
            grid=(n_tiles,),
            in_specs=[rows, wspec((D_MODEL, D_FF)), wspec((D_MODEL, D_FF)), wspec((D_FF, D_MODEL))],
            out_specs=rows,
            scratch_shapes=[pltpu.VMEM((tm, D_MODEL), F32)],
        ),
        out_shape=jax.ShapeDtypeStruct(xs.shape, F32),
        compiler_params=_params(1),
        name="moe_experts",
    )(tile_expert, n_used, xs, wg, wu, wd)


def _combine_kernel(tm, n_tiles, pos1_ref, pos2_ref, next1_ref, next2_ref, x_ref, route_ref, gf_ref, y_ref,
                    o_ref, buf_ref, sems):
    i = pl.program_id(0)
    slot = i % 2

    def gather(p1_ref, p2_ref, slot):
        def issue(r, c):
            pltpu.make_async_copy(_row_tile(y_ref, p1_ref[0, r]), _row_tile(buf_ref.at[slot, 0], r),
                                  sems.at[slot]).start(priority=0)
            pltpu.make_async_copy(_row_tile(y_ref, p2_ref[0, r]), _row_tile(buf_ref.at[slot, 1], r),
                                  sems.at[slot]).start(priority=1)
            return c

        lax.fori_loop(0, tm, issue, 0, unroll=8)

    @pl.when(i == 0)
    def _():
        gather(pos1_ref, pos2_ref, 0)

    @pl.when(i + 1 < n_tiles)
    def _():
        gather(next1_ref, next2_ref, 1 - slot)

    route = route_ref[...]
    lane = lax.broadcasted_iota(jnp.int32, route.shape, 1)
    w1 = jnp.sum(jnp.where(lane == ROUTE_W1, route, 0.0), axis=-1, keepdims=True)
    w2 = jnp.sum(jnp.where(lane == ROUTE_W2, route, 0.0), axis=-1, keepdims=True)
    for k in range(2):
        pltpu.make_async_copy(y_ref.at[pl.ds(0, tm * ROW_TILE)], buf_ref.at[slot, k], sems.at[slot]).wait()
    y1, y2 = (_tiles_to_rows(buf_ref.at[slot, k], tm) for k in range(2))
    o_ref[...] = _rms_normed(x_ref[...] + w1 * y1 + w2 * y2, gf_ref[...])


def _combine(x, route, pos1, pos2, y, gain_final, tm):
    t = x.shape[0]
    nt = t // tm
    smem_row = lambda ahead: pl.BlockSpec((None, 1, tm), lambda i: (jnp.minimum(i + ahead, nt - 1), 0, 0),
                                          memory_space=pltpu.SMEM)
    row = lambda n: pl.BlockSpec((tm, n), lambda i: (i, 0))
    pos1, pos2 = pos1.reshape(nt, 1, tm), pos2.reshape(nt, 1, tm)
    return pl.pallas_call(
        functools.partial(_combine_kernel, tm, nt),
        grid=(nt,),
        in_specs=[smem_row(0), smem_row(0), smem_row(1), smem_row(1), row(D_MODEL), row(LANES),
                  _full((1, D_MODEL)), pl.BlockSpec(memory_space=pl.ANY)],
        out_specs=row(D_MODEL),
        out_shape=jax.ShapeDtypeStruct((t, D_MODEL), F32),
        scratch_shapes=[pltpu.VMEM((2, 2, tm * ROW_TILE, LANES), F32), pltpu.SemaphoreType.DMA((2,))],
        compiler_params=_params(1),
        name="moe_combine",
    )(pos1, pos2, pos1, pos2, x, route, gain_final, y)


def _moe(x, gain, gain_final, wr, wg, wu, wd, tm):
    t = x.shape[0]
    route, fields, counts = _router(x, gain, wr, tm)
    col = lambda c: fields[:, c, :].reshape(t).astype(jnp.int32)
    counts = counts[0, :N_EXPERTS].astype(jnp.int32)
    padded = (counts + tm - 1) // tm * tm
    ends = jnp.cumsum(padded)
    offsets = ends - padded
    pos1 = offsets[col(ROUTE_E1)] + col(ROUTE_RANK1)
    pos2 = offsets[col(ROUTE_E2)] + col(ROUTE_RANK2)
    n_rows = 2 * t + N_EXPERTS * tm
    tile_start = jnp.arange(n_rows // tm, dtype=jnp.int32) * tm
    tile_expert = jnp.minimum(jnp.sum(tile_start[:, None] >= ends[None, :], axis=1), N_EXPERTS - 1).astype(jnp.int32)
    n_used = (ends[-1:] // tm).astype(jnp.int32)
    tail = lambda a, v: jnp.concatenate([a, v.astype(jnp.int32)])
    xs = _dispatch(x, gain, pos1, pos2, tail(counts, jnp.zeros((1,))), tail(padded, n_rows - ends[-1:]),
                   tail(offsets, ends[-1:]), n_rows, tm)
    y = _experts(xs, tile_expert, n_used, wg, wu, wd, tm)
    return _combine(x, route, pos1, pos2, y, gain_final, tm // 2)


def _pad_heads(w, heads, width, padded):
    lead = w.shape[:-1]
    w = w.reshape(lead + (heads, width))
    w = jnp.pad(w, [(0, 0)] * len(lead) + [(0, 0), (0, padded - width)])
    return w.reshape(lead + (heads * padded,))


def _pad_cols(w, n):
    return jnp.pad(w, [(0, 0)] * (w.ndim - 1) + [(0, n - w.shape[-1])])


def kernel(x, mem, norm_mix, norm_mem, norm_ffn, norm_kv, norm_final, w_in_a, w_gla_gate2, b_gla_gate2,
           gla_onorm, w_in_b, w_kv, b_forget, w_mem_kv, w_out, w_ff_gate, w_ff_up, w_ff_down, w_router,
           w_moe_gate, w_moe_up, w_moe_down):
    bsz, seq, _ = x.shape
    n_mem = mem.shape[1]
    t = bsz * seq
    tm = 512
    xt = x.reshape(t, D_MODEL)
    memt = mem.reshape(bsz * n_mem, D_MODEL)
    gain = lambda g: g.reshape(1, D_MODEL)
    kw = GLA_HEADS * GLA_DK

    def mem_kv(layer):
        wk, wv = w_mem_kv[layer, :, :MEM_WIDTH], w_mem_kv[layer, :, MEM_WIDTH:]
        mk, mv = _norm_proj(memt, gain(norm_mem[layer]), [wk.astype(BF16), wv.astype(BF16)], (1.0, 1.0),
                            n_mem, "mem_kv_proj")
        return mk.reshape(bsz, n_mem, MEM_WIDTH), mv.reshape(bsz, n_mem, MEM_WIDTH)

    wa = w_in_a[0]
    wq = _pad_heads(wa[:, :kw], GLA_HEADS, GLA_DK, GLA_DK_PAD).astype(BF16)
    wk = _pad_heads(wa[:, kw:2 * kw], GLA_HEADS, GLA_DK, GLA_DK_PAD).astype(BF16)
    off = 2 * kw
    wv = _pad_heads(wa[:, off:off + MAIN_WIDTH], GLA_HEADS, GLA_DV, GLA_DV_PAD).astype(BF16)
    off += MAIN_WIDTH
    wr = _pad_heads(wa[:, off:off + MAIN_WIDTH], GLA_HEADS, GLA_DV, GLA_DV_PAD).astype(BF16)
    off += MAIN_WIDTH
    wg1 = _pad_cols(wa[:, off:off + GLA_GATE_RANK], LANES).astype(BF16)
    off += GLA_GATE_RANK
    wmq = wa[:, off:off + MEM_WIDTH].astype(BF16)
    wg2 = jnp.pad(_pad_heads(w_gla_gate2[0], GLA_HEADS, GLA_DK, GLA_DK_PAD), ((0, LANES - GLA_GATE_RANK), (0, 0)))
    bg2 = _pad_heads(b_gla_gate2[0], GLA_HEADS, GLA_DK, GLA_DK_PAD).reshape(1, -1)
    onorm = _pad_cols(gla_onorm[0], GLA_DV_PAD).reshape(1, GLA_DV_PAD)

    q, k, v, r, mq, la = _in_a(xt, gain(norm_mix[0]), wq, wk, wv, wr, wmq, wg1, wg2, bg2, 2 * tm)
    main = _gla(q, k, v, r, la, onorm, bsz, seq, tm)
    mk, mv = mem_kv(0)
    wo = w_out[0]
    wo_main = jnp.pad(wo[:MAIN_WIDTH].reshape(GLA_HEADS, GLA_DV, D_MODEL),
                      ((0, 0), (0, GLA_DV_PAD - GLA_DV), (0, 0))).reshape(GLA_HEADS * GLA_DV_PAD, D_MODEL)
    xt = _out_proj(xt, main, mq, mk, mv, wo_main.astype(BF16), wo[MAIN_WIDTH:].astype(BF16), seq, 2 * tm)
    xt = _ffn(xt, gain(norm_ffn[0]), w_ff_gate[0].astype(BF16), w_ff_up[0].astype(BF16),
              w_ff_down[0].astype(BF16), tm)

    wf = _pad_cols(w_kv[:, 2 * MAIN_WIDTH:], LANES).astype(BF16)
    bf = _pad_cols(b_forget, LANES).reshape(1, LANES)
    k_sh, v_sh, c = _kv_proj(xt, gain(norm_kv), w_kv[:, :MAIN_WIDTH].astype(BF16),
                             w_kv[:, MAIN_WIDTH:2 * MAIN_WIDTH].astype(BF16), wf, bf, seq, tm)

    wb = w_in_b[0]
    qf, mq = _in_b(xt, gain(norm_mix[1]), c, wb[:, :MAIN_WIDTH].astype(BF16), wb[:, MAIN_WIDTH:].astype(BF16),
                   2 * tm)
    main = _fox(qf, k_sh, v_sh, bsz, seq, 1024)
    mk, mv = mem_kv(1)
    wo = w_out[1]
    xt = _out_proj(xt, main, mq, mk, mv, wo[:MAIN_WIDTH].astype(BF16), wo[MAIN_WIDTH:].astype(BF16), seq, 2 * tm)
    out = _moe(xt, gain(norm_ffn[1]), gain(norm_final), _pad_cols(w_router[0], LANES), w_moe_gate[0].astype(BF16),
               w_moe_up[0].astype(BF16), w_moe_down[0].astype(BF16), tm)
    return out.reshape(bsz, seq, D_MODEL)
```

```python
import functools

import jax
import jax.numpy as jnp
import numpy as np
from jax import lax
from jax.experimental import pallas as pl
from jax.experimental.pallas import tpu as pltpu

D_MODEL = 1024
EPS = 1e-6
GLA_CHUNK = 64
GLA_SUB = 16
MEM_HEADS = 4
MEM_HEAD_DIM = 64
MEM_WIDTH = MEM_HEADS * MEM_HEAD_DIM
MAIN_WIDTH = D_MODEL - MEM_WIDTH
GLA_HEADS = 4
GLA_DK = MAIN_WIDTH // 2 // GLA_HEADS
GLA_DV = MAIN_WIDTH // GLA_HEADS
GLA_DK_PAD = 128
GLA_DV_PAD = 256
GLA_GATE_RANK = 16
GLA_GATE_TAU = 16.0
FOX_HEADS = 12
FOX_HEAD_DIM = 64
D_FF = 2816
N_EXPERTS = 8
LANES = 128
FF_CHUNK = 256
VMEM_LIMIT = 56 * 1024 * 1024

BF16 = jnp.bfloat16
F32 = jnp.float32


def _params(n_axes, vmem=VMEM_LIMIT):
    return pltpu.CompilerParams(dimension_semantics=("arbitrary",) * n_axes, vmem_limit_bytes=vmem)


def _rms_normed(x, gain):
    ms = jnp.mean(x * x, axis=-1, keepdims=True)
    return x * lax.rsqrt(ms + EPS) * gain


def _log_sigmoid(z):
    return jnp.minimum(z, 0.0) - jnp.log(1.0 + jnp.exp(-jnp.abs(z)))


def _silu(z):
    return z / (1.0 + jnp.exp(-z))


def _dot(a, b):
    return jnp.dot(a, b, preferred_element_type=F32)


def _dot_nt(a, b):
    return lax.dot_general(a, b, (((1,), (1,)), ((), ())), preferred_element_type=F32)


def _dot_tn(a, b):
    return lax.dot_general(a, b, (((0,), (0,)), ((), ())), preferred_element_type=F32)


def _bf16_terms(v):
    hi = v.astype(BF16).astype(F32)
    mid = (v - hi).astype(BF16).astype(F32)
    return hi, mid, v - hi - mid


def _dot_3pass(a, b):
    a_hi, a_lo, _ = _bf16_terms(a)
    b_hi, b_lo, _ = _bf16_terms(b)
    a_hi, a_lo, b_hi, b_lo = (t.astype(BF16) for t in (a_hi, a_lo, b_hi, b_lo))
    return _dot(a_hi, b_hi) + _dot(a_hi, b_lo) + _dot(a_lo, b_hi)


def _cumsum_rows(x):
    n = x.shape[0]
    r = lax.broadcasted_iota(jnp.int32, (n, n), 0)
    c = lax.broadcasted_iota(jnp.int32, (n, n), 1)
    tril = jnp.where(c <= r, 1.0, 0.0).astype(BF16)
    hi, mid, lo = _bf16_terms(x)
    return _dot(tril, hi.astype(BF16)) + _dot(tril, mid.astype(BF16)) + _dot(tril, lo.astype(BF16))


def _full(shape):
    return pl.BlockSpec(shape, lambda *_: (0,) * len(shape))


def _in_a_kernel(x_ref, g_ref, wq_ref, wk_ref, wv_ref, wr_ref, wmq_ref, wg1_ref, wg2_ref, bg2_ref,
                 q_ref, k_ref, v_ref, r_ref, mq_ref, la_ref):
    hn = _rms_normed(x_ref[...], g_ref[...]).astype(BF16)
    q_ref[...] = (_dot(hn, wq_ref[...]) * (GLA_DK ** -0.5)).astype(BF16)
    k_ref[...] = _dot(hn, wk_ref[...]).astype(BF16)
    v_ref[...] = _dot(hn, wv_ref[...]).astype(BF16)
    r_ref[...] = _dot(hn, wr_ref[...]).astype(BF16)
    mq_ref[...] = _dot(hn, wmq_ref[...]).astype(BF16)
    glr = _dot(hn, wg1_ref[...])
    z = _dot_3pass(glr, wg2_ref[...])
    la_ref[...] = _log_sigmoid(z + bg2_ref[...]) * (1.0 / GLA_GATE_TAU)


def _in_a(x, gain, wq, wk, wv, wr, wmq, wg1, wg2, bg2, tm):
    t = x.shape[0]
    row = lambda n: pl.BlockSpec((tm, n), lambda i: (i, 0))
    kw, vw = GLA_HEADS * GLA_DK_PAD, GLA_HEADS * GLA_DV_PAD
    return pl.pallas_call(
        _in_a_kernel,
        grid=(t // tm,),
        in_specs=[row(D_MODEL), _full((1, D_MODEL)), _full(wq.shape), _full(wk.shape), _full(wv.shape),
                  _full(wr.shape), _full(wmq.shape), _full(wg1.shape), _full(wg2.shape), _full(bg2.shape)],
        out_specs=[row(kw), row(kw), row(vw), row(vw), row(MEM_WIDTH), row(kw)],
        out_shape=[jax.ShapeDtypeStruct((t, kw), BF16), jax.ShapeDtypeStruct((t, kw), BF16),
                   jax.ShapeDtypeStruct((t, vw), BF16), jax.ShapeDtypeStruct((t, vw), BF16),
                   jax.ShapeDtypeStruct((t, MEM_WIDTH), BF16), jax.ShapeDtypeStruct((t, kw), F32)],
        compiler_params=_params(1),
        name="in_proj_gla",
    )(x, gain, wq, wk, wv, wr, wmq, wg1, wg2, bg2)


def _norm_proj_kernel(n_out, scales, x_ref, g_ref, *refs):
    w_refs, o_refs = refs[:n_out], refs[n_out:]
    hn = _rms_normed(x_ref[...], g_ref[...]).astype(BF16)
    for w_ref, o_ref, s in zip(w_refs, o_refs, scales):
        o = _dot(hn, w_ref[...])
        if s != 1.0:
            o = o * s
        o_ref[...] = o.astype(o_ref.dtype)


def _norm_proj(x, gain, weights, scales, tm, name):
    t = x.shape[0]
    row = lambda n: pl.BlockSpec((tm, n), lambda i: (i, 0))
    return pl.pallas_call(
        functools.partial(_norm_proj_kernel, len(weights), scales),
        grid=(t // tm,),
        in_specs=[row(D_MODEL), _full((1, D_MODEL))] + [_full(w.shape) for w in weights],
        out_specs=[row(w.shape[1]) for w in weights],
        out_shape=[jax.ShapeDtypeStruct((t, w.shape[1]), BF16) for w in weights],
        compiler_params=_params(1),
        name=name,
    )(x, gain, *weights)


FOX_BLOCK = LANES
FOX_WIDTH = FOX_HEADS * FOX_BLOCK
LOG2E = 1.4426950408889634
FOX_DIAG_BANDS = 4


FOX_ONE_LANE = 3 * FOX_HEADS


def _fox_bias_matrix(query_side):
    e = np.zeros((LANES, FOX_WIDTH), np.float32)
    for h in range(FOX_HEADS):
        base = h * FOX_BLOCK + FOX_HEAD_DIM
        for n in range(3):
            if query_side:
                e[n * FOX_HEADS + h, base + n] = 1.0
                e[FOX_ONE_LANE, base + 3 + n] = 1.0
            else:
                e[FOX_ONE_LANE, base + n] = 1.0
                e[n * FOX_HEADS + h, base + 3 + n] = -1.0
    return jnp.asarray(e, BF16)


def _fox_bias_lanes(c, place_ref):
    hi, mid, lo = _bf16_terms(c * LOG2E)
    lane = lax.broadcasted_iota(jnp.int32, c.shape, 1)
    packed = jnp.where(lane == FOX_ONE_LANE, 1.0, 0.0)
    for n, term in ((2, lo), (1, mid), (0, hi)):
        shifted = pltpu.roll(term, n * FOX_HEADS, axis=1) if n else term
        packed = jnp.where((lane >= n * FOX_HEADS) & (lane < (n + 1) * FOX_HEADS), shifted, packed)
    placed = _dot(packed.astype(BF16), place_ref[...])
    return [placed[:, h * FOX_BLOCK:(h + 1) * FOX_BLOCK] for h in range(FOX_HEADS)]


def _fox_blocks(dense, spare):
    lane = lax.broadcasted_iota(jnp.int32, (dense.shape[0], FOX_BLOCK), 1)
    blocks = []
    for h in range(FOX_HEADS):
        pair = dense[:, (h // 2) * FOX_BLOCK:(h // 2 + 1) * FOX_BLOCK]
        own = pair if h % 2 == 0 else pltpu.roll(pair, FOX_HEAD_DIM, axis=1)
        blocks.append(jnp.where(lane < FOX_HEAD_DIM, own, spare[h]))
    return jnp.concatenate(blocks, axis=-1)


def _kv_kernel(tiles_per_seq, n_riders, x_ref, g_ref, wk_ref, wv_ref, wf_ref, bf_ref, place_ref, *refs):
    rider_in, (k_ref, v_ref, c_ref), rider_out = refs[:n_riders], refs[n_riders:n_riders + 3], refs[n_riders + 3:-1]
    carry_ref = refs[-1]
    for src_ref, dst_ref in zip(rider_in, rider_out):
        dst_ref[...] = src_ref[...].astype(BF16)

    @pl.when(pl.program_id(0) % tiles_per_seq == 0)
    def _():
        carry_ref[...] = jnp.zeros_like(carry_ref)

    hn = _rms_normed(x_ref[...], g_ref[...]).astype(BF16)
    log_f = _log_sigmoid(_dot(hn, wf_ref[...]) + bf_ref[...])
    tm = log_f.shape[0]
    c = carry_ref[...] + _cumsum_rows(log_f)
    carry_ref[...] = c[tm - 1:tm, :]
    c_ref[...] = c
    k_ref[...] = _fox_blocks(_dot(hn, wk_ref[...]), _fox_bias_lanes(c, place_ref)).astype(BF16)
    lane = lax.broadcasted_iota(jnp.int32, (tm, FOX_BLOCK), 1)
    row_sum_lane = jnp.where(lane == FOX_HEAD_DIM, 1.0, 0.0)
    v_ref[...] = _fox_blocks(_dot(hn, wv_ref[...]), [row_sum_lane] * FOX_HEADS).astype(BF16)


BF16_SUBLANES = 16


def _kv_proj(x, gain, wk, wv, wf, bf, seq, tm, riders):
    t = x.shape[0]
    steps = t // tm
    place = _fox_bias_matrix(query_side=False)
    row = lambda n: pl.BlockSpec((tm, n), lambda i: (i, 0))
    ride = [a.shape[0] % (steps * BF16_SUBLANES) == 0 for a in riders]
    riding = [a for a, ok in zip(riders, ride) if ok]
    rider_spec = lambda a: pl.BlockSpec((a.shape[0] // steps, a.shape[1]), lambda i: (i, 0))
    outs = pl.pallas_call(
        functools.partial(_kv_kernel, seq // tm, len(riding)),
        grid=(steps,),
        in_specs=[row(D_MODEL), _full((1, D_MODEL)), _full(wk.shape), _full(wv.shape), _full(wf.shape),
                  _full(bf.shape), _full(place.shape)] + [rider_spec(a) for a in riding],
        out_specs=[row(FOX_WIDTH), row(FOX_WIDTH), row(LANES)] + [rider_spec(a) for a in riding],
        out_shape=[jax.ShapeDtypeStruct((t, FOX_WIDTH), BF16), jax.ShapeDtypeStruct((t, FOX_WIDTH), BF16),
                   jax.ShapeDtypeStruct((t, LANES), F32)]
                  + [jax.ShapeDtypeStruct(a.shape, BF16) for a in riding],
        scratch_shapes=[pltpu.VMEM((1, LANES), F32)],
        compiler_params=_params(1),
        name="kv_proj_fox",
    )(x, gain, wk, wv, wf, bf, place, *riding)
    rode = iter(outs[3:])
    return outs[0], outs[1], outs[2], [next(rode) if ok else a.astype(BF16) for a, ok in zip(riders, ride)]


def _in_b_kernel(x_ref, g_ref, c_ref, wq_ref, wmq_ref, place_ref, q_ref, mq_ref):
    hn = _rms_normed(x_ref[...], g_ref[...]).astype(BF16)
    q = _dot(hn, wq_ref[...]) * (FOX_HEAD_DIM ** -0.5 * LOG2E)
    q_ref[...] = _fox_blocks(q, _fox_bias_lanes(c_ref[...], place_ref)).astype(BF16)
    mq_ref[...] = _dot(hn, wmq_ref[...]).astype(BF16)


def _in_b(x, gain, c, wq, wmq, tm):
    t = x.shape[0]
    place = _fox_bias_matrix(query_side=True)
    row = lambda n: pl.BlockSpec((tm, n), lambda i: (i, 0))
    return pl.pallas_call(
        _in_b_kernel,
        grid=(t // tm,),
        in_specs=[row(D_MODEL), _full((1, D_MODEL)), row(LANES), _full(wq.shape), _full(wmq.shape),
                  _full(place.shape)],
        out_specs=[row(FOX_WIDTH), row(MEM_WIDTH)],
        out_shape=[jax.ShapeDtypeStruct((t, FOX_WIDTH), BF16), jax.ShapeDtypeStruct((t, MEM_WIDTH), BF16)],
        compiler_params=_params(1),
        name="in_proj_fox",
    )(x, gain, c, wq, wmq, place)


def _gla_kernel(chunks, n_seq, q_ref, k_ref, v_ref, r_ref, la_ref, on_ref, o_ref, *st_refs):
    @pl.when(pl.program_id(1) == 0)
    def _():
        for st_ref in st_refs:
            st_ref[...] = jnp.zeros_like(st_ref)

    n_sub = GLA_CHUNK // GLA_SUB
    masks = []
    for i in range(n_sub):
        n_keys = GLA_SUB * (i + 1)
        rr = lax.broadcasted_iota(jnp.int32, (GLA_SUB, n_keys), 0) + GLA_SUB * i
        cc = lax.broadcasted_iota(jnp.int32, (GLA_SUB, n_keys), 1)
        masks.append(cc <= rr)

    def chunk_body(c, carry):
        r0 = pl.multiple_of(c * GLA_CHUNK, GLA_CHUNK)
        rows = pl.ds(r0, GLA_CHUNK)
        streams = [(s, h) for s in range(n_seq) for h in range(GLA_HEADS)]
        b_all = [_cumsum_rows(la_ref[s, rows, :]) for s in range(n_seq)]
        vals, new_states, o_inter, scores = {}, {}, {}, {}
        for s, h in streams:
            ks = slice(h * GLA_DK_PAD, (h + 1) * GLA_DK_PAD)
            vs = slice(h * GLA_DV_PAD, (h + 1) * GLA_DV_PAD)
            q = q_ref[s, rows, ks].astype(F32)
            k = k_ref[s, rows, ks].astype(F32)
            v = v_ref[s, rows, vs]
            b = b_all[s][:, ks]
            b_last = b[GLA_CHUNK - 1:GLA_CHUNK, :]
            st = st_refs[s * GLA_HEADS + h][...]
            o_inter[s, h] = _dot_nt((q * jnp.exp(b)).astype(BF16), st.astype(BF16))
            kdec = (k * jnp.exp(b_last - b)).astype(BF16)
            new_states[s, h] = st * jnp.exp(b_last) + _dot_tn(v, kdec)
            head_scores = []
            for i in range(n_sub):
                lo, hi = GLA_SUB * i, GLA_SUB * (i + 1)
                b_i = b[lo:hi, :]
                qd = q[lo:hi, :] * jnp.exp(b_i - b[lo - 1:lo, :]) if i else q[lo:hi, :] * jnp.exp(b_i)
                kd = k[:hi, :] * jnp.exp(b[lo - 1:lo, :] - b[:hi, :]) if i else k[:hi, :] * jnp.exp(-b[:hi, :])
                head_scores.append(_dot_nt(qd.astype(BF16), kd.astype(BF16)))
            scores[s, h] = head_scores
            vals[s, h] = v
        outs = {}
        for s, h in streams:
            vs = slice(h * GLA_DV_PAD, (h + 1) * GLA_DV_PAD)
            o_parts = [_dot(jnp.where(masks[i], scores[s, h][i], 0.0).astype(BF16),
                            vals[s, h][:GLA_SUB * (i + 1), :]) for i in range(n_sub)]
            o = o_inter[s, h] + jnp.concatenate(o_parts, axis=0)
            ms = jnp.sum(o * o, axis=-1, keepdims=True) * (1.0 / GLA_DV)
            y = o * lax.rsqrt(ms + EPS) * on_ref[...] * _silu(r_ref[s, rows, vs].astype(F32))
            outs[s, h] = y.astype(BF16)
        for s, h in streams:
            st_refs[s * GLA_HEADS + h][...] = new_states[s, h]
            o_ref[s, rows, h * GLA_DV_PAD:(h + 1) * GLA_DV_PAD] = outs[s, h]
        return carry

    lax.fori_loop(0, chunks, chunk_body, 0, unroll=4)


def _gla(q, k, v, r, la, onorm, bsz, seq, tm):
    t = q.shape[0]
    n_seq = 2 if bsz % 2 == 0 else 1
    kw, vw = GLA_HEADS * GLA_DK_PAD, GLA_HEADS * GLA_DV_PAD
    rows = lambda n: pl.BlockSpec((n_seq, tm, n), lambda b, i: (b, i, 0))
    per_seq = lambda a: a.reshape(bsz, seq, a.shape[-1])
    out = pl.pallas_call(
        functools.partial(_gla_kernel, tm // GLA_CHUNK, n_seq),
        grid=(bsz // n_seq, seq // tm),
        in_specs=[rows(kw), rows(kw), rows(vw), rows(vw), rows(kw),
                  pl.BlockSpec((1, GLA_DV_PAD), lambda b, i: (0, 0))],
        out_specs=rows(vw),
        out_shape=jax.ShapeDtypeStruct((bsz, seq, vw), BF16),
        scratch_shapes=[pltpu.VMEM((GLA_DV_PAD, GLA_DK_PAD), F32)] * (GLA_HEADS * n_seq),
        compiler_params=_params(2),
        name="gla_scan",
    )(per_seq(q), per_seq(k), per_seq(v), per_seq(r), per_seq(la), onorm)
    return out.reshape(t, vw)


def _fox_kernel(tile, q_ref, k_ref, v_ref, o_ref):
    i = pl.program_id(2)
    q = q_ref[...]
    qs = (q[:, :FOX_BLOCK], q[:, FOX_BLOCK:])
    blks = (slice(0, FOX_BLOCK), slice(FOX_BLOCK, 2 * FOX_BLOCK))

    def absorb(m, acc, s, v):
        m_new = jnp.maximum(m, jnp.max(s, axis=-1, keepdims=True))
        p = jnp.exp2(s - m_new).astype(BF16)
        return m_new, jnp.exp2(m - m_new) * acc + _dot(p, v)

    def step(j, state):
        keys = pl.ds(pl.multiple_of(j * tile, tile), tile)
        return tuple(absorb(*state[hh], _dot_nt(qs[hh], k_ref[keys, blks[hh]]), v_ref[keys, blks[hh]])
                     for hh in range(2))

    init = (jnp.full((tile, 1), -jnp.inf, F32), jnp.zeros((tile, FOX_BLOCK), F32))
    state = lax.fori_loop(0, i, step, (init, init))

    band = tile // FOX_DIAG_BANDS
    start = pl.multiple_of(i * tile, tile)
    bands = [(hh, r) for r in range(FOX_DIAG_BANDS) for hh in range(2)]
    band_rows = lambda r: slice(r * band, (r + 1) * band)
    band_keys = lambda r: pl.ds(start, (r + 1) * band)
    scores = {(hh, r): _dot_nt(qs[hh][band_rows(r)], k_ref[band_keys(r), blks[hh]]) for hh, r in bands}
    causal = (lax.broadcasted_iota(jnp.int32, (band, band), 1) <= lax.broadcasted_iota(jnp.int32, (band, band), 0))
    lane = lax.broadcasted_iota(jnp.int32, (band, FOX_BLOCK), 1)
    for r in range(FOX_DIAG_BANDS):
        outs = []
        for hh in range(2):
            m, acc = state[hh]
            s = scores[hh, r]
            tail = jnp.where(causal, s[:, r * band:], -jnp.inf)
            s = jnp.concatenate([s[:, :r * band], tail], axis=1) if r else tail
            _, acc = absorb(m[band_rows(r)], acc[band_rows(r)], s, v_ref[band_keys(r), blks[hh]])
            outs.append(acc / jnp.sum(jnp.where(lane == FOX_HEAD_DIM, acc, 0.0), axis=-1, keepdims=True))
        o_ref[band_rows(r), :] = jnp.where(lane < FOX_HEAD_DIM, outs[0],
                                           pltpu.roll(outs[1], FOX_HEAD_DIM, axis=1)).astype(BF16)


def _fox(q, k, v, bsz, seq, tile):
    t = q.shape[0]
    nq = seq // tile
    pairs = FOX_HEADS // 2
    return pl.pallas_call(
        functools.partial(_fox_kernel, tile),
        grid=(bsz, pairs, nq),
        in_specs=[pl.BlockSpec((tile, 2 * FOX_BLOCK), lambda b, p, i: (b * nq + i, p)),
                  pl.BlockSpec((seq, 2 * FOX_BLOCK), lambda b, p, i: (b, p)),
                  pl.BlockSpec((seq, 2 * FOX_BLOCK), lambda b, p, i: (b, p))],
        out_specs=pl.BlockSpec((tile, LANES), lambda b, p, i: (b * nq + i, p)),
        out_shape=jax.ShapeDtypeStruct((t, MAIN_WIDTH), BF16),
        compiler_params=_params(3),
        name="fox_attention",
    )(q, k, v)


def _out_kernel(x_ref, main_ref, mq_ref, mk_ref, mv_ref, wo_main_ref, wo_mem_ref, o_ref):
    mq, mk, mv = mq_ref[...], mk_ref[...], mv_ref[...]
    head = lambda a, h: a[:, h * MEM_HEAD_DIM:(h + 1) * MEM_HEAD_DIM]
    scores = [_dot_nt(head(mq, h), head(mk, h)) * (MEM_HEAD_DIM ** -0.5) for h in range(MEM_HEADS)]
    base = x_ref[...] + _dot(main_ref[...], wo_main_ref[...])
    heads = []
    for h in range(MEM_HEADS):
        s = scores[h]
        p = jnp.exp(s - jnp.max(s, axis=-1, keepdims=True))
        l = jnp.sum(p, axis=-1, keepdims=True)
        heads.append(_dot(p.astype(BF16), head(mv, h)) / l)
    mem_o = jnp.concatenate(heads, axis=-1).astype(BF16)
    o_ref[...] = base + _dot(mem_o, wo_mem_ref[...])


def _out_proj(x, main, mq, mem_k, mem_v, wo_main, wo_mem, seq, tm):
    t = x.shape[0]
    tiles_per_seq = seq // tm
    n_mem = mem_k.shape[1]
    row = lambda n: pl.BlockSpec((tm, n), lambda i: (i, 0))
    mem = pl.BlockSpec((None, n_mem, MEM_WIDTH), lambda i: (i // tiles_per_seq, 0, 0))
    return pl.pallas_call(
        _out_kernel,
        grid=(t // tm,),
        in_specs=[row(D_MODEL), row(main.shape[1]), row(MEM_WIDTH), mem, mem, _full(wo_main.shape),
                  _full(wo_mem.shape)],
        out_specs=row(D_MODEL),
        out_shape=jax.ShapeDtypeStruct((t, D_MODEL), F32),
        compiler_params=_params(1),
        name="out_proj_mem_attn",
    )(x, main, mq, mem_k, mem_v, wo_main, wo_mem)


def _swiglu_acc(hn, wg_ref, wu_ref, wd_ref, acc_ref, row_scale=None):
    for j in range(D_FF // FF_CHUNK):
        cs = slice(j * FF_CHUNK, (j + 1) * FF_CHUNK)
        a = _silu(_dot(hn, wg_ref[:, cs])) * _dot(hn, wu_ref[:, cs])
        if row_scale is not None:
            a = a * row_scale
        acc_ref[...] += _dot(a.astype(BF16), wd_ref[cs, :])


def _ffn_kernel(x_ref, g_ref, wg_ref, wu_ref, wd_ref, o_ref):
    x = x_ref[...]
    o_ref[...] = x
    _swiglu_acc(_rms_normed(x, g_ref[...]).astype(BF16), wg_ref, wu_ref, wd_ref, o_ref)


def _ffn(x, gain, wg, wu, wd, tm):
    t = x.shape[0]
    row = pl.BlockSpec((tm, D_MODEL), lambda i: (i, 0))
    return pl.pallas_call(
        _ffn_kernel,
        grid=(t // tm,),
        in_specs=[row, _full((1, D_MODEL)), _full(wg.shape), _full(wu.shape), _full(wd.shape)],
        out_specs=row,
        out_shape=jax.ShapeDtypeStruct((t, D_MODEL), F32),
        compiler_params=_params(1),
        name="dense_swiglu",
    )(x, gain, wg, wu, wd)


ROW_TILE = D_MODEL // LANES
ROUTE_E1, ROUTE_E2, ROUTE_W1, ROUTE_W2, ROUTE_RANK1, ROUTE_RANK2 = range(6)


def _rows_to_tiles(dst_ref, rows):
    n = rows.shape[0]
    for j in range(ROW_TILE):
        dst_ref[pl.ds(j, n, stride=ROW_TILE), :] = rows[:, j * LANES:(j + 1) * LANES]


def _tiles_to_rows(src_ref, n):
    return jnp.concatenate([src_ref[pl.ds(j, n, stride=ROW_TILE), :] for j in range(ROW_TILE)], axis=1)


def _row_tile(ref, row):
    return ref.at[pl.ds(pl.multiple_of(row * ROW_TILE, ROW_TILE), ROW_TILE)]


def _router_kernel(x_ref, g_ref, wr_ref, route_ref, fields_ref, counts_ref, carry_ref):
    @pl.when(pl.program_id(0) == 0)
    def _():
        carry_ref[...] = jnp.zeros_like(carry_ref)

    hn = _rms_normed(x_ref[...], g_ref[...])
    logits = _dot_3pass(hn, wr_ref[...]).T[:N_EXPERTS, :]
    tm = logits.shape[1]
    expert = lax.broadcasted_iota(jnp.int32, logits.shape, 0)
    m1 = jnp.max(logits, axis=0, keepdims=True)
    i1 = jnp.min(jnp.where(logits == m1, expert, N_EXPERTS), axis=0, keepdims=True)
    rest = jnp.where(expert == i1, -jnp.inf, logits)
    m2 = jnp.max(rest, axis=0, keepdims=True)
    i2 = jnp.min(jnp.where(rest == m2, expert, N_EXPERTS), axis=0, keepdims=True)
    e2 = jnp.exp(m2 - m1)
    w1 = 1.0 / (1.0 + e2)
    chosen = jnp.where((expert == i1) | (expert == i2), 1.0, 0.0)
    token = lax.broadcasted_iota(jnp.int32, logits.shape, 1)
    run = chosen
    shift = 1
    while shift < tm:
        run = run + jnp.where(token >= shift, pltpu.roll(run, shift, axis=1), 0.0)
        shift *= 2
    base = carry_ref[...][:, :1] + (run - chosen)
    rank1 = jnp.sum(jnp.where(expert == i1, base, 0.0), axis=0, keepdims=True)
    rank2 = jnp.sum(jnp.where(expert == i2, base, 0.0), axis=0, keepdims=True)
    carry_ref[...] += jnp.sum(chosen, axis=1, keepdims=True)
    counts_ref[...] = carry_ref[...]
    fields = jnp.zeros_like(logits)
    for slot, val in ((ROUTE_E1, i1.astype(F32)), (ROUTE_E2, i2.astype(F32)), (ROUTE_W1, w1), (ROUTE_W2, e2 * w1),
                      (ROUTE_RANK1, rank1), (ROUTE_RANK2, rank2)):
        fields = jnp.where(expert == slot, val, fields)
    fields_ref[...] = fields
    route_ref[...] = jnp.concatenate([fields, jnp.zeros((LANES - N_EXPERTS, tm), F32)], axis=0).T


def _router(x, gain, wr, tm):
    t = x.shape[0]
    row = lambda n: pl.BlockSpec((tm, n), lambda i: (i, 0))
    n_fields = N_EXPERTS
    return pl.pallas_call(
        _router_kernel,
        grid=(t // tm,),
        in_specs=[row(D_MODEL), _full((1, D_MODEL)), _full(wr.shape)],
        out_specs=[row(LANES), pl.BlockSpec((None, n_fields, tm), lambda i: (i, 0, 0)), _full((N_EXPERTS, LANES))],
        out_shape=[jax.ShapeDtypeStruct((t, LANES), F32), jax.ShapeDtypeStruct((t // tm, n_fields, tm), F32),
                   jax.ShapeDtypeStruct((N_EXPERTS, LANES), F32)],
        scratch_shapes=[pltpu.VMEM((N_EXPERTS, LANES), F32)],
        compiler_params=_params(1),
        name="moe_router",
    )(x, gain, wr)


def _dispatch_kernel(tm, cnt_ref, pad_ref, off_ref, pos1_ref, pos2_ref, x_ref, g_ref, xs_ref, stage_ref, zero_ref,
                     sem, pad_sem):
    @pl.when(pl.program_id(0) == 0)
    def _():
        zero_ref[...] = jnp.zeros_like(zero_ref)
        for e in range(N_EXPERTS + 1):
            lo, hi = off_ref[e] + cnt_ref[e], off_ref[e] + pad_ref[e]

            def fill(r, c):
                pltpu.make_async_copy(zero_ref, _row_tile(xs_ref, r), pad_sem).start()
                return c

            def drain(r, c):
                pltpu.make_async_copy(zero_ref, _row_tile(xs_ref, r), pad_sem).wait()
                return c

            lax.fori_loop(lo, hi, fill, 0)
            lax.fori_loop(lo, hi, drain, 0)

    _rows_to_tiles(stage_ref, _rms_normed(x_ref[...], g_ref[...]))

    def issue(r, c):
        src = _row_tile(stage_ref, r)
        pltpu.make_async_copy(src, _row_tile(xs_ref, pos1_ref[0, r]), sem).start(priority=0)
        pltpu.make_async_copy(src, _row_tile(xs_ref, pos2_ref[0, r]), sem).start(priority=1)
        return c

    lax.fori_loop(0, tm, issue, 0, unroll=8)
    for _ in range(2):
        pltpu.make_async_copy(stage_ref, xs_ref.at[pl.ds(0, tm * ROW_TILE)], sem).wait()


def _dispatch(x, gain, pos1, pos2, counts, padded, offsets, n_rows, tm):
    t = x.shape[0]
    nt = t // tm
    smem_row = pl.BlockSpec((None, 1, tm), lambda i, *_: (i, 0, 0), memory_space=pltpu.SMEM)
    return pl.pallas_call(
        functools.partial(_dispatch_kernel, tm),
        grid_spec=pltpu.PrefetchScalarGridSpec(
            num_scalar_prefetch=3,
            grid=(nt,),
            in_specs=[smem_row, smem_row, pl.BlockSpec((tm, D_MODEL), lambda i, *_: (i, 0)),
                      pl.BlockSpec((1, D_MODEL), lambda i, *_: (0, 0))],
            out_specs=pl.BlockSpec(memory_space=pl.ANY),
            scratch_shapes=[pltpu.VMEM((tm * ROW_TILE, LANES), F32), pltpu.VMEM((ROW_TILE, LANES), F32),
                            pltpu.SemaphoreType.DMA, pltpu.SemaphoreType.DMA],
        ),
        out_shape=jax.ShapeDtypeStruct((n_rows * ROW_TILE, LANES), F32),
        compiler_params=_params(1),
        name="moe_dispatch",
    )(counts, padded, offsets, pos1.reshape(nt, 1, tm), pos2.reshape(nt, 1, tm), x, gain)


def _experts_kernel(tm, tile_expert_ref, n_used_ref, xs_ref, wg_ref, wu_ref, wd_ref, o_ref, acc_ref):
    acc_ref[...] = jnp.zeros_like(acc_ref)

    @pl.when(pl.program_id(0) < n_used_ref[0])
    def _():
        _swiglu_acc(_tiles_to_rows(xs_ref, tm).astype(BF16), wg_ref, wu_ref, wd_ref, acc_ref)

    _rows_to_tiles(o_ref, acc_ref[...])


def _experts(xs, tile_expert, n_used, wg, wu, wd, tm):
    n_tiles = xs.shape[0] // (tm * ROW_TILE)
    wspec = lambda shape: pl.BlockSpec(
        (None,) + shape, lambda i, te, nu: (te[jnp.minimum(i, nu[0] - 1)], 0, 0))
    rows = pl.BlockSpec((tm * ROW_TILE, LANES), lambda i, te, nu: (i, 0))
    return pl.pallas_call(
        functools.partial(_experts_kernel, tm),
        grid_spec=pltpu.PrefetchScalarGridSpec(
            num_scalar_prefetch=2,
            grid=(n_tiles,),
            in_specs=[rows, wspec((D_MODEL, D_FF)), wspec((D_MODEL, D_FF)), wspec((D_FF, D_MODEL))],
            out_specs=rows,
            scratch_shapes=[pltpu.VMEM((tm, D_MODEL), F32)],
        ),
        out_shape=jax.ShapeDtypeStruct(xs.shape, F32),
        compiler_params=_params(1),
        name="moe_experts",
    )(tile_expert, n_used, xs, wg, wu, wd)


def _combine_kernel(tm, n_tiles, pos1_ref, pos2_ref, next1_ref, next2_ref, x_ref, route_ref, gf_ref, y_ref,
                    o_ref, buf_ref, sems):
    i = pl.program_id(0)
    slot = i % 2

    def gather(p1_ref, p2_ref, slot):
        def issue(r, c):
            pltpu.make_async_copy(_row_tile(y_ref, p1_ref[0, r]), _row_tile(buf_ref.at[slot, 0], r),
                                  sems.at[slot]).start(priority=0)
            pltpu.make_async_copy(_row_tile(y_ref, p2_ref[0, r]), _row_tile(buf_ref.at[slot, 1], r),
                                  sems.at[slot]).start(priority=1)
            return c

        lax.fori_loop(0, tm, issue, 0, unroll=8)

    @pl.when(i == 0)
    def _():
        gather(pos1_ref, pos2_ref, 0)

    @pl.when(i + 1 < n_tiles)
    def _():
        gather(next1_ref, next2_ref, 1 - slot)

    route = route_ref[...]
    lane = lax.broadcasted_iota(jnp.int32, route.shape, 1)
    w1 = jnp.sum(jnp.where(lane == ROUTE_W1, route, 0.0), axis=-1, keepdims=True)
    w2 = jnp.sum(jnp.where(lane == ROUTE_W2, route, 0.0), axis=-1, keepdims=True)
    for k in range(2):
        pltpu.make_async_copy(y_ref.at[pl.ds(0, tm * ROW_TILE)], buf_ref.at[slot, k], sems.at[slot]).wait()
    y1, y2 = (_tiles_to_rows(buf_ref.at[slot, k], tm) for k in range(2))
    o_ref[...] = _rms_normed(x_ref[...] + w1 * y1 + w2 * y2, gf_ref[...])


def _combine(x, route, pos1, pos2, y, gain_final, tm):
    t = x.shape[0]
    nt = t // tm
    smem_row = lambda ahead: pl.BlockSpec((None, 1, tm), lambda i: (jnp.minimum(i + ahead, nt - 1), 0, 0),
                                          memory_space=pltpu.SMEM)
    row = lambda n: pl.BlockSpec((tm, n), lambda i: (i, 0))
    pos1, pos2 = pos1.reshape(nt, 1, tm), pos2.reshape(nt, 1, tm)
    return pl.pallas_call(
        functools.partial(_combine_kernel, tm, nt),
        grid=(nt,),
        in_specs=[smem_row(0), smem_row(0), smem_row(1), smem_row(1), row(D_MODEL), row(LANES),
                  _full((1, D_MODEL)), pl.BlockSpec(memory_space=pl.ANY)],
        out_specs=row(D_MODEL),
        out_shape=jax.ShapeDtypeStruct((t, D_MODEL), F32),
        scratch_shapes=[pltpu.VMEM((2, 2, tm * ROW_TILE, LANES), F32), pltpu.SemaphoreType.DMA((2,))],
        compiler_params=_params(1),
        name="moe_combine",
    )(pos1, pos2, pos1, pos2, x, route, gain_final, y)


def _moe(x, gain, gain_final, wr, wg, wu, wd, tm):
    t = x.shape[0]
    route, fields, counts = _router(x, gain, wr, tm)
    col = lambda c: fields[:, c, :].reshape(t).astype(jnp.int32)
    counts = counts[:, 0].astype(jnp.int32)
    padded = (counts + tm - 1) // tm * tm
    ends = jnp.cumsum(padded)
    offsets = ends - padded
    pos1 = offsets[col(ROUTE_E1)] + col(ROUTE_RANK1)
    pos2 = offsets[col(ROUTE_E2)] + col(ROUTE_RANK2)
    n_rows = 2 * t + N_EXPERTS * tm
    tile_start = jnp.arange(n_rows // tm, dtype=jnp.int32) * tm
    tile_expert = jnp.minimum(jnp.sum(tile_start[:, None] >= ends[None, :], axis=1), N_EXPERTS - 1).astype(jnp.int32)
    n_used = (ends[-1:] // tm).astype(jnp.int32)
    tail = lambda a, v: jnp.concatenate([a, v.astype(jnp.int32)])
    xs = _dispatch(x, gain, pos1, pos2, tail(counts, jnp.zeros((1,))), tail(padded, n_rows - ends[-1:]),
                   tail(offsets, ends[-1:]), n_rows, tm)
    y = _experts(xs, tile_expert, n_used, wg, wu, wd, tm)
    return _combine(x, route, pos1, pos2, y, gain_final, tm // 2)


def _pad_heads(w, heads, width, padded):
    lead = w.shape[:-1]
    w = w.reshape(lead + (heads, width))
    w = jnp.pad(w, [(0, 0)] * len(lead) + [(0, 0), (0, padded - width)])
    return w.reshape(lead + (heads * padded,))


def _pad_cols(w, n):
    return jnp.pad(w, [(0, 0)] * (w.ndim - 1) + [(0, n - w.shape[-1])])


def kernel(x, mem, norm_mix, norm_mem, norm_ffn, norm_kv, norm_final, w_in_a, w_gla_gate2, b_gla_gate2,
           gla_onorm, w_in_b, w_kv, b_forget, w_mem_kv, w_out, w_ff_gate, w_ff_up, w_ff_down, w_router,
           w_moe_gate, w_moe_up, w_moe_down):
    bsz, seq, _ = x.shape
    n_mem = mem.shape[1]
    t = bsz * seq
    tm = 512
    xt = x.reshape(t, D_MODEL)
    memt = mem.reshape(bsz * n_mem, D_MODEL)
    gain = lambda g: g.reshape(1, D_MODEL)
    kw = GLA_HEADS * GLA_DK

    def mem_kv(layer):
        wk, wv = w_mem_kv[layer, :, :MEM_WIDTH], w_mem_kv[layer, :, MEM_WIDTH:]
        mk, mv = _norm_proj(memt, gain(norm_mem[layer]), [wk.astype(BF16), wv.astype(BF16)], (1.0, 1.0),
                            n_mem, "mem_kv_proj")
        return mk.reshape(bsz, n_mem, MEM_WIDTH), mv.reshape(bsz, n_mem, MEM_WIDTH)

    wa = w_in_a[0]
    wq = _pad_heads(wa[:, :kw], GLA_HEADS, GLA_DK, GLA_DK_PAD).astype(BF16)
    wk = _pad_heads(wa[:, kw:2 * kw], GLA_HEADS, GLA_DK, GLA_DK_PAD).astype(BF16)
    off = 2 * kw
    wv = _pad_heads(wa[:, off:off + MAIN_WIDTH], GLA_HEADS, GLA_DV, GLA_DV_PAD).astype(BF16)
    off += MAIN_WIDTH
    wr = _pad_heads(wa[:, off:off + MAIN_WIDTH], GLA_HEADS, GLA_DV, GLA_DV_PAD).astype(BF16)
    off += MAIN_WIDTH
    wg1 = _pad_cols(wa[:, off:off + GLA_GATE_RANK], LANES).astype(BF16)
    off += GLA_GATE_RANK
    wmq = wa[:, off:off + MEM_WIDTH].astype(BF16)
    wg2 = jnp.pad(_pad_heads(w_gla_gate2[0], GLA_HEADS, GLA_DK, GLA_DK_PAD), ((0, LANES - GLA_GATE_RANK), (0, 0)))
    bg2 = _pad_heads(b_gla_gate2[0], GLA_HEADS, GLA_DK, GLA_DK_PAD).reshape(1, -1)
    onorm = _pad_cols(gla_onorm[0], GLA_DV_PAD).reshape(1, GLA_DV_PAD)

    q, k, v, r, mq, la = _in_a(xt, gain(norm_mix[0]), wq, wk, wv, wr, wmq, wg1, wg2, bg2, 2 * tm)
    main = _gla(q, k, v, r, la, onorm, bsz, seq, tm)
    mk, mv = mem_kv(0)
    wo = w_out[0]
    wo_main = jnp.pad(wo[:MAIN_WIDTH].reshape(GLA_HEADS, GLA_DV, D_MODEL),
                      ((0, 0), (0, GLA_DV_PAD - GLA_DV), (0, 0))).reshape(GLA_HEADS * GLA_DV_PAD, D_MODEL)
    xt = _out_proj(xt, main, mq, mk, mv, wo_main.astype(BF16), wo[MAIN_WIDTH:].astype(BF16), seq, 2 * tm)
    xt = _ffn(xt, gain(norm_ffn[0]), w_ff_gate[0].astype(BF16), w_ff_up[0].astype(BF16),
              w_ff_down[0].astype(BF16), tm)

    wf = _pad_cols(w_kv[:, 2 * MAIN_WIDTH:], LANES).astype(BF16)
    bf = _pad_cols(b_forget, LANES).reshape(1, LANES)
    expert_weights = [w_moe_gate[0].reshape(N_EXPERTS * D_MODEL, D_FF), w_moe_up[0].reshape(N_EXPERTS * D_MODEL, D_FF),
                      w_moe_down[0].reshape(N_EXPERTS * D_FF, D_MODEL)]
    k_sh, v_sh, c, expert_weights = _kv_proj(xt, gain(norm_kv), w_kv[:, :MAIN_WIDTH].astype(BF16),
                                             w_kv[:, MAIN_WIDTH:2 * MAIN_WIDTH].astype(BF16), wf, bf, seq, tm,
                                             expert_weights)
    wg, wu, wd = (w.reshape(N_EXPERTS, -1, w.shape[-1]) for w in expert_weights)

    wb = w_in_b[0]
    qf, mq = _in_b(xt, gain(norm_mix[1]), c, wb[:, :MAIN_WIDTH].astype(BF16), wb[:, MAIN_WIDTH:].astype(BF16),
                   2 * tm)
    main = _fox(qf, k_sh, v_sh, bsz, seq, 1024)
    mk, mv = mem_kv(1)
    wo = w_out[1]
    xt = _out_proj(xt, main, mq, mk, mv, wo[:MAIN_WIDTH].astype(BF16), wo[MAIN_WIDTH:].astype(BF16), seq, 2 * tm)
    out = _moe(xt, gain(norm_ffn[1]), gain(norm_final), _pad_cols(w_router[0], LANES), wg, wu, wd, tm)
    return out.reshape(bsz, seq, D_MODEL)
```

```python
import functools

import jax
import jax.numpy as jnp
import numpy as np
from jax import lax
from jax.experimental import pallas as pl
from jax.experimental.pallas import tpu as pltpu

D_MODEL = 1024
EPS = 1e-6
GLA_CHUNK = 64
GLA_SUB = 16
MEM_HEADS = 4
MEM_HEAD_DIM = 64
MEM_WIDTH = MEM_HEADS * MEM_HEAD_DIM
MAIN_WIDTH = D_MODEL - MEM_WIDTH
GLA_HEADS = 4
GLA_DK = MAIN_WIDTH // 2 // GLA_HEADS
GLA_DV = MAIN_WIDTH // GLA_HEADS
GLA_DK_PAD = 128
GLA_DV_PAD = 256
GLA_GATE_RANK = 16
GLA_GATE_TAU = 16.0
FOX_HEADS = 12
FOX_HEAD_DIM = 64
D_FF = 2816
N_EXPERTS = 8
LANES = 128
FF_CHUNK = 256
VMEM_LIMIT = 56 * 1024 * 1024

BF16 = jnp.bfloat16
F32 = jnp.float32


def _params(n_axes, vmem=VMEM_LIMIT):
    return pltpu.CompilerParams(dimension_semantics=("arbitrary",) * n_axes, vmem_limit_bytes=vmem)


def _rms_normed(x, gain):
    ms = jnp.mean(x * x, axis=-1, keepdims=True)
    return x * lax.rsqrt(ms + EPS) * gain


def _log_sigmoid(z):
    return jnp.minimum(z, 0.0) - jnp.log(1.0 + jnp.exp(-jnp.abs(z)))


def _silu(z):
    return z / (1.0 + jnp.exp(-z))


def _dot(a, b):
    return jnp.dot(a, b, preferred_element_type=F32)


def _dot_nt(a, b):
    return lax.dot_general(a, b, (((1,), (1,)), ((), ())), preferred_element_type=F32)


def _dot_tn(a, b):
    return lax.dot_general(a, b, (((0,), (0,)), ((), ())), preferred_element_type=F32)


def _bf16_terms(v):
    hi = v.astype(BF16).astype(F32)
    mid = (v - hi).astype(BF16).astype(F32)
    return hi, mid, v - hi - mid


def _dot_3pass(a, b):
    a_hi, a_lo, _ = _bf16_terms(a)
    b_hi, b_lo, _ = _bf16_terms(b)
    a_hi, a_lo, b_hi, b_lo = (t.astype(BF16) for t in (a_hi, a_lo, b_hi, b_lo))
    return _dot(a_hi, b_hi) + _dot(a_hi, b_lo) + _dot(a_lo, b_hi)


def _cumsum_rows(x):
    n = x.shape[0]
    r = lax.broadcasted_iota(jnp.int32, (n, n), 0)
    c = lax.broadcasted_iota(jnp.int32, (n, n), 1)
    tril = jnp.where(c <= r, 1.0, 0.0).astype(BF16)
    hi, mid, lo = _bf16_terms(x)
    return _dot(tril, hi.astype(BF16)) + _dot(tril, mid.astype(BF16)) + _dot(tril, lo.astype(BF16))


def _full(shape):
    return pl.BlockSpec(shape, lambda *_: (0,) * len(shape))


BF16_SUBLANES = 16


def _riding(kernel_fn, n_in, n_out, n_riders):
    def body(*refs):
        ins, rest = refs[:n_in], refs[n_in:]
        rider_in, rest = rest[:n_riders], rest[n_riders:]
        outs, rest = rest[:n_out], rest[n_out:]
        rider_out, scratch = rest[:n_riders], rest[n_riders:]
        for src_ref, dst_ref in zip(rider_in, rider_out):
            dst_ref[...] = src_ref[...].astype(BF16)
        kernel_fn(*ins, *outs, *scratch)
    return body


def _rider_blocks(riders, steps):
    for a in riders:
        assert a.shape[0] % (steps * BF16_SUBLANES) == 0, (a.shape, steps)
    specs = [pl.BlockSpec((a.shape[0] // steps, a.shape[1]), lambda i: (i, 0)) for a in riders]
    return specs, [jax.ShapeDtypeStruct(a.shape, BF16) for a in riders]


def _in_a_kernel(x_ref, g_ref, wq_ref, wk_ref, wv_ref, wr_ref, wmq_ref, wg1_ref, wg2_ref, bg2_ref,
                 q_ref, k_ref, v_ref, r_ref, mq_ref, la_ref):
    hn = _rms_normed(x_ref[...], g_ref[...]).astype(BF16)
    q_ref[...] = (_dot(hn, wq_ref[...]) * (GLA_DK ** -0.5)).astype(BF16)
    k_ref[...] = _dot(hn, wk_ref[...]).astype(BF16)
    v_ref[...] = _dot(hn, wv_ref[...]).astype(BF16)
    r_ref[...] = _dot(hn, wr_ref[...]).astype(BF16)
    mq_ref[...] = _dot(hn, wmq_ref[...]).astype(BF16)
    glr = _dot(hn, wg1_ref[...])
    z = _dot_3pass(glr, wg2_ref[...])
    la_ref[...] = _log_sigmoid(z + bg2_ref[...]) * (1.0 / GLA_GATE_TAU)


def _in_a(x, gain, wq, wk, wv, wr, wmq, wg1, wg2, bg2, tm, riders):
    t = x.shape[0]
    steps = t // tm
    row = lambda n: pl.BlockSpec((tm, n), lambda i: (i, 0))
    kw, vw = GLA_HEADS * GLA_DK_PAD, GLA_HEADS * GLA_DV_PAD
    rider_specs, rider_shapes = _rider_blocks(riders, steps)
    out = pl.pallas_call(
        _riding(_in_a_kernel, 10, 6, len(riders)),
        grid=(steps,),
        in_specs=[row(D_MODEL), _full((1, D_MODEL)), _full(wq.shape), _full(wk.shape), _full(wv.shape),
                  _full(wr.shape), _full(wmq.shape), _full(wg1.shape), _full(wg2.shape), _full(bg2.shape)]
                 + rider_specs,
        out_specs=[row(kw), row(kw), row(vw), row(vw), row(MEM_WIDTH), row(kw)] + rider_specs,
        out_shape=[jax.ShapeDtypeStruct((t, kw), BF16), jax.ShapeDtypeStruct((t, kw), BF16),
                   jax.ShapeDtypeStruct((t, vw), BF16), jax.ShapeDtypeStruct((t, vw), BF16),
                   jax.ShapeDtypeStruct((t, MEM_WIDTH), BF16), jax.ShapeDtypeStruct((t, kw), F32)] + rider_shapes,
        compiler_params=_params(1),
        name="in_proj_gla",
    )(x, gain, wq, wk, wv, wr, wmq, wg1, wg2, bg2, *riders)
    return out[:6], out[6:]


def _norm_proj_kernel(n_out, scales, x_ref, g_ref, *refs):
    w_refs, o_refs = refs[:n_out], refs[n_out:]
    hn = _rms_normed(x_ref[...], g_ref[...]).astype(BF16)
    for w_ref, o_ref, s in zip(w_refs, o_refs, scales):
        o = _dot(hn, w_ref[...])
        if s != 1.0:
            o = o * s
        o_ref[...] = o.astype(o_ref.dtype)


def _norm_proj(x, gain, weights, scales, tm, name):
    t = x.shape[0]
    row = lambda n: pl.BlockSpec((tm, n), lambda i: (i, 0))
    return pl.pallas_call(
        functools.partial(_norm_proj_kernel, len(weights), scales),
        grid=(t // tm,),
        in_specs=[row(D_MODEL), _full((1, D_MODEL))] + [_full(w.shape) for w in weights],
        out_specs=[row(w.shape[1]) for w in weights],
        out_shape=[jax.ShapeDtypeStruct((t, w.shape[1]), BF16) for w in weights],
        compiler_params=_params(1),
        name=name,
    )(x, gain, *weights)


FOX_BLOCK = LANES
FOX_WIDTH = FOX_HEADS * FOX_BLOCK
LOG2E = 1.4426950408889634
FOX_DIAG_BANDS = 4


FOX_ONE_LANE = 3 * FOX_HEADS


def _fox_bias_matrix(query_side):
    e = np.zeros((LANES, FOX_WIDTH), np.float32)
    for h in range(FOX_HEADS):
        base = h * FOX_BLOCK + FOX_HEAD_DIM
        for n in range(3):
            if query_side:
                e[n * FOX_HEADS + h, base + n] = 1.0
                e[FOX_ONE_LANE, base + 3 + n] = 1.0
            else:
                e[FOX_ONE_LANE, base + n] = 1.0
                e[n * FOX_HEADS + h, base + 3 + n] = -1.0
    return jnp.asarray(e, BF16)


def _fox_bias_lanes(c, place_ref):
    hi, mid, lo = _bf16_terms(c * LOG2E)
    lane = lax.broadcasted_iota(jnp.int32, c.shape, 1)
    packed = jnp.where(lane == FOX_ONE_LANE, 1.0, 0.0)
    for n, term in ((2, lo), (1, mid), (0, hi)):
        shifted = pltpu.roll(term, n * FOX_HEADS, axis=1) if n else term
        packed = jnp.where((lane >= n * FOX_HEADS) & (lane < (n + 1) * FOX_HEADS), shifted, packed)
    placed = _dot(packed.astype(BF16), place_ref[...])
    return [placed[:, h * FOX_BLOCK:(h + 1) * FOX_BLOCK] for h in range(FOX_HEADS)]


def _fox_blocks(dense, spare):
    lane = lax.broadcasted_iota(jnp.int32, (dense.shape[0], FOX_BLOCK), 1)
    blocks = []
    for h in range(FOX_HEADS):
        pair = dense[:, (h // 2) * FOX_BLOCK:(h // 2 + 1) * FOX_BLOCK]
        own = pair if h % 2 == 0 else pltpu.roll(pair, FOX_HEAD_DIM, axis=1)
        blocks.append(jnp.where(lane < FOX_HEAD_DIM, own, spare[h]))
    return jnp.concatenate(blocks, axis=-1)


def _kv_kernel(tiles_per_seq, x_ref, g_ref, wk_ref, wv_ref, wf_ref, bf_ref, place_ref, k_ref, v_ref, c_ref,
               carry_ref):
    @pl.when(pl.program_id(0) % tiles_per_seq == 0)
    def _():
        carry_ref[...] = jnp.zeros_like(carry_ref)

    hn = _rms_normed(x_ref[...], g_ref[...]).astype(BF16)
    log_f = _log_sigmoid(_dot(hn, wf_ref[...]) + bf_ref[...])
    tm = log_f.shape[0]
    c = carry_ref[...] + _cumsum_rows(log_f)
    carry_ref[...] = c[tm - 1:tm, :]
    c_ref[...] = c
    k_ref[...] = _fox_blocks(_dot(hn, wk_ref[...]), _fox_bias_lanes(c, place_ref)).astype(BF16)
    lane = lax.broadcasted_iota(jnp.int32, (tm, FOX_BLOCK), 1)
    row_sum_lane = jnp.where(lane == FOX_HEAD_DIM, 1.0, 0.0)
    v_ref[...] = _fox_blocks(_dot(hn, wv_ref[...]), [row_sum_lane] * FOX_HEADS).astype(BF16)


def _kv_proj(x, gain, wk, wv, wf, bf, seq, tm):
    t = x.shape[0]
    place = _fox_bias_matrix(query_side=False)
    row = lambda n: pl.BlockSpec((tm, n), lambda i: (i, 0))
    return pl.pallas_call(
        functools.partial(_kv_kernel, seq // tm),
        grid=(t // tm,),
        in_specs=[row(D_MODEL), _full((1, D_MODEL)), _full(wk.shape), _full(wv.shape), _full(wf.shape),
                  _full(bf.shape), _full(place.shape)],
        out_specs=[row(FOX_WIDTH), row(FOX_WIDTH), row(LANES)],
        out_shape=[jax.ShapeDtypeStruct((t, FOX_WIDTH), BF16), jax.ShapeDtypeStruct((t, FOX_WIDTH), BF16),
                   jax.ShapeDtypeStruct((t, LANES), F32)],
        scratch_shapes=[pltpu.VMEM((1, LANES), F32)],
        compiler_params=_params(1),
        name="kv_proj_fox",
    )(x, gain, wk, wv, wf, bf, place)

def _in_b_kernel(x_ref, g_ref, c_ref, wq_ref, wmq_ref, place_ref, q_ref, mq_ref):
    hn = _rms_normed(x_ref[...], g_ref[...]).astype(BF16)
    q = _dot(hn, wq_ref[...]) * (FOX_HEAD_DIM ** -0.5 * LOG2E)
    q_ref[...] = _fox_blocks(q, _fox_bias_lanes(c_ref[...], place_ref)).astype(BF16)
    mq_ref[...] = _dot(hn, wmq_ref[...]).astype(BF16)


def _in_b(x, gain, c, wq, wmq, tm):
    t = x.shape[0]
    place = _fox_bias_matrix(query_side=True)
    row = lambda n: pl.BlockSpec((tm, n), lambda i: (i, 0))
    return pl.pallas_call(
        _in_b_kernel,
        grid=(t // tm,),
        in_specs=[row(D_MODEL), _full((1, D_MODEL)), row(LANES), _full(wq.shape), _full(wmq.shape),
                  _full(place.shape)],
        out_specs=[row(FOX_WIDTH), row(MEM_WIDTH)],
        out_shape=[jax.ShapeDtypeStruct((t, FOX_WIDTH), BF16), jax.ShapeDtypeStruct((t, MEM_WIDTH), BF16)],
        compiler_params=_params(1),
        name="in_proj_fox",
    )(x, gain, c, wq, wmq, place)


def _gla_kernel(chunks, n_seq, q_ref, k_ref, v_ref, r_ref, la_ref, on_ref, o_ref, *st_refs):
    @pl.when(pl.program_id(1) == 0)
    def _():
        for st_ref in st_refs:
            st_ref[...] = jnp.zeros_like(st_ref)

    n_sub = GLA_CHUNK // GLA_SUB
    masks = []
    for i in range(n_sub):
        n_keys = GLA_SUB * (i + 1)
        rr = lax.broadcasted_iota(jnp.int32, (GLA_SUB, n_keys), 0) + GLA_SUB * i
        cc = lax.broadcasted_iota(jnp.int32, (GLA_SUB, n_keys), 1)
        masks.append(cc <= rr)

    def chunk_body(c, carry):
        r0 = pl.multiple_of(c * GLA_CHUNK, GLA_CHUNK)
        rows = pl.ds(r0, GLA_CHUNK)
        streams = [(s, h) for s in range(n_seq) for h in range(GLA_HEADS)]
        b_all = [_cumsum_rows(la_ref[s, rows, :]) for s in range(n_seq)]
        vals, new_states, o_inter, scores = {}, {}, {}, {}
        for s, h in streams:
            ks = slice(h * GLA_DK_PAD, (h + 1) * GLA_DK_PAD)
            vs = slice(h * GLA_DV_PAD, (h + 1) * GLA_DV_PAD)
            q = q_ref[s, rows, ks].astype(F32)
            k = k_ref[s, rows, ks].astype(F32)
            v = v_ref[s, rows, vs]
            b = b_all[s][:, ks]
            b_last = b[GLA_CHUNK - 1:GLA_CHUNK, :]
            st = st_refs[s * GLA_HEADS + h][...]
            o_inter[s, h] = _dot_nt((q * jnp.exp(b)).astype(BF16), st.astype(BF16))
            kdec = (k * jnp.exp(b_last - b)).astype(BF16)
            new_states[s, h] = st * jnp.exp(b_last) + _dot_tn(v, kdec)
            head_scores = []
            for i in range(n_sub):
                lo, hi = GLA_SUB * i, GLA_SUB * (i + 1)
                b_i = b[lo:hi, :]
                qd = q[lo:hi, :] * jnp.exp(b_i - b[lo - 1:lo, :]) if i else q[lo:hi, :] * jnp.exp(b_i)
                kd = k[:hi, :] * jnp.exp(b[lo - 1:lo, :] - b[:hi, :]) if i else k[:hi, :] * jnp.exp(-b[:hi, :])
                head_scores.append(_dot_nt(qd.astype(BF16), kd.astype(BF16)))
            scores[s, h] = head_scores
            vals[s, h] = v
        outs = {}
        for s, h in streams:
            vs = slice(h * GLA_DV_PAD, (h + 1) * GLA_DV_PAD)
            o_parts = [_dot(jnp.where(masks[i], scores[s, h][i], 0.0).astype(BF16),
                            vals[s, h][:GLA_SUB * (i + 1), :]) for i in range(n_sub)]
            o = o_inter[s, h] + jnp.concatenate(o_parts, axis=0)
            ms = jnp.sum(o * o, axis=-1, keepdims=True) * (1.0 / GLA_DV)
            y = o * lax.rsqrt(ms + EPS) * on_ref[...] * _silu(r_ref[s, rows, vs].astype(F32))
            outs[s, h] = y.astype(BF16)
        for s, h in streams:
            st_refs[s * GLA_HEADS + h][...] = new_states[s, h]
            o_ref[s, rows, h * GLA_DV_PAD:(h + 1) * GLA_DV_PAD] = outs[s, h]
        return carry

    lax.fori_loop(0, chunks, chunk_body, 0, unroll=4)


def _gla(q, k, v, r, la, onorm, bsz, seq, tm):
    t = q.shape[0]
    n_seq = 2 if bsz % 2 == 0 else 1
    kw, vw = GLA_HEADS * GLA_DK_PAD, GLA_HEADS * GLA_DV_PAD
    rows = lambda n: pl.BlockSpec((n_seq, tm, n), lambda b, i: (b, i, 0))
    per_seq = lambda a: a.reshape(bsz, seq, a.shape[-1])
    out = pl.pallas_call(
        functools.partial(_gla_kernel, tm // GLA_CHUNK, n_seq),
        grid=(bsz // n_seq, seq // tm),
        in_specs=[rows(kw), rows(kw), rows(vw), rows(vw), rows(kw),
                  pl.BlockSpec((1, GLA_DV_PAD), lambda b, i: (0, 0))],
        out_specs=rows(vw),
        out_shape=jax.ShapeDtypeStruct((bsz, seq, vw), BF16),
        scratch_shapes=[pltpu.VMEM((GLA_DV_PAD, GLA_DK_PAD), F32)] * (GLA_HEADS * n_seq),
        compiler_params=_params(2),
        name="gla_scan",
    )(per_seq(q), per_seq(k), per_seq(v), per_seq(r), per_seq(la), onorm)
    return out.reshape(t, vw)


def _fox_kernel(tile, q_ref, k_ref, v_ref, o_ref):
    i = pl.program_id(2)
    q = q_ref[...]
    qs = (q[:, :FOX_BLOCK], q[:, FOX_BLOCK:])
    blks = (slice(0, FOX_BLOCK), slice(FOX_BLOCK, 2 * FOX_BLOCK))

    def absorb(m, acc, s, v):
        m_new = jnp.maximum(m, jnp.max(s, axis=-1, keepdims=True))
        p = jnp.exp2(s - m_new).astype(BF16)
        return m_new, jnp.exp2(m - m_new) * acc + _dot(p, v)

    def step(j, state):
        keys = pl.ds(pl.multiple_of(j * tile, tile), tile)
        return tuple(absorb(*state[hh], _dot_nt(qs[hh], k_ref[keys, blks[hh]]), v_ref[keys, blks[hh]])
                     for hh in range(2))

    init = (jnp.full((tile, 1), -jnp.inf, F32), jnp.zeros((tile, FOX_BLOCK), F32))
    state = lax.fori_loop(0, i, step, (init, init))

    band = tile // FOX_DIAG_BANDS
    start = pl.multiple_of(i * tile, tile)
    bands = [(hh, r) for r in range(FOX_DIAG_BANDS) for hh in range(2)]
    band_rows = lambda r: slice(r * band, (r + 1) * band)
    band_keys = lambda r: pl.ds(start, (r + 1) * band)
    scores = {(hh, r): _dot_nt(qs[hh][band_rows(r)], k_ref[band_keys(r), blks[hh]]) for hh, r in bands}
    causal = (lax.broadcasted_iota(jnp.int32, (band, band), 1) <= lax.broadcasted_iota(jnp.int32, (band, band), 0))
    lane = lax.broadcasted_iota(jnp.int32, (band, FOX_BLOCK), 1)
    for r in range(FOX_DIAG_BANDS):
        outs = []
        for hh in range(2):
            m, acc = state[hh]
            s = scores[hh, r]
            tail = jnp.where(causal, s[:, r * band:], -jnp.inf)
            s = jnp.concatenate([s[:, :r * band], tail], axis=1) if r else tail
            _, acc = absorb(m[band_rows(r)], acc[band_rows(r)], s, v_ref[band_keys(r), blks[hh]])
            outs.append(acc / jnp.sum(jnp.where(lane == FOX_HEAD_DIM, acc, 0.0), axis=-1, keepdims=True))
        o_ref[band_rows(r), :] = jnp.where(lane < FOX_HEAD_DIM, outs[0],
                                           pltpu.roll(outs[1], FOX_HEAD_DIM, axis=1)).astype(BF16)


def _fox(q, k, v, bsz, seq, tile):
    t = q.shape[0]
    nq = seq // tile
    pairs = FOX_HEADS // 2
    return pl.pallas_call(
        functools.partial(_fox_kernel, tile),
        grid=(bsz, pairs, nq),
        in_specs=[pl.BlockSpec((tile, 2 * FOX_BLOCK), lambda b, p, i: (b * nq + i, p)),
                  pl.BlockSpec((seq, 2 * FOX_BLOCK), lambda b, p, i: (b, p)),
                  pl.BlockSpec((seq, 2 * FOX_BLOCK), lambda b, p, i: (b, p))],
        out_specs=pl.BlockSpec((tile, LANES), lambda b, p, i: (b * nq + i, p)),
        out_shape=jax.ShapeDtypeStruct((t, MAIN_WIDTH), BF16),
        compiler_params=_params(3),
        name="fox_attention",
    )(q, k, v)


def _out_kernel(x_ref, main_ref, mq_ref, mk_ref, mv_ref, wo_main_ref, wo_mem_ref, o_ref):
    mq, mk, mv = mq_ref[...], mk_ref[...], mv_ref[...]
    head = lambda a, h: a[:, h * MEM_HEAD_DIM:(h + 1) * MEM_HEAD_DIM]
    scores = [_dot_nt(head(mq, h), head(mk, h)) * (MEM_HEAD_DIM ** -0.5) for h in range(MEM_HEADS)]
    base = x_ref[...] + _dot(main_ref[...], wo_main_ref[...])
    heads = []
    for h in range(MEM_HEADS):
        s = scores[h]
        p = jnp.exp(s - jnp.max(s, axis=-1, keepdims=True))
        l = jnp.sum(p, axis=-1, keepdims=True)
        heads.append(_dot(p.astype(BF16), head(mv, h)) / l)
    mem_o = jnp.concatenate(heads, axis=-1).astype(BF16)
    o_ref[...] = base + _dot(mem_o, wo_mem_ref[...])


def _out_proj(x, main, mq, mem_k, mem_v, wo_main, wo_mem, seq, tm):
    t = x.shape[0]
    tiles_per_seq = seq // tm
    n_mem = mem_k.shape[1]
    row = lambda n: pl.BlockSpec((tm, n), lambda i: (i, 0))
    mem = pl.BlockSpec((None, n_mem, MEM_WIDTH), lambda i: (i // tiles_per_seq, 0, 0))
    return pl.pallas_call(
        _out_kernel,
        grid=(t // tm,),
        in_specs=[row(D_MODEL), row(main.shape[1]), row(MEM_WIDTH), mem, mem, _full(wo_main.shape),
                  _full(wo_mem.shape)],
        out_specs=row(D_MODEL),
        out_shape=jax.ShapeDtypeStruct((t, D_MODEL), F32),
        compiler_params=_params(1),
        name="out_proj_mem_attn",
    )(x, main, mq, mem_k, mem_v, wo_main, wo_mem)


def _swiglu_acc(hn, wg_ref, wu_ref, wd_ref, acc_ref, row_scale=None):
    for j in range(D_FF // FF_CHUNK):
        cs = slice(j * FF_CHUNK, (j + 1) * FF_CHUNK)
        a = _silu(_dot(hn, wg_ref[:, cs])) * _dot(hn, wu_ref[:, cs])
        if row_scale is not None:
            a = a * row_scale
        acc_ref[...] += _dot(a.astype(BF16), wd_ref[cs, :])


def _ffn_kernel(x_ref, g_ref, wg_ref, wu_ref, wd_ref, o_ref):
    x = x_ref[...]
    o_ref[...] = x
    _swiglu_acc(_rms_normed(x, g_ref[...]).astype(BF16), wg_ref, wu_ref, wd_ref, o_ref)


def _ffn(x, gain, wg, wu, wd, tm, riders):
    t = x.shape[0]
    steps = t // tm
    row = pl.BlockSpec((tm, D_MODEL), lambda i: (i, 0))
    held = lambda shape: pl.BlockSpec(shape, lambda i: (0, 0), pipeline_mode=pl.Buffered(1))
    rider_specs, rider_shapes = _rider_blocks(riders, steps)
    out = pl.pallas_call(
        _riding(_ffn_kernel, 5, 1, len(riders)),
        grid=(steps,),
        in_specs=[row, _full((1, D_MODEL)), held(wg.shape), held(wu.shape), held(wd.shape)] + rider_specs,
        out_specs=[row] + rider_specs,
        out_shape=[jax.ShapeDtypeStruct((t, D_MODEL), F32)] + rider_shapes,
        compiler_params=_params(1),
        name="dense_swiglu",
    )(x, gain, wg, wu, wd, *riders)
    return out[0], out[1:]


ROW_TILE = D_MODEL // LANES
ROUTE_E1, ROUTE_E2, ROUTE_W1, ROUTE_W2, ROUTE_RANK1, ROUTE_RANK2 = range(6)


def _rows_to_tiles(dst_ref, rows):
    n = rows.shape[0]
    for j in range(ROW_TILE):
        dst_ref[pl.ds(j, n, stride=ROW_TILE), :] = rows[:, j * LANES:(j + 1) * LANES]


def _tiles_to_rows(src_ref, n):
    return jnp.concatenate([src_ref[pl.ds(j, n, stride=ROW_TILE), :] for j in range(ROW_TILE)], axis=1)


def _row_tile(ref, row):
    return ref.at[pl.ds(pl.multiple_of(row * ROW_TILE, ROW_TILE), ROW_TILE)]


def _router_kernel(x_ref, g_ref, wr_ref, route_ref, fields_ref, counts_ref, carry_ref):
    @pl.when(pl.program_id(0) == 0)
    def _():
        carry_ref[...] = jnp.zeros_like(carry_ref)

    hn = _rms_normed(x_ref[...], g_ref[...])
    logits = _dot_3pass(hn, wr_ref[...]).T[:N_EXPERTS, :]
    tm = logits.shape[1]
    expert = lax.broadcasted_iota(jnp.int32, logits.shape, 0)
    m1 = jnp.max(logits, axis=0, keepdims=True)
    i1 = jnp.min(jnp.where(logits == m1, expert, N_EXPERTS), axis=0, keepdims=True)
    rest = jnp.where(expert == i1, -jnp.inf, logits)
    m2 = jnp.max(rest, axis=0, keepdims=True)
    i2 = jnp.min(jnp.where(rest == m2, expert, N_EXPERTS), axis=0, keepdims=True)
    e2 = jnp.exp(m2 - m1)
    w1 = 1.0 / (1.0 + e2)
    chosen = jnp.where((expert == i1) | (expert == i2), 1.0, 0.0)
    token = lax.broadcasted_iota(jnp.int32, logits.shape, 1)
    run = chosen
    shift = 1
    while shift < tm:
        run = run + jnp.where(token >= shift, pltpu.roll(run, shift, axis=1), 0.0)
        shift *= 2
    base = carry_ref[...][:, :1] + (run - chosen)
    rank1 = jnp.sum(jnp.where(expert == i1, base, 0.0), axis=0, keepdims=True)
    rank2 = jnp.sum(jnp.where(expert == i2, base, 0.0), axis=0, keepdims=True)
    carry_ref[...] += jnp.sum(chosen, axis=1, keepdims=True)
    counts_ref[...] = carry_ref[...]
    fields = jnp.zeros_like(logits)
    for slot, val in ((ROUTE_E1, i1.astype(F32)), (ROUTE_E2, i2.astype(F32)), (ROUTE_W1, w1), (ROUTE_W2, e2 * w1),
                      (ROUTE_RANK1, rank1), (ROUTE_RANK2, rank2)):
        fields = jnp.where(expert == slot, val, fields)
    fields_ref[...] = fields
    route_ref[...] = jnp.concatenate([fields, jnp.zeros((LANES - N_EXPERTS, tm), F32)], axis=0).T


def _router(x, gain, wr, tm):
    t = x.shape[0]
    row = lambda n: pl.BlockSpec((tm, n), lambda i: (i, 0))
    n_fields = N_EXPERTS
    return pl.pallas_call(
        _router_kernel,
        grid=(t // tm,),
        in_specs=[row(D_MODEL), _full((1, D_MODEL)), _full(wr.shape)],
        out_specs=[row(LANES), pl.BlockSpec((None, n_fields, tm), lambda i: (i, 0, 0)), _full((N_EXPERTS, LANES))],
        out_shape=[jax.ShapeDtypeStruct((t, LANES), F32), jax.ShapeDtypeStruct((t // tm, n_fields, tm), F32),
                   jax.ShapeDtypeStruct((N_EXPERTS, LANES), F32)],
        scratch_shapes=[pltpu.VMEM((N_EXPERTS, LANES), F32)],
        compiler_params=_params(1),
        name="moe_router",
    )(x, gain, wr)


def _dispatch_kernel(tm, cnt_ref, pad_ref, off_ref, pos1_ref, pos2_ref, x_ref, g_ref, xs_ref, stage_ref, zero_ref,
                     sem, pad_sem):
    @pl.when(pl.program_id(0) == 0)
    def _():
        zero_ref[...] = jnp.zeros_like(zero_ref)
        for e in range(N_EXPERTS + 1):
            lo, hi = off_ref[e] + cnt_ref[e], off_ref[e] + pad_ref[e]

            def fill(r, c):
                pltpu.make_async_copy(zero_ref, _row_tile(xs_ref, r), pad_sem).start()
                return c

            def drain(r, c):
                pltpu.make_async_copy(zero_ref, _row_tile(xs_ref, r), pad_sem).wait()
                return c

            lax.fori_loop(lo, hi, fill, 0)
            lax.fori_loop(lo, hi, drain, 0)

    _rows_to_tiles(stage_ref, _rms_normed(x_ref[...], g_ref[...]))

    def issue(r, c):
        src = _row_tile(stage_ref, r)
        pltpu.make_async_copy(src, _row_tile(xs_ref, pos1_ref[0, r]), sem).start(priority=0)
        pltpu.make_async_copy(src, _row_tile(xs_ref, pos2_ref[0, r]), sem).start(priority=1)
        return c

    lax.fori_loop(0, tm, issue, 0, unroll=8)
    for _ in range(2):
        pltpu.make_async_copy(stage_ref, xs_ref.at[pl.ds(0, tm * ROW_TILE)], sem).wait()


def _dispatch(x, gain, pos1, pos2, counts, padded, offsets, n_rows, tm):
    t = x.shape[0]
    nt = t // tm
    smem_row = pl.BlockSpec((None, 1, tm), lambda i, *_: (i, 0, 0), memory_space=pltpu.SMEM)
    return pl.pallas_call(
        functools.partial(_dispatch_kernel, tm),
        grid_spec=pltpu.PrefetchScalarGridSpec(
            num_scalar_prefetch=3,
            grid=(nt,),
            in_specs=[smem_row, smem_row, pl.BlockSpec((tm, D_MODEL), lambda i, *_: (i, 0)),
                      pl.BlockSpec((1, D_MODEL), lambda i, *_: (0, 0))],
            out_specs=pl.BlockSpec(memory_space=pl.ANY),
            scratch_shapes=[pltpu.VMEM((tm * ROW_TILE, LANES), F32), pltpu.VMEM((ROW_TILE, LANES), F32),
                            pltpu.SemaphoreType.DMA, pltpu.SemaphoreType.DMA],
        ),
        out_shape=jax.ShapeDtypeStruct((n_rows * ROW_TILE, LANES), F32),
        compiler_params=_params(1),
        name="moe_dispatch",
    )(counts, padded, offsets, pos1.reshape(nt, 1, tm), pos2.reshape(nt, 1, tm), x, gain)


def _experts_kernel(tm, tile_expert_ref, n_used_ref, xs_ref, wg_ref, wu_ref, wd_ref, o_ref, acc_ref):
    acc_ref[...] = jnp.zeros_like(acc_ref)

    @pl.when(pl.program_id(0) < n_used_ref[0])
    def _():
        _swiglu_acc(_tiles_to_rows(xs_ref, tm).astype(BF16), wg_ref, wu_ref, wd_ref, acc_ref)

    _rows_to_tiles(o_ref, acc_ref[...])


def _experts(xs, tile_expert, n_used, wg, wu, wd, tm):
    n_tiles = xs.shape[0] // (tm * ROW_TILE)
    wspec = lambda shape: pl.BlockSpec(
        (None,) + shape, lambda i, te, nu: (te[jnp.minimum(i, nu[0] - 1)], 0, 0))
    rows = pl.BlockSpec((tm * ROW_TILE, LANES), lambda i, te, nu: (i, 0))
    return pl.pallas_call(
        functools.partial(_experts_kernel, tm),
        grid_spec=pltpu.PrefetchScalarGridSpec(
            num_scalar_prefetch=2,
            grid=(n_tiles,),
            in_specs=[rows, wspec((D_MODEL, D_FF)), wspec((D_MODEL, D_FF)), wspec((D_FF, D_MODEL))],
            out_specs=rows,
            scratch_shapes=[pltpu.VMEM((tm, D_MODEL), F32)],
        ),
        out_shape=jax.ShapeDtypeStruct(xs.shape, F32),
        compiler_params=_params(1),
        name="moe_experts",
    )(tile_expert, n_used, xs, wg, wu, wd)


def _combine_kernel(tm, n_tiles, pos1_ref, pos2_ref, next1_ref, next2_ref, x_ref, route_ref, gf_ref, y_ref,
                    o_ref, buf_ref, sems):
    i = pl.program_id(0)
    slot = i % 2

    def gather(p1_ref, p2_ref, slot):
        def issue(r, c):
            pltpu.make_async_copy(_row_tile(y_ref, p1_ref[0, r]), _row_tile(buf_ref.at[slot, 0], r),
                                  sems.at[slot]).start(priority=0)
            pltpu.make_async_copy(_row_tile(y_ref, p2_ref[0, r]), _row_tile(buf_ref.at[slot, 1], r),
                                  sems.at[slot]).start(priority=1)
            return c

        lax.fori_loop(0, tm, issue, 0, unroll=8)

    @pl.when(i == 0)
    def _():
        gather(pos1_ref, pos2_ref, 0)

    @pl.when(i + 1 < n_tiles)
    def _():
        gather(next1_ref, next2_ref, 1 - slot)

    route = route_ref[...]
    lane = lax.broadcasted_iota(jnp.int32, route.shape, 1)
    w1 = jnp.sum(jnp.where(lane == ROUTE_W1, route, 0.0), axis=-1, keepdims=True)
    w2 = jnp.sum(jnp.where(lane == ROUTE_W2, route, 0.0), axis=-1, keepdims=True)
    for k in range(2):
        pltpu.make_async_copy(y_ref.at[pl.ds(0, tm * ROW_TILE)], buf_ref.at[slot, k], sems.at[slot]).wait()
    y1, y2 = (_tiles_to_rows(buf_ref.at[slot, k], tm) for k in range(2))
    o_ref[...] = _rms_normed(x_ref[...] + w1 * y1 + w2 * y2, gf_ref[...])


def _combine(x, route, pos1, pos2, y, gain_final, tm):
    t = x.shape[0]
    nt = t // tm
    smem_row = lambda ahead: pl.BlockSpec((None, 1, tm), lambda i: (jnp.minimum(i + ahead, nt - 1), 0, 0),
                                          memory_space=pltpu.SMEM)
    row = lambda n: pl.BlockSpec((tm, n), lambda i: (i, 0))
    pos1, pos2 = pos1.reshape(nt, 1, tm), pos2.reshape(nt, 1, tm)
    return pl.pallas_call(
        functools.partial(_combine_kernel, tm, nt),
        grid=(nt,),
        in_specs=[smem_row(0), smem_row(0), smem_row(1), smem_row(1), row(D_MODEL), row(LANES),
                  _full((1, D_MODEL)), pl.BlockSpec(memory_space=pl.ANY)],
        out_specs=row(D_MODEL),
        out_shape=jax.ShapeDtypeStruct((t, D_MODEL), F32),
        scratch_shapes=[pltpu.VMEM((2, 2, tm * ROW_TILE, LANES), F32), pltpu.SemaphoreType.DMA((2,))],
        compiler_params=_params(1),
        name="moe_combine",
    )(pos1, pos2, pos1, pos2, x, route, gain_final, y)


def _moe(x, gain, gain_final, wr, wg, wu, wd, tm):
    t = x.shape[0]
    route, fields, counts = _router(x, gain, wr, tm)
    col = lambda c: fields[:, c, :].reshape(t).astype(jnp.int32)
    counts = counts[:, 0].astype(jnp.int32)
    padded = (counts + tm - 1) // tm * tm
    ends = jnp.cumsum(padded)
    offsets = ends - padded
    pos1 = offsets[col(ROUTE_E1)] + col(ROUTE_RANK1)
    pos2 = offsets[col(ROUTE_E2)] + col(ROUTE_RANK2)
    n_rows = 2 * t + N_EXPERTS * tm
    tile_start = jnp.arange(n_rows // tm, dtype=jnp.int32) * tm
    tile_expert = jnp.minimum(jnp.sum(tile_start[:, None] >= ends[None, :], axis=1), N_EXPERTS - 1).astype(jnp.int32)
    n_used = (ends[-1:] // tm).astype(jnp.int32)
    tail = lambda a, v: jnp.concatenate([a, v.astype(jnp.int32)])
    xs = _dispatch(x, gain, pos1, pos2, tail(counts, jnp.zeros((1,))), tail(padded, n_rows - ends[-1:]),
                   tail(offsets, ends[-1:]), n_rows, tm)
    y = _experts(xs, tile_expert, n_used, wg, wu, wd, tm)
    return _combine(x, route, pos1, pos2, y, gain_final, tm // 2)


def _pad_heads(w, heads, width, padded):
    lead = w.shape[:-1]
    w = w.reshape(lead + (heads, width))
    w = jnp.pad(w, [(0, 0)] * len(lead) + [(0, 0), (0, padded - width)])
    return w.reshape(lead + (heads * padded,))


def _pad_cols(w, n):
    return jnp.pad(w, [(0, 0)] * (w.ndim - 1) + [(0, n - w.shape[-1])])


def kernel(x, mem, norm_mix, norm_mem, norm_ffn, norm_kv, norm_final, w_in_a, w_gla_gate2, b_gla_gate2,
           gla_onorm, w_in_b, w_kv, b_forget, w_mem_kv, w_out, w_ff_gate, w_ff_up, w_ff_down, w_router,
           w_moe_gate, w_moe_up, w_moe_down):
    bsz, seq, _ = x.shape
    n_mem = mem.shape[1]
    t = bsz * seq
    tm = 512
    xt = x.reshape(t, D_MODEL)
    memt = mem.reshape(bsz * n_mem, D_MODEL)
    gain = lambda g: g.reshape(1, D_MODEL)
    kw = GLA_HEADS * GLA_DK

    def mem_kv(layer):
        wk, wv = w_mem_kv[layer, :, :MEM_WIDTH], w_mem_kv[layer, :, MEM_WIDTH:]
        mk, mv = _norm_proj(memt, gain(norm_mem[layer]), [wk.astype(BF16), wv.astype(BF16)], (1.0, 1.0),
                            n_mem, "mem_kv_proj")
        return mk.reshape(bsz, n_mem, MEM_WIDTH), mv.reshape(bsz, n_mem, MEM_WIDTH)

    wa = w_in_a[0]
    wq = _pad_heads(wa[:, :kw], GLA_HEADS, GLA_DK, GLA_DK_PAD).astype(BF16)
    wk = _pad_heads(wa[:, kw:2 * kw], GLA_HEADS, GLA_DK, GLA_DK_PAD).astype(BF16)
    off = 2 * kw
    wv = _pad_heads(wa[:, off:off + MAIN_WIDTH], GLA_HEADS, GLA_DV, GLA_DV_PAD).astype(BF16)
    off += MAIN_WIDTH
    wr = _pad_heads(wa[:, off:off + MAIN_WIDTH], GLA_HEADS, GLA_DV, GLA_DV_PAD).astype(BF16)
    off += MAIN_WIDTH
    wg1 = _pad_cols(wa[:, off:off + GLA_GATE_RANK], LANES).astype(BF16)
    off += GLA_GATE_RANK
    wmq = wa[:, off:off + MEM_WIDTH].astype(BF16)
    wg2 = jnp.pad(_pad_heads(w_gla_gate2[0], GLA_HEADS, GLA_DK, GLA_DK_PAD), ((0, LANES - GLA_GATE_RANK), (0, 0)))
    bg2 = _pad_heads(b_gla_gate2[0], GLA_HEADS, GLA_DK, GLA_DK_PAD).reshape(1, -1)
    onorm = _pad_cols(gla_onorm[0], GLA_DV_PAD).reshape(1, GLA_DV_PAD)

    dense_weights = [w_ff_gate[0], w_ff_up[0], w_ff_down[0].reshape(2 * D_FF, D_MODEL // 2)]
    (q, k, v, r, mq, la), dense_weights = _in_a(xt, gain(norm_mix[0]), wq, wk, wv, wr, wmq, wg1, wg2, bg2, 2 * tm,
                                                dense_weights)
    main = _gla(q, k, v, r, la, onorm, bsz, seq, tm)
    mk, mv = mem_kv(0)
    wo = w_out[0]
    wo_main = jnp.pad(wo[:MAIN_WIDTH].reshape(GLA_HEADS, GLA_DV, D_MODEL),
                      ((0, 0), (0, GLA_DV_PAD - GLA_DV), (0, 0))).reshape(GLA_HEADS * GLA_DV_PAD, D_MODEL)
    xt = _out_proj(xt, main, mq, mk, mv, wo_main.astype(BF16), wo[MAIN_WIDTH:].astype(BF16), seq, 2 * tm)
    expert_weights = [w_moe_gate[0].reshape(N_EXPERTS * D_MODEL, D_FF), w_moe_up[0].reshape(N_EXPERTS * D_MODEL, D_FF),
                      w_moe_down[0].reshape(N_EXPERTS * D_FF, D_MODEL)]
    xt, expert_weights = _ffn(xt, gain(norm_ffn[0]), dense_weights[0], dense_weights[1],
                              dense_weights[2].reshape(D_FF, D_MODEL), tm, expert_weights)
    wg, wu, wd = (w.reshape(N_EXPERTS, -1, w.shape[-1]) for w in expert_weights)

    wf = _pad_cols(w_kv[:, 2 * MAIN_WIDTH:], LANES).astype(BF16)
    bf = _pad_cols(b_forget, LANES).reshape(1, LANES)
    k_sh, v_sh, c = _kv_proj(xt, gain(norm_kv), w_kv[:, :MAIN_WIDTH].astype(BF16),
                             w_kv[:, MAIN_WIDTH:2 * MAIN_WIDTH].astype(BF16), wf, bf, seq, tm)

    wb = w_in_b[0]
    qf, mq = _in_b(xt, gain(norm_mix[1]), c, wb[:, :MAIN_WIDTH].astype(BF16), wb[:, MAIN_WIDTH:].astype(BF16),
                   2 * tm)
    main = _fox(qf, k_sh, v_sh, bsz, seq, 1024)
    mk, mv = mem_kv(1)
    wo = w_out[1]
    xt = _out_proj(xt, main, mq, mk, mv, wo[:MAIN_WIDTH].astype(BF16), wo[MAIN_WIDTH:].astype(BF16), seq, 2 * tm)
    out = _moe(xt, gain(norm_ffn[1]), gain(norm_final), _pad_cols(w_router[0], LANES), wg, wu, wd, tm)
    return out.reshape(bsz, seq, D_MODEL)
```

```python
import functools

import jax
import jax.numpy as jnp
import numpy as np
from jax import lax
from jax.experimental import pallas as pl
from jax.experimental.pallas import tpu as pltpu

D_MODEL = 1024
EPS = 1e-6
GLA_CHUNK = 64
GLA_SUB = 16
MEM_HEADS = 4
MEM_HEAD_DIM = 64
MEM_WIDTH = MEM_HEADS * MEM_HEAD_DIM
MAIN_WIDTH = D_MODEL - MEM_WIDTH
GLA_HEADS = 4
GLA_DK = MAIN_WIDTH // 2 // GLA_HEADS
GLA_DV = MAIN_WIDTH // GLA_HEADS
GLA_DK_PAD = 128
GLA_DV_PAD = 256
GLA_GATE_RANK = 16
GLA_GATE_TAU = 16.0
FOX_HEADS = 12
FOX_HEAD_DIM = 64
D_FF = 2816
N_EXPERTS = 8
LANES = 128
FF_CHUNK = 256
VMEM_LIMIT = 56 * 1024 * 1024

BF16 = jnp.bfloat16
F32 = jnp.float32


def _params(n_axes, vmem=VMEM_LIMIT):
    return pltpu.CompilerParams(dimension_semantics=("arbitrary",) * n_axes, vmem_limit_bytes=vmem)


def _rms_normed(x, gain):
    ms = jnp.mean(x * x, axis=-1, keepdims=True)
    return x * lax.rsqrt(ms + EPS) * gain


def _log_sigmoid(z):
    return jnp.minimum(z, 0.0) - jnp.log(1.0 + jnp.exp(-jnp.abs(z)))


def _silu(z):
    return z / (1.0 + jnp.exp(-z))


def _dot(a, b):
    return jnp.dot(a, b, preferred_element_type=F32)


def _dot_nt(a, b):
    return lax.dot_general(a, b, (((1,), (1,)), ((), ())), preferred_element_type=F32)


def _dot_tn(a, b):
    return lax.dot_general(a, b, (((0,), (0,)), ((), ())), preferred_element_type=F32)


def _bf16_terms(v):
    hi = v.astype(BF16).astype(F32)
    mid = (v - hi).astype(BF16).astype(F32)
    return hi, mid, v - hi - mid


def _dot_3pass(a, b):
    a_hi, a_lo, _ = _bf16_terms(a)
    b_hi, b_lo, _ = _bf16_terms(b)
    a_hi, a_lo, b_hi, b_lo = (t.astype(BF16) for t in (a_hi, a_lo, b_hi, b_lo))
    return _dot(a_hi, b_hi) + _dot(a_hi, b_lo) + _dot(a_lo, b_hi)


def _cumsum_rows(x):
    n = x.shape[0]
    r = lax.broadcasted_iota(jnp.int32, (n, n), 0)
    c = lax.broadcasted_iota(jnp.int32, (n, n), 1)
    tril = jnp.where(c <= r, 1.0, 0.0).astype(BF16)
    hi, mid, lo = _bf16_terms(x)
    return _dot(tril, hi.astype(BF16)) + _dot(tril, mid.astype(BF16)) + _dot(tril, lo.astype(BF16))


def _full(shape):
    return pl.BlockSpec(shape, lambda *_: (0,) * len(shape))


BF16_SUBLANES = 16


def _riding(kernel_fn, n_in, n_out, n_riders):
    def body(*refs):
        ins, rest = refs[:n_in], refs[n_in:]
        rider_in, rest = rest[:n_riders], rest[n_riders:]
        outs, rest = rest[:n_out], rest[n_out:]
        rider_out, scratch = rest[:n_riders], rest[n_riders:]
        for src_ref, dst_ref in zip(rider_in, rider_out):
            dst_ref[...] = src_ref[...].astype(BF16)
        kernel_fn(*ins, *outs, *scratch)
    return body


def _rider_blocks(riders, steps):
    for a in riders:
        assert a.shape[0] % (steps * BF16_SUBLANES) == 0, (a.shape, steps)
    specs = [pl.BlockSpec((a.shape[0] // steps, a.shape[1]), lambda i: (i, 0)) for a in riders]
    return specs, [jax.ShapeDtypeStruct(a.shape, BF16) for a in riders]


def _in_a_kernel(x_ref, g_ref, wq_ref, wk_ref, wv_ref, wr_ref, wmq_ref, wg1_ref, wg2_ref, bg2_ref,
                 q_ref, k_ref, v_ref, r_ref, mq_ref, la_ref):
    hn = _rms_normed(x_ref[...], g_ref[...]).astype(BF16)
    q_ref[...] = (_dot(hn, wq_ref[...]) * (GLA_DK ** -0.5)).astype(BF16)
    k_ref[...] = _dot(hn, wk_ref[...]).astype(BF16)
    v_ref[...] = _dot(hn, wv_ref[...]).astype(BF16)
    r_ref[...] = _dot(hn, wr_ref[...]).astype(BF16)
    mq_ref[...] = _dot(hn, wmq_ref[...]).astype(BF16)
    glr = _dot(hn, wg1_ref[...])
    z = _dot_3pass(glr, wg2_ref[...])
    la_ref[...] = _log_sigmoid(z + bg2_ref[...]) * (1.0 / GLA_GATE_TAU)


def _in_a(x, gain, wq, wk, wv, wr, wmq, wg1, wg2, bg2, tm, riders):
    t = x.shape[0]
    steps = t // tm
    row = lambda n: pl.BlockSpec((tm, n), lambda i: (i, 0))
    kw, vw = GLA_HEADS * GLA_DK_PAD, GLA_HEADS * GLA_DV_PAD
    rider_specs, rider_shapes = _rider_blocks(riders, steps)
    out = pl.pallas_call(
        _riding(_in_a_kernel, 10, 6, len(riders)),
        grid=(steps,),
        in_specs=[row(D_MODEL), _full((1, D_MODEL)), _full(wq.shape), _full(wk.shape), _full(wv.shape),
                  _full(wr.shape), _full(wmq.shape), _full(wg1.shape), _full(wg2.shape), _full(bg2.shape)]
                 + rider_specs,
        out_specs=[row(kw), row(kw), row(vw), row(vw), row(MEM_WIDTH), row(kw)] + rider_specs,
        out_shape=[jax.ShapeDtypeStruct((t, kw), BF16), jax.ShapeDtypeStruct((t, kw), BF16),
                   jax.ShapeDtypeStruct((t, vw), BF16), jax.ShapeDtypeStruct((t, vw), BF16),
                   jax.ShapeDtypeStruct((t, MEM_WIDTH), BF16), jax.ShapeDtypeStruct((t, kw), F32)] + rider_shapes,
        compiler_params=_params(1),
        name="in_proj_gla",
    )(x, gain, wq, wk, wv, wr, wmq, wg1, wg2, bg2, *riders)
    return out[:6], out[6:]


def _norm_proj_kernel(n_out, scales, x_ref, g_ref, *refs):
    w_refs, o_refs = refs[:n_out], refs[n_out:]
    hn = _rms_normed(x_ref[...], g_ref[...]).astype(BF16)
    for w_ref, o_ref, s in zip(w_refs, o_refs, scales):
        o = _dot(hn, w_ref[...])
        if s != 1.0:
            o = o * s
        o_ref[...] = o.astype(o_ref.dtype)


def _norm_proj(x, gain, weights, scales, tm, name):
    t = x.shape[0]
    row = lambda n: pl.BlockSpec((tm, n), lambda i: (i, 0))
    return pl.pallas_call(
        functools.partial(_norm_proj_kernel, len(weights), scales),
        grid=(t // tm,),
        in_specs=[row(D_MODEL), _full((1, D_MODEL))] + [_full(w.shape) for w in weights],
        out_specs=[row(w.shape[1]) for w in weights],
        out_shape=[jax.ShapeDtypeStruct((t, w.shape[1]), BF16) for w in weights],
        compiler_params=_params(1),
        name=name,
    )(x, gain, *weights)


FOX_BLOCK = LANES
FOX_WIDTH = FOX_HEADS * FOX_BLOCK
LOG2E = 1.4426950408889634
FOX_DIAG_BANDS = 4


FOX_ONE_LANE = 3 * FOX_HEADS


def _fox_bias_matrix(query_side):
    e = np.zeros((LANES, FOX_WIDTH), np.float32)
    for h in range(FOX_HEADS):
        base = h * FOX_BLOCK + FOX_HEAD_DIM
        for n in range(3):
            if query_side:
                e[n * FOX_HEADS + h, base + n] = 1.0
                e[FOX_ONE_LANE, base + 3 + n] = 1.0
            else:
                e[FOX_ONE_LANE, base + n] = 1.0
                e[n * FOX_HEADS + h, base + 3 + n] = -1.0
    return jnp.asarray(e, BF16)


def _fox_bias_lanes(c, place_ref):
    hi, mid, lo = _bf16_terms(c * LOG2E)
    lane = lax.broadcasted_iota(jnp.int32, c.shape, 1)
    packed = jnp.where(lane == FOX_ONE_LANE, 1.0, 0.0)
    for n, term in ((2, lo), (1, mid), (0, hi)):
        shifted = pltpu.roll(term, n * FOX_HEADS, axis=1) if n else term
        packed = jnp.where((lane >= n * FOX_HEADS) & (lane < (n + 1) * FOX_HEADS), shifted, packed)
    placed = _dot(packed.astype(BF16), place_ref[...])
    return [placed[:, h * FOX_BLOCK:(h + 1) * FOX_BLOCK] for h in range(FOX_HEADS)]


def _fox_blocks(dense, spare):
    lane = lax.broadcasted_iota(jnp.int32, (dense.shape[0], FOX_BLOCK), 1)
    blocks = []
    for h in range(FOX_HEADS):
        pair = dense[:, (h // 2) * FOX_BLOCK:(h // 2 + 1) * FOX_BLOCK]
        own = pair if h % 2 == 0 else pltpu.roll(pair, FOX_HEAD_DIM, axis=1)
        blocks.append(jnp.where(lane < FOX_HEAD_DIM, own, spare[h]))
    return jnp.concatenate(blocks, axis=-1)


def _kv_kernel(tiles_per_seq, x_ref, g_ref, wk_ref, wv_ref, wf_ref, bf_ref, place_ref, k_ref, v_ref, c_ref,
               carry_ref):
    @pl.when(pl.program_id(0) % tiles_per_seq == 0)
    def _():
        carry_ref[...] = jnp.zeros_like(carry_ref)

    hn = _rms_normed(x_ref[...], g_ref[...]).astype(BF16)
    log_f = _log_sigmoid(_dot(hn, wf_ref[...]) + bf_ref[...])
    tm = log_f.shape[0]
    c = carry_ref[...] + _cumsum_rows(log_f)
    carry_ref[...] = c[tm - 1:tm, :]
    c_ref[...] = c
    k_ref[...] = _fox_blocks(_dot(hn, wk_ref[...]), _fox_bias_lanes(c, place_ref)).astype(BF16)
    lane = lax.broadcasted_iota(jnp.int32, (tm, FOX_BLOCK), 1)
    row_sum_lane = jnp.where(lane == FOX_HEAD_DIM, 1.0, 0.0)
    v_ref[...] = _fox_blocks(_dot(hn, wv_ref[...]), [row_sum_lane] * FOX_HEADS).astype(BF16)


def _kv_proj(x, gain, wk, wv, wf, bf, seq, tm):
    t = x.shape[0]
    place = _fox_bias_matrix(query_side=False)
    row = lambda n: pl.BlockSpec((tm, n), lambda i: (i, 0))
    return pl.pallas_call(
        functools.partial(_kv_kernel, seq // tm),
        grid=(t // tm,),
        in_specs=[row(D_MODEL), _full((1, D_MODEL)), _full(wk.shape), _full(wv.shape), _full(wf.shape),
                  _full(bf.shape), _full(place.shape)],
        out_specs=[row(FOX_WIDTH), row(FOX_WIDTH), row(LANES)],
        out_shape=[jax.ShapeDtypeStruct((t, FOX_WIDTH), BF16), jax.ShapeDtypeStruct((t, FOX_WIDTH), BF16),
                   jax.ShapeDtypeStruct((t, LANES), F32)],
        scratch_shapes=[pltpu.VMEM((1, LANES), F32)],
        compiler_params=_params(1),
        name="kv_proj_fox",
    )(x, gain, wk, wv, wf, bf, place)

def _in_b_kernel(x_ref, g_ref, c_ref, wq_ref, wmq_ref, place_ref, q_ref, mq_ref):
    hn = _rms_normed(x_ref[...], g_ref[...]).astype(BF16)
    q = _dot(hn, wq_ref[...]) * (FOX_HEAD_DIM ** -0.5 * LOG2E)
    q_ref[...] = _fox_blocks(q, _fox_bias_lanes(c_ref[...], place_ref)).astype(BF16)
    mq_ref[...] = _dot(hn, wmq_ref[...]).astype(BF16)


def _in_b(x, gain, c, wq, wmq, tm):
    t = x.shape[0]
    place = _fox_bias_matrix(query_side=True)
    row = lambda n: pl.BlockSpec((tm, n), lambda i: (i, 0))
    return pl.pallas_call(
        _in_b_kernel,
        grid=(t // tm,),
        in_specs=[row(D_MODEL), _full((1, D_MODEL)), row(LANES), _full(wq.shape), _full(wmq.shape),
                  _full(place.shape)],
        out_specs=[row(FOX_WIDTH), row(MEM_WIDTH)],
        out_shape=[jax.ShapeDtypeStruct((t, FOX_WIDTH), BF16), jax.ShapeDtypeStruct((t, MEM_WIDTH), BF16)],
        compiler_params=_params(1),
        name="in_proj_fox",
    )(x, gain, c, wq, wmq, place)


def _gla_kernel(chunks, n_seq, q_ref, k_ref, v_ref, r_ref, la_ref, on_ref, o_ref, *st_refs):
    @pl.when(pl.program_id(1) == 0)
    def _():
        for st_ref in st_refs:
            st_ref[...] = jnp.zeros_like(st_ref)

    n_sub = GLA_CHUNK // GLA_SUB
    masks = []
    for i in range(n_sub):
        n_keys = GLA_SUB * (i + 1)
        rr = lax.broadcasted_iota(jnp.int32, (GLA_SUB, n_keys), 0) + GLA_SUB * i
        cc = lax.broadcasted_iota(jnp.int32, (GLA_SUB, n_keys), 1)
        masks.append(cc <= rr)

    def chunk_body(c, carry):
        r0 = pl.multiple_of(c * GLA_CHUNK, GLA_CHUNK)
        rows = pl.ds(r0, GLA_CHUNK)
        streams = [(s, h) for s in range(n_seq) for h in range(GLA_HEADS)]
        b_all = [_cumsum_rows(la_ref[s, rows, :]) for s in range(n_seq)]
        vals, new_states, o_inter, scores = {}, {}, {}, {}
        for s, h in streams:
            ks = slice(h * GLA_DK_PAD, (h + 1) * GLA_DK_PAD)
            vs = slice(h * GLA_DV_PAD, (h + 1) * GLA_DV_PAD)
            q = q_ref[s, rows, ks].astype(F32)
            k = k_ref[s, rows, ks].astype(F32)
            v = v_ref[s, rows, vs]
            b = b_all[s][:, ks]
            b_last = b[GLA_CHUNK - 1:GLA_CHUNK, :]
            st = st_refs[s * GLA_HEADS + h][...]
            o_inter[s, h] = _dot_nt((q * jnp.exp(b)).astype(BF16), st.astype(BF16))
            kdec = (k * jnp.exp(b_last - b)).astype(BF16)
            new_states[s, h] = st * jnp.exp(b_last) + _dot_tn(v, kdec)
            head_scores = []
            for i in range(n_sub):
                lo, hi = GLA_SUB * i, GLA_SUB * (i + 1)
                b_i = b[lo:hi, :]
                qd = q[lo:hi, :] * jnp.exp(b_i - b[lo - 1:lo, :]) if i else q[lo:hi, :] * jnp.exp(b_i)
                kd = k[:hi, :] * jnp.exp(b[lo - 1:lo, :] - b[:hi, :]) if i else k[:hi, :] * jnp.exp(-b[:hi, :])
                head_scores.append(_dot_nt(qd.astype(BF16), kd.astype(BF16)))
            scores[s, h] = head_scores
            vals[s, h] = v
        outs = {}
        for s, h in streams:
            vs = slice(h * GLA_DV_PAD, (h + 1) * GLA_DV_PAD)
            o_parts = [_dot(jnp.where(masks[i], scores[s, h][i], 0.0).astype(BF16),
                            vals[s, h][:GLA_SUB * (i + 1), :]) for i in range(n_sub)]
            o = o_inter[s, h] + jnp.concatenate(o_parts, axis=0)
            ms = jnp.sum(o * o, axis=-1, keepdims=True) * (1.0 / GLA_DV)
            y = o * lax.rsqrt(ms + EPS) * on_ref[...] * _silu(r_ref[s, rows, vs].astype(F32))
            outs[s, h] = y.astype(BF16)
        for s, h in streams:
            st_refs[s * GLA_HEADS + h][...] = new_states[s, h]
            o_ref[s, rows, h * GLA_DV_PAD:(h + 1) * GLA_DV_PAD] = outs[s, h]
        return carry

    lax.fori_loop(0, chunks, chunk_body, 0, unroll=4)


def _gla(q, k, v, r, la, onorm, bsz, seq, tm):
    t = q.shape[0]
    n_seq = 4 if bsz % 4 == 0 else 2 if bsz % 2 == 0 else 1
    kw, vw = GLA_HEADS * GLA_DK_PAD, GLA_HEADS * GLA_DV_PAD
    rows = lambda n: pl.BlockSpec((n_seq, tm, n), lambda b, i: (b, i, 0))
    per_seq = lambda a: a.reshape(bsz, seq, a.shape[-1])
    out = pl.pallas_call(
        functools.partial(_gla_kernel, tm // GLA_CHUNK, n_seq),
        grid=(bsz // n_seq, seq // tm),
        in_specs=[rows(kw), rows(kw), rows(vw), rows(vw), rows(kw),
                  pl.BlockSpec((1, GLA_DV_PAD), lambda b, i: (0, 0))],
        out_specs=rows(vw),
        out_shape=jax.ShapeDtypeStruct((bsz, seq, vw), BF16),
        scratch_shapes=[pltpu.VMEM((GLA_DV_PAD, GLA_DK_PAD), F32)] * (GLA_HEADS * n_seq),
        compiler_params=_params(2),
        name="gla_scan",
    )(per_seq(q), per_seq(k), per_seq(v), per_seq(r), per_seq(la), onorm)
    return out.reshape(t, vw)


def _fox_kernel(tile, q_ref, k_ref, v_ref, o_ref):
    i = pl.program_id(2)
    q = q_ref[...]
    qs = (q[:, :FOX_BLOCK], q[:, FOX_BLOCK:])
    blks = (slice(0, FOX_BLOCK), slice(FOX_BLOCK, 2 * FOX_BLOCK))

    def absorb(m, acc, s, v):
        m_new = jnp.maximum(m, jnp.max(s, axis=-1, keepdims=True))
        p = jnp.exp2(s - m_new).astype(BF16)
        return m_new, jnp.exp2(m - m_new) * acc + _dot(p, v)

    def step(j, state):
        keys = pl.ds(pl.multiple_of(j * tile, tile), tile)
        return tuple(absorb(*state[hh], _dot_nt(qs[hh], k_ref[keys, blks[hh]]), v_ref[keys, blks[hh]])
                     for hh in range(2))

    init = (jnp.full((tile, 1), -jnp.inf, F32), jnp.zeros((tile, FOX_BLOCK), F32))
    state = lax.fori_loop(0, i, step, (init, init))

    band = tile // FOX_DIAG_BANDS
    start = pl.multiple_of(i * tile, tile)
    bands = [(hh, r) for r in range(FOX_DIAG_BANDS) for hh in range(2)]
    band_rows = lambda r: slice(r * band, (r + 1) * band)
    band_keys = lambda r: pl.ds(start, (r + 1) * band)
    scores = {(hh, r): _dot_nt(qs[hh][band_rows(r)], k_ref[band_keys(r), blks[hh]]) for hh, r in bands}
    causal = (lax.broadcasted_iota(jnp.int32, (band, band), 1) <= lax.broadcasted_iota(jnp.int32, (band, band), 0))
    lane = lax.broadcasted_iota(jnp.int32, (band, FOX_BLOCK), 1)
    for r in range(FOX_DIAG_BANDS):
        outs = []
        for hh in range(2):
            m, acc = state[hh]
            s = scores[hh, r]
            tail = jnp.where(causal, s[:, r * band:], -jnp.inf)
            s = jnp.concatenate([s[:, :r * band], tail], axis=1) if r else tail
            _, acc = absorb(m[band_rows(r)], acc[band_rows(r)], s, v_ref[band_keys(r), blks[hh]])
            outs.append(acc / jnp.sum(jnp.where(lane == FOX_HEAD_DIM, acc, 0.0), axis=-1, keepdims=True))
        o_ref[band_rows(r), :] = jnp.where(lane < FOX_HEAD_DIM, outs[0],
                                           pltpu.roll(outs[1], FOX_HEAD_DIM, axis=1)).astype(BF16)


def _fox(q, k, v, bsz, seq, tile):
    t = q.shape[0]
    nq = seq // tile
    pairs = FOX_HEADS // 2
    return pl.pallas_call(
        functools.partial(_fox_kernel, tile),
        grid=(bsz, pairs, nq),
        in_specs=[pl.BlockSpec((tile, 2 * FOX_BLOCK), lambda b, p, i: (b * nq + i, p)),
                  pl.BlockSpec((seq, 2 * FOX_BLOCK), lambda b, p, i: (b, p)),
                  pl.BlockSpec((seq, 2 * FOX_BLOCK), lambda b, p, i: (b, p))],
        out_specs=pl.BlockSpec((tile, LANES), lambda b, p, i: (b * nq + i, p)),
        out_shape=jax.ShapeDtypeStruct((t, MAIN_WIDTH), BF16),
        compiler_params=_params(3),
        name="fox_attention",
    )(q, k, v)


def _out_kernel(x_ref, main_ref, mq_ref, mk_ref, mv_ref, wo_main_ref, wo_mem_ref, o_ref):
    mq, mk, mv = mq_ref[...], mk_ref[...], mv_ref[...]
    head = lambda a, h: a[:, h * MEM_HEAD_DIM:(h + 1) * MEM_HEAD_DIM]
    scores = [_dot_nt(head(mq, h), head(mk, h)) * (MEM_HEAD_DIM ** -0.5) for h in range(MEM_HEADS)]
    base = x_ref[...] + _dot(main_ref[...], wo_main_ref[...])
    heads = []
    for h in range(MEM_HEADS):
        s = scores[h]
        p = jnp.exp(s - jnp.max(s, axis=-1, keepdims=True))
        l = jnp.sum(p, axis=-1, keepdims=True)
        heads.append(_dot(p.astype(BF16), head(mv, h)) / l)
    mem_o = jnp.concatenate(heads, axis=-1).astype(BF16)
    o_ref[...] = base + _dot(mem_o, wo_mem_ref[...])


def _out_proj(x, main, mq, mem_k, mem_v, wo_main, wo_mem, seq, tm):
    t = x.shape[0]
    tiles_per_seq = seq // tm
    n_mem = mem_k.shape[1]
    row = lambda n: pl.BlockSpec((tm, n), lambda i: (i, 0))
    mem = pl.BlockSpec((None, n_mem, MEM_WIDTH), lambda i: (i // tiles_per_seq, 0, 0))
    return pl.pallas_call(
        _out_kernel,
        grid=(t // tm,),
        in_specs=[row(D_MODEL), row(main.shape[1]), row(MEM_WIDTH), mem, mem, _full(wo_main.shape),
                  _full(wo_mem.shape)],
        out_specs=row(D_MODEL),
        out_shape=jax.ShapeDtypeStruct((t, D_MODEL), F32),
        compiler_params=_params(1),
        name="out_proj_mem_attn",
    )(x, main, mq, mem_k, mem_v, wo_main, wo_mem)


def _swiglu_acc(hn, wg_ref, wu_ref, wd_ref, acc_ref, row_scale=None):
    for j in range(D_FF // FF_CHUNK):
        cs = slice(j * FF_CHUNK, (j + 1) * FF_CHUNK)
        a = _silu(_dot(hn, wg_ref[:, cs])) * _dot(hn, wu_ref[:, cs])
        if row_scale is not None:
            a = a * row_scale
        acc_ref[...] += _dot(a.astype(BF16), wd_ref[cs, :])


def _ffn_kernel(x_ref, g_ref, wg_ref, wu_ref, wd_ref, o_ref):
    x = x_ref[...]
    o_ref[...] = x
    _swiglu_acc(_rms_normed(x, g_ref[...]).astype(BF16), wg_ref, wu_ref, wd_ref, o_ref)


def _ffn(x, gain, wg, wu, wd, tm, riders):
    t = x.shape[0]
    steps = t // tm
    row = pl.BlockSpec((tm, D_MODEL), lambda i: (i, 0))
    held = lambda shape: pl.BlockSpec(shape, lambda i: (0, 0), pipeline_mode=pl.Buffered(1))
    rider_specs, rider_shapes = _rider_blocks(riders, steps)
    out = pl.pallas_call(
        _riding(_ffn_kernel, 5, 1, len(riders)),
        grid=(steps,),
        in_specs=[row, _full((1, D_MODEL)), held(wg.shape), held(wu.shape), held(wd.shape)] + rider_specs,
        out_specs=[row] + rider_specs,
        out_shape=[jax.ShapeDtypeStruct((t, D_MODEL), F32)] + rider_shapes,
        compiler_params=_params(1),
        name="dense_swiglu",
    )(x, gain, wg, wu, wd, *riders)
    return out[0], out[1:]


ROW_TILE = D_MODEL // LANES
ROUTE_E1, ROUTE_E2, ROUTE_W1, ROUTE_W2, ROUTE_RANK1, ROUTE_RANK2 = range(6)


def _rows_to_tiles(dst_ref, rows):
    n = rows.shape[0]
    for j in range(ROW_TILE):
        dst_ref[pl.ds(j, n, stride=ROW_TILE), :] = rows[:, j * LANES:(j + 1) * LANES]


def _tiles_to_rows(src_ref, n):
    return jnp.concatenate([src_ref[pl.ds(j, n, stride=ROW_TILE), :] for j in range(ROW_TILE)], axis=1)


def _row_tile(ref, row):
    return ref.at[pl.ds(pl.multiple_of(row * ROW_TILE, ROW_TILE), ROW_TILE)]


def _router_kernel(x_ref, g_ref, wr_ref, route_ref, fields_ref, counts_ref, carry_ref):
    @pl.when(pl.program_id(0) == 0)
    def _():
        carry_ref[...] = jnp.zeros_like(carry_ref)

    hn = _rms_normed(x_ref[...], g_ref[...])
    logits = _dot_3pass(hn, wr_ref[...]).T[:N_EXPERTS, :]
    tm = logits.shape[1]
    expert = lax.broadcasted_iota(jnp.int32, logits.shape, 0)
    m1 = jnp.max(logits, axis=0, keepdims=True)
    i1 = jnp.min(jnp.where(logits == m1, expert, N_EXPERTS), axis=0, keepdims=True)
    rest = jnp.where(expert == i1, -jnp.inf, logits)
    m2 = jnp.max(rest, axis=0, keepdims=True)
    i2 = jnp.min(jnp.where(rest == m2, expert, N_EXPERTS), axis=0, keepdims=True)
    e2 = jnp.exp(m2 - m1)
    w1 = 1.0 / (1.0 + e2)
    chosen = jnp.where((expert == i1) | (expert == i2), 1.0, 0.0)
    token = lax.broadcasted_iota(jnp.int32, logits.shape, 1)
    run = chosen
    shift = 1
    while shift < tm:
        run = run + jnp.where(token >= shift, pltpu.roll(run, shift, axis=1), 0.0)
        shift *= 2
    base = carry_ref[...][:, :1] + (run - chosen)
    rank1 = jnp.sum(jnp.where(expert == i1, base, 0.0), axis=0, keepdims=True)
    rank2 = jnp.sum(jnp.where(expert == i2, base, 0.0), axis=0, keepdims=True)
    carry_ref[...] += jnp.sum(chosen, axis=1, keepdims=True)
    counts_ref[...] = carry_ref[...]
    fields = jnp.zeros_like(logits)
    for slot, val in ((ROUTE_E1, i1.astype(F32)), (ROUTE_E2, i2.astype(F32)), (ROUTE_W1, w1), (ROUTE_W2, e2 * w1),
                      (ROUTE_RANK1, rank1), (ROUTE_RANK2, rank2)):
        fields = jnp.where(expert == slot, val, fields)
    fields_ref[...] = fields
    route_ref[...] = jnp.concatenate([fields, jnp.zeros((LANES - N_EXPERTS, tm), F32)], axis=0).T


def _router(x, gain, wr, tm):
    t = x.shape[0]
    row = lambda n: pl.BlockSpec((tm, n), lambda i: (i, 0))
    n_fields = N_EXPERTS
    return pl.pallas_call(
        _router_kernel,
        grid=(t // tm,),
        in_specs=[row(D_MODEL), _full((1, D_MODEL)), _full(wr.shape)],
        out_specs=[row(LANES), pl.BlockSpec((None, n_fields, tm), lambda i: (i, 0, 0)), _full((N_EXPERTS, LANES))],
        out_shape=[jax.ShapeDtypeStruct((t, LANES), F32), jax.ShapeDtypeStruct((t // tm, n_fields, tm), F32),
                   jax.ShapeDtypeStruct((N_EXPERTS, LANES), F32)],
        scratch_shapes=[pltpu.VMEM((N_EXPERTS, LANES), F32)],
        compiler_params=_params(1),
        name="moe_router",
    )(x, gain, wr)


def _dispatch_kernel(tm, cnt_ref, pad_ref, off_ref, pos1_ref, pos2_ref, x_ref, g_ref, xs_ref, stage_ref, zero_ref,
                     sem, pad_sem):
    @pl.when(pl.program_id(0) == 0)
    def _():
        zero_ref[...] = jnp.zeros_like(zero_ref)
        for e in range(N_EXPERTS + 1):
            lo, hi = off_ref[e] + cnt_ref[e], off_ref[e] + pad_ref[e]

            def fill(r, c):
                pltpu.make_async_copy(zero_ref, _row_tile(xs_ref, r), pad_sem).start()
                return c

            def drain(r, c):
                pltpu.make_async_copy(zero_ref, _row_tile(xs_ref, r), pad_sem).wait()
                return c

            lax.fori_loop(lo, hi, fill, 0)
            lax.fori_loop(lo, hi, drain, 0)

    _rows_to_tiles(stage_ref, _rms_normed(x_ref[...], g_ref[...]))

    def issue(r, c):
        src = _row_tile(stage_ref, r)
        pltpu.make_async_copy(src, _row_tile(xs_ref, pos1_ref[0, r]), sem).start(priority=0)
        pltpu.make_async_copy(src, _row_tile(xs_ref, pos2_ref[0, r]), sem).start(priority=1)
        return c

    lax.fori_loop(0, tm, issue, 0, unroll=8)
    for _ in range(2):
        pltpu.make_async_copy(stage_ref, xs_ref.at[pl.ds(0, tm * ROW_TILE)], sem).wait()


def _dispatch(x, gain, pos1, pos2, counts, padded, offsets, n_rows, tm):
    t = x.shape[0]
    nt = t // tm
    smem_row = pl.BlockSpec((None, 1, tm), lambda i, *_: (i, 0, 0), memory_space=pltpu.SMEM)
    return pl.pallas_call(
        functools.partial(_dispatch_kernel, tm),
        grid_spec=pltpu.PrefetchScalarGridSpec(
            num_scalar_prefetch=3,
            grid=(nt,),
            in_specs=[smem_row, smem_row, pl.BlockSpec((tm, D_MODEL), lambda i, *_: (i, 0)),
                      pl.BlockSpec((1, D_MODEL), lambda i, *_: (0, 0))],
            out_specs=pl.BlockSpec(memory_space=pl.ANY),
            scratch_shapes=[pltpu.VMEM((tm * ROW_TILE, LANES), F32), pltpu.VMEM((ROW_TILE, LANES), F32),
                            pltpu.SemaphoreType.DMA, pltpu.SemaphoreType.DMA],
        ),
        out_shape=jax.ShapeDtypeStruct((n_rows * ROW_TILE, LANES), F32),
        compiler_params=_params(1),
        name="moe_dispatch",
    )(counts, padded, offsets, pos1.reshape(nt, 1, tm), pos2.reshape(nt, 1, tm), x, gain)


def _experts_kernel(tm, tile_expert_ref, n_used_ref, xs_ref, wg_ref, wu_ref, wd_ref, o_ref, acc_ref):
    acc_ref[...] = jnp.zeros_like(acc_ref)

    @pl.when(pl.program_id(0) < n_used_ref[0])
    def _():
        _swiglu_acc(_tiles_to_rows(xs_ref, tm).astype(BF16), wg_ref, wu_ref, wd_ref, acc_ref)

    _rows_to_tiles(o_ref, acc_ref[...])


def _experts(xs, tile_expert, n_used, wg, wu, wd, tm):
    n_tiles = xs.shape[0] // (tm * ROW_TILE)
    wspec = lambda shape: pl.BlockSpec(
        (None,) + shape, lambda i, te, nu: (te[jnp.minimum(i, nu[0] - 1)], 0, 0))
    rows = pl.BlockSpec((tm * ROW_TILE, LANES), lambda i, te, nu: (i, 0))
    return pl.pallas_call(
        functools.partial(_experts_kernel, tm),
        grid_spec=pltpu.PrefetchScalarGridSpec(
            num_scalar_prefetch=2,
            grid=(n_tiles,),
            in_specs=[rows, wspec((D_MODEL, D_FF)), wspec((D_MODEL, D_FF)), wspec((D_FF, D_MODEL))],
            out_specs=rows,
            scratch_shapes=[pltpu.VMEM((tm, D_MODEL), F32)],
        ),
        out_shape=jax.ShapeDtypeStruct(xs.shape, F32),
        compiler_params=_params(1),
        name="moe_experts",
    )(tile_expert, n_used, xs, wg, wu, wd)


def _combine_kernel(tm, n_tiles, pos1_ref, pos2_ref, next1_ref, next2_ref, x_ref, route_ref, gf_ref, y_ref,
                    o_ref, buf_ref, sems):
    i = pl.program_id(0)
    slot = i % 2

    def gather(p1_ref, p2_ref, slot):
        def issue(r, c):
            pltpu.make_async_copy(_row_tile(y_ref, p1_ref[0, r]), _row_tile(buf_ref.at[slot, 0], r),
                                  sems.at[slot]).start(priority=0)
            pltpu.make_async_copy(_row_tile(y_ref, p2_ref[0, r]), _row_tile(buf_ref.at[slot, 1], r),
                                  sems.at[slot]).start(priority=1)
            return c

        lax.fori_loop(0, tm, issue, 0, unroll=8)

    @pl.when(i == 0)
    def _():
        gather(pos1_ref, pos2_ref, 0)

    @pl.when(i + 1 < n_tiles)
    def _():
        gather(next1_ref, next2_ref, 1 - slot)

    route = route_ref[...]
    lane = lax.broadcasted_iota(jnp.int32, route.shape, 1)
    w1 = jnp.sum(jnp.where(lane == ROUTE_W1, route, 0.0), axis=-1, keepdims=True)
    w2 = jnp.sum(jnp.where(lane == ROUTE_W2, route, 0.0), axis=-1, keepdims=True)
    for k in range(2):
        pltpu.make_async_copy(y_ref.at[pl.ds(0, tm * ROW_TILE)], buf_ref.at[slot, k], sems.at[slot]).wait()
    y1, y2 = (_tiles_to_rows(buf_ref.at[slot, k], tm) for k in range(2))
    o_ref[...] = _rms_normed(x_ref[...] + w1 * y1 + w2 * y2, gf_ref[...])


def _combine(x, route, pos1, pos2, y, gain_final, tm):
    t = x.shape[0]
    nt = t // tm
    smem_row = lambda ahead: pl.BlockSpec((None, 1, tm), lambda i: (jnp.minimum(i + ahead, nt - 1), 0, 0),
                                          memory_space=pltpu.SMEM)
    row = lambda n: pl.BlockSpec((tm, n), lambda i: (i, 0))
    pos1, pos2 = pos1.reshape(nt, 1, tm), pos2.reshape(nt, 1, tm)
    return pl.pallas_call(
        functools.partial(_combine_kernel, tm, nt),
        grid=(nt,),
        in_specs=[smem_row(0), smem_row(0), smem_row(1), smem_row(1), row(D_MODEL), row(LANES),
                  _full((1, D_MODEL)), pl.BlockSpec(memory_space=pl.ANY)],
        out_specs=row(D_MODEL),
        out_shape=jax.ShapeDtypeStruct((t, D_MODEL), F32),
        scratch_shapes=[pltpu.VMEM((2, 2, tm * ROW_TILE, LANES), F32), pltpu.SemaphoreType.DMA((2,))],
        compiler_params=_params(1),
        name="moe_combine",
    )(pos1, pos2, pos1, pos2, x, route, gain_final, y)


def _moe(x, gain, gain_final, wr, wg, wu, wd, tm):
    t = x.shape[0]
    route, fields, counts = _router(x, gain, wr, tm)
    col = lambda c: fields[:, c, :].reshape(t).astype(jnp.int32)
    counts = counts[:, 0].astype(jnp.int32)
    padded = (counts + tm - 1) // tm * tm
    ends = jnp.cumsum(padded)
    offsets = ends - padded
    pos1 = offsets[col(ROUTE_E1)] + col(ROUTE_RANK1)
    pos2 = offsets[col(ROUTE_E2)] + col(ROUTE_RANK2)
    n_rows = 2 * t + N_EXPERTS * tm
    tile_start = jnp.arange(n_rows // tm, dtype=jnp.int32) * tm
    tile_expert = jnp.minimum(jnp.sum(tile_start[:, None] >= ends[None, :], axis=1), N_EXPERTS - 1).astype(jnp.int32)
    n_used = (ends[-1:] // tm).astype(jnp.int32)
    tail = lambda a, v: jnp.concatenate([a, v.astype(jnp.int32)])
    xs = _dispatch(x, gain, pos1, pos2, tail(counts, jnp.zeros((1,))), tail(padded, n_rows - ends[-1:]),
                   tail(offsets, ends[-1:]), n_rows, tm)
    y = _experts(xs, tile_expert, n_used, wg, wu, wd, tm)
    return _combine(x, route, pos1, pos2, y, gain_final, tm)


def _pad_heads(w, heads, width, padded):
    lead = w.shape[:-1]
    w = w.reshape(lead + (heads, width))
    w = jnp.pad(w, [(0, 0)] * len(lead) + [(0, 0), (0, padded - width)])
    return w.reshape(lead + (heads * padded,))


def _pad_cols(w, n):
    return jnp.pad(w, [(0, 0)] * (w.ndim - 1) + [(0, n - w.shape[-1])])


def kernel(x, mem, norm_mix, norm_mem, norm_ffn, norm_kv, norm_final, w_in_a, w_gla_gate2, b_gla_gate2,
           gla_onorm, w_in_b, w_kv, b_forget, w_mem_kv, w_out, w_ff_gate, w_ff_up, w_ff_down, w_router,
           w_moe_gate, w_moe_up, w_moe_down):
    bsz, seq, _ = x.shape
    n_mem = mem.shape[1]
    t = bsz * seq
    tm = 512
    xt = x.reshape(t, D_MODEL)
    memt = mem.reshape(bsz * n_mem, D_MODEL)
    gain = lambda g: g.reshape(1, D_MODEL)
    kw = GLA_HEADS * GLA_DK

    def mem_kv(layer):
        wk, wv = w_mem_kv[layer, :, :MEM_WIDTH], w_mem_kv[layer, :, MEM_WIDTH:]
        mk, mv = _norm_proj(memt, gain(norm_mem[layer]), [wk.astype(BF16), wv.astype(BF16)], (1.0, 1.0),
                            n_mem, "mem_kv_proj")
        return mk.reshape(bsz, n_mem, MEM_WIDTH), mv.reshape(bsz, n_mem, MEM_WIDTH)

    wa = w_in_a[0]
    wq = _pad_heads(wa[:, :kw], GLA_HEADS, GLA_DK, GLA_DK_PAD).astype(BF16)
    wk = _pad_heads(wa[:, kw:2 * kw], GLA_HEADS, GLA_DK, GLA_DK_PAD).astype(BF16)
    off = 2 * kw
    wv = _pad_heads(wa[:, off:off + MAIN_WIDTH], GLA_HEADS, GLA_DV, GLA_DV_PAD).astype(BF16)
    off += MAIN_WIDTH
    wr = _pad_heads(wa[:, off:off + MAIN_WIDTH], GLA_HEADS, GLA_DV, GLA_DV_PAD).astype(BF16)
    off += MAIN_WIDTH
    wg1 = _pad_cols(wa[:, off:off + GLA_GATE_RANK], LANES).astype(BF16)
    off += GLA_GATE_RANK
    wmq = wa[:, off:off + MEM_WIDTH].astype(BF16)
    wg2 = jnp.pad(_pad_heads(w_gla_gate2[0], GLA_HEADS, GLA_DK, GLA_DK_PAD), ((0, LANES - GLA_GATE_RANK), (0, 0)))
    bg2 = _pad_heads(b_gla_gate2[0], GLA_HEADS, GLA_DK, GLA_DK_PAD).reshape(1, -1)
    onorm = _pad_cols(gla_onorm[0], GLA_DV_PAD).reshape(1, GLA_DV_PAD)

    (q, k, v, r, mq, la), dense_weights = _in_a(xt, gain(norm_mix[0]), wq, wk, wv, wr, wmq, wg1, wg2, bg2, 2 * tm,
                                                [w_ff_gate[0], w_ff_up[0]])
    main = _gla(q, k, v, r, la, onorm, bsz, seq, tm)
    mk, mv = mem_kv(0)
    wo = w_out[0]
    wo_main = jnp.pad(wo[:MAIN_WIDTH].reshape(GLA_HEADS, GLA_DV, D_MODEL),
                      ((0, 0), (0, GLA_DV_PAD - GLA_DV), (0, 0))).reshape(GLA_HEADS * GLA_DV_PAD, D_MODEL)
    xt = _out_proj(xt, main, mq, mk, mv, wo_main.astype(BF16), wo[MAIN_WIDTH:].astype(BF16), seq, 2 * tm)
    expert_weights = [w_moe_gate[0].reshape(N_EXPERTS * D_MODEL, D_FF), w_moe_up[0].reshape(N_EXPERTS * D_MODEL, D_FF),
                      w_moe_down[0].reshape(N_EXPERTS * D_FF, D_MODEL)]
    xt, expert_weights = _ffn(xt, gain(norm_ffn[0]), dense_weights[0], dense_weights[1], w_ff_down[0].astype(BF16),
                              tm, expert_weights)
    wg, wu, wd = (w.reshape(N_EXPERTS, -1, w.shape[-1]) for w in expert_weights)

    wf = _pad_cols(w_kv[:, 2 * MAIN_WIDTH:], LANES).astype(BF16)
    bf = _pad_cols(b_forget, LANES).reshape(1, LANES)
    k_sh, v_sh, c = _kv_proj(xt, gain(norm_kv), w_kv[:, :MAIN_WIDTH].astype(BF16),
                             w_kv[:, MAIN_WIDTH:2 * MAIN_WIDTH].astype(BF16), wf, bf, seq, tm)

    wb = w_in_b[0]
    qf, mq = _in_b(xt, gain(norm_mix[1]), c, wb[:, :MAIN_WIDTH].astype(BF16), wb[:, MAIN_WIDTH:].astype(BF16),
                   2 * tm)
    main = _fox(qf, k_sh, v_sh, bsz, seq, 1024)
    mk, mv = mem_kv(1)
    wo = w_out[1]
    xt = _out_proj(xt, main, mq, mk, mv, wo[:MAIN_WIDTH].astype(BF16), wo[MAIN_WIDTH:].astype(BF16), seq, 2 * tm)
    out = _moe(xt, gain(norm_ffn[1]), gain(norm_final), _pad_cols(w_router[0], LANES), wg, wu, wd, tm)
    return out.reshape(bsz, seq, D_MODEL)
```

```python
import functools

import jax
import jax.numpy as jnp
import numpy as np
from jax import lax
from jax.experimental import pallas as pl
from jax.experimental.pallas import tpu as pltpu

D_MODEL = 1024
EPS = 1e-6
GLA_CHUNK = 64
GLA_SUB = 16
MEM_HEADS = 4
MEM_HEAD_DIM = 64
MEM_WIDTH = MEM_HEADS * MEM_HEAD_DIM
MAIN_WIDTH = D_MODEL - MEM_WIDTH
GLA_HEADS = 4
GLA_DK = MAIN_WIDTH // 2 // GLA_HEADS
GLA_DV = MAIN_WIDTH // GLA_HEADS
GLA_DK_PAD = 128
GLA_DV_PAD = 256
GLA_GATE_RANK = 16
GLA_GATE_TAU = 16.0
FOX_HEADS = 12
FOX_HEAD_DIM = 64
D_FF = 2816
N_EXPERTS = 8
LANES = 128
FF_CHUNK = 256
VMEM_LIMIT = 56 * 1024 * 1024

BF16 = jnp.bfloat16
F32 = jnp.float32


def _params(n_axes, vmem=VMEM_LIMIT):
    return pltpu.CompilerParams(dimension_semantics=("arbitrary",) * n_axes, vmem_limit_bytes=vmem)


def _rms_normed(x, gain):
    ms = jnp.mean(x * x, axis=-1, keepdims=True)
    return x * lax.rsqrt(ms + EPS) * gain


def _log_sigmoid(z):
    return jnp.minimum(z, 0.0) - jnp.log(1.0 + jnp.exp(-jnp.abs(z)))


def _silu(z):
    return z / (1.0 + jnp.exp(-z))


def _dot(a, b):
    return jnp.dot(a, b, preferred_element_type=F32)


def _dot_nt(a, b):
    return lax.dot_general(a, b, (((1,), (1,)), ((), ())), preferred_element_type=F32)


def _dot_tn(a, b):
    return lax.dot_general(a, b, (((0,), (0,)), ((), ())), preferred_element_type=F32)


def _bf16_terms(v):
    hi = v.astype(BF16).astype(F32)
    mid = (v - hi).astype(BF16).astype(F32)
    return hi, mid, v - hi - mid


def _dot_3pass(a, b):
    a_hi, a_lo, _ = _bf16_terms(a)
    b_hi, b_lo, _ = _bf16_terms(b)
    a_hi, a_lo, b_hi, b_lo = (t.astype(BF16) for t in (a_hi, a_lo, b_hi, b_lo))
    return _dot(a_hi, b_hi) + _dot(a_hi, b_lo) + _dot(a_lo, b_hi)


def _cumsum_rows(x):
    n = x.shape[0]
    r = lax.broadcasted_iota(jnp.int32, (n, n), 0)
    c = lax.broadcasted_iota(jnp.int32, (n, n), 1)
    tril = jnp.where(c <= r, 1.0, 0.0).astype(BF16)
    hi, mid, lo = _bf16_terms(x)
    return _dot(tril, hi.astype(BF16)) + _dot(tril, mid.astype(BF16)) + _dot(tril, lo.astype(BF16))


def _full(shape):
    return pl.BlockSpec(shape, lambda *_: (0,) * len(shape))


BF16_SUBLANES = 16


def _riding(kernel_fn, n_in, n_out, n_riders):
    def body(*refs):
        ins, rest = refs[:n_in], refs[n_in:]
        rider_in, rest = rest[:n_riders], rest[n_riders:]
        outs, rest = rest[:n_out], rest[n_out:]
        rider_out, scratch = rest[:n_riders], rest[n_riders:]
        for src_ref, dst_ref in zip(rider_in, rider_out):
            dst_ref[...] = src_ref[...].astype(BF16)
        kernel_fn(*ins, *outs, *scratch)
    return body


def _rider_blocks(riders, steps):
    for a in riders:
        assert a.shape[0] % (steps * BF16_SUBLANES) == 0, (a.shape, steps)
    specs = [pl.BlockSpec((a.shape[0] // steps, a.shape[1]), lambda i: (i, 0)) for a in riders]
    return specs, [jax.ShapeDtypeStruct(a.shape, BF16) for a in riders]


def _in_a_kernel(x_ref, g_ref, wq_ref, wk_ref, wv_ref, wr_ref, wmq_ref, wg1_ref, wg2_ref, bg2_ref,
                 q_ref, k_ref, v_ref, r_ref, mq_ref, la_ref):
    hn = _rms_normed(x_ref[...], g_ref[...]).astype(BF16)
    q_ref[...] = (_dot(hn, wq_ref[...]) * (GLA_DK ** -0.5)).astype(BF16)
    k_ref[...] = _dot(hn, wk_ref[...]).astype(BF16)
    v_ref[...] = _dot(hn, wv_ref[...]).astype(BF16)
    r_ref[...] = _dot(hn, wr_ref[...]).astype(BF16)
    mq_ref[...] = _dot(hn, wmq_ref[...]).astype(BF16)
    glr = _dot(hn, wg1_ref[...])
    z = _dot_3pass(glr, wg2_ref[...])
    la_ref[...] = _log_sigmoid(z + bg2_ref[...]) * (1.0 / GLA_GATE_TAU)


def _in_a(x, gain, wq, wk, wv, wr, wmq, wg1, wg2, bg2, tm, riders):
    t = x.shape[0]
    steps = t // tm
    row = lambda n: pl.BlockSpec((tm, n), lambda i: (i, 0))
    kw, vw = GLA_HEADS * GLA_DK_PAD, GLA_HEADS * GLA_DV_PAD
    rider_specs, rider_shapes = _rider_blocks(riders, steps)
    out = pl.pallas_call(
        _riding(_in_a_kernel, 10, 6, len(riders)),
        grid=(steps,),
        in_specs=[row(D_MODEL), _full((1, D_MODEL)), _full(wq.shape), _full(wk.shape), _full(wv.shape),
                  _full(wr.shape), _full(wmq.shape), _full(wg1.shape), _full(wg2.shape), _full(bg2.shape)]
                 + rider_specs,
        out_specs=[row(kw), row(kw), row(vw), row(vw), row(MEM_WIDTH), row(kw)] + rider_specs,
        out_shape=[jax.ShapeDtypeStruct((t, kw), BF16), jax.ShapeDtypeStruct((t, kw), BF16),
                   jax.ShapeDtypeStruct((t, vw), BF16), jax.ShapeDtypeStruct((t, vw), BF16),
                   jax.ShapeDtypeStruct((t, MEM_WIDTH), BF16), jax.ShapeDtypeStruct((t, kw), F32)] + rider_shapes,
        compiler_params=_params(1),
        name="in_proj_gla",
    )(x, gain, wq, wk, wv, wr, wmq, wg1, wg2, bg2, *riders)
    return out[:6], out[6:]


def _norm_proj_kernel(n_out, scales, x_ref, g_ref, *refs):
    w_refs, o_refs = refs[:n_out], refs[n_out:]
    hn = _rms_normed(x_ref[...], g_ref[...]).astype(BF16)
    for w_ref, o_ref, s in zip(w_refs, o_refs, scales):
        o = _dot(hn, w_ref[...])
        if s != 1.0:
            o = o * s
        o_ref[...] = o.astype(o_ref.dtype)


def _norm_proj(x, gain, weights, scales, tm, name):
    t = x.shape[0]
    row = lambda n: pl.BlockSpec((tm, n), lambda i: (i, 0))
    return pl.pallas_call(
        functools.partial(_norm_proj_kernel, len(weights), scales),
        grid=(t // tm,),
        in_specs=[row(D_MODEL), _full((1, D_MODEL))] + [_full(w.shape) for w in weights],
        out_specs=[row(w.shape[1]) for w in weights],
        out_shape=[jax.ShapeDtypeStruct((t, w.shape[1]), BF16) for w in weights],
        compiler_params=_params(1),
        name=name,
    )(x, gain, *weights)


FOX_BLOCK = LANES
FOX_WIDTH = FOX_HEADS * FOX_BLOCK
LOG2E = 1.4426950408889634
FOX_DIAG_BANDS = 4


FOX_ONE_LANE = 3 * FOX_HEADS


def _fox_bias_matrix(query_side):
    e = np.zeros((LANES, FOX_WIDTH), np.float32)
    for h in range(FOX_HEADS):
        base = h * FOX_BLOCK + FOX_HEAD_DIM
        for n in range(3):
            if query_side:
                e[n * FOX_HEADS + h, base + n] = 1.0
                e[FOX_ONE_LANE, base + 3 + n] = 1.0
            else:
                e[FOX_ONE_LANE, base + n] = 1.0
                e[n * FOX_HEADS + h, base + 3 + n] = -1.0
    return jnp.asarray(e, BF16)


def _pack_head_terms(x, fill):
    hi, mid, lo = _bf16_terms(x)
    lane = lax.broadcasted_iota(jnp.int32, x.shape, 1)
    packed = fill(lane)
    for n, term in ((2, lo), (1, mid), (0, hi)):
        shifted = pltpu.roll(term, n * FOX_HEADS, axis=1) if n else term
        packed = jnp.where((lane >= n * FOX_HEADS) & (lane < (n + 1) * FOX_HEADS), shifted, packed)
    return packed.astype(BF16)


def _cumsum_heads(x):
    n = x.shape[0]
    r = lax.broadcasted_iota(jnp.int32, (n, n), 0)
    c = lax.broadcasted_iota(jnp.int32, (n, n), 1)
    tril = jnp.where(c <= r, 1.0, 0.0).astype(BF16)
    sums = _dot(tril, _pack_head_terms(x, lambda lane: jnp.zeros(lane.shape, F32)))
    total = sums + pltpu.roll(sums, LANES - FOX_HEADS, axis=1) + pltpu.roll(sums, LANES - 2 * FOX_HEADS, axis=1)
    lane = lax.broadcasted_iota(jnp.int32, x.shape, 1)
    return jnp.where(lane < FOX_HEADS, total, 0.0)


def _fox_bias_lanes(c, place_ref):
    packed = _pack_head_terms(c * LOG2E, lambda lane: jnp.where(lane == FOX_ONE_LANE, 1.0, 0.0))
    placed = _dot(packed, place_ref[...])
    return [placed[:, h * FOX_BLOCK:(h + 1) * FOX_BLOCK] for h in range(FOX_HEADS)]


def _fox_blocks(dense, spare):
    lane = lax.broadcasted_iota(jnp.int32, (dense.shape[0], FOX_BLOCK), 1)
    blocks = []
    for h in range(FOX_HEADS):
        pair = dense[:, (h // 2) * FOX_BLOCK:(h // 2 + 1) * FOX_BLOCK]
        own = pair if h % 2 == 0 else pltpu.roll(pair, FOX_HEAD_DIM, axis=1)
        blocks.append(jnp.where(lane < FOX_HEAD_DIM, own, spare[h]))
    return jnp.concatenate(blocks, axis=-1)


def _kv_kernel(tiles_per_seq, x_ref, g_ref, wk_ref, wv_ref, wf_ref, bf_ref, place_ref, k_ref, v_ref, c_ref,
               carry_ref):
    @pl.when(pl.program_id(0) % tiles_per_seq == 0)
    def _():
        carry_ref[...] = jnp.zeros_like(carry_ref)

    hn = _rms_normed(x_ref[...], g_ref[...]).astype(BF16)
    log_f = _log_sigmoid(_dot(hn, wf_ref[...]) + bf_ref[...])
    tm = log_f.shape[0]
    c = carry_ref[...] + _cumsum_heads(log_f)
    carry_ref[...] = c[tm - 1:tm, :]
    c_ref[...] = c
    k_ref[...] = _fox_blocks(_dot(hn, wk_ref[...]), _fox_bias_lanes(c, place_ref)).astype(BF16)
    lane = lax.broadcasted_iota(jnp.int32, (tm, FOX_BLOCK), 1)
    row_sum_lane = jnp.where(lane == FOX_HEAD_DIM, 1.0, 0.0)
    v_ref[...] = _fox_blocks(_dot(hn, wv_ref[...]), [row_sum_lane] * FOX_HEADS).astype(BF16)


def _kv_proj(x, gain, wk, wv, wf, bf, seq, tm):
    t = x.shape[0]
    place = _fox_bias_matrix(query_side=False)
    row = lambda n: pl.BlockSpec((tm, n), lambda i: (i, 0))
    return pl.pallas_call(
        functools.partial(_kv_kernel, seq // tm),
        grid=(t // tm,),
        in_specs=[row(D_MODEL), _full((1, D_MODEL)), _full(wk.shape), _full(wv.shape), _full(wf.shape),
                  _full(bf.shape), _full(place.shape)],
        out_specs=[row(FOX_WIDTH), row(FOX_WIDTH), row(LANES)],
        out_shape=[jax.ShapeDtypeStruct((t, FOX_WIDTH), BF16), jax.ShapeDtypeStruct((t, FOX_WIDTH), BF16),
                   jax.ShapeDtypeStruct((t, LANES), F32)],
        scratch_shapes=[pltpu.VMEM((1, LANES), F32)],
        compiler_params=_params(1),
        name="kv_proj_fox",
    )(x, gain, wk, wv, wf, bf, place)

def _in_b_kernel(x_ref, g_ref, c_ref, wq_ref, wmq_ref, place_ref, q_ref, mq_ref):
    hn = _rms_normed(x_ref[...], g_ref[...]).astype(BF16)
    q = _dot(hn, wq_ref[...]) * (FOX_HEAD_DIM ** -0.5 * LOG2E)
    q_ref[...] = _fox_blocks(q, _fox_bias_lanes(c_ref[...], place_ref)).astype(BF16)
    mq_ref[...] = _dot(hn, wmq_ref[...]).astype(BF16)


def _in_b(x, gain, c, wq, wmq, tm):
    t = x.shape[0]
    place = _fox_bias_matrix(query_side=True)
    row = lambda n: pl.BlockSpec((tm, n), lambda i: (i, 0))
    return pl.pallas_call(
        _in_b_kernel,
        grid=(t // tm,),
        in_specs=[row(D_MODEL), _full((1, D_MODEL)), row(LANES), _full(wq.shape), _full(wmq.shape),
                  _full(place.shape)],
        out_specs=[row(FOX_WIDTH), row(MEM_WIDTH)],
        out_shape=[jax.ShapeDtypeStruct((t, FOX_WIDTH), BF16), jax.ShapeDtypeStruct((t, MEM_WIDTH), BF16)],
        compiler_params=_params(1),
        name="in_proj_fox",
    )(x, gain, c, wq, wmq, place)


def _gla_kernel(chunks, n_seq, q_ref, k_ref, v_ref, r_ref, la_ref, on_ref, o_ref, *st_refs):
    @pl.when(pl.program_id(1) == 0)
    def _():
        for st_ref in st_refs:
            st_ref[...] = jnp.zeros_like(st_ref)

    n_sub = GLA_CHUNK // GLA_SUB
    masks = []
    for i in range(n_sub):
        n_keys = GLA_SUB * (i + 1)
        rr = lax.broadcasted_iota(jnp.int32, (GLA_SUB, n_keys), 0) + GLA_SUB * i
        cc = lax.broadcasted_iota(jnp.int32, (GLA_SUB, n_keys), 1)
        masks.append(cc <= rr)

    def chunk_body(c, carry):
        r0 = pl.multiple_of(c * GLA_CHUNK, GLA_CHUNK)
        rows = pl.ds(r0, GLA_CHUNK)
        streams = [(s, h) for s in range(n_seq) for h in range(GLA_HEADS)]
        b_all = [_cumsum_rows(la_ref[s, rows, :]) for s in range(n_seq)]
        vals, new_states, o_inter, scores = {}, {}, {}, {}
        for s, h in streams:
            ks = slice(h * GLA_DK_PAD, (h + 1) * GLA_DK_PAD)
            vs = slice(h * GLA_DV_PAD, (h + 1) * GLA_DV_PAD)
            q = q_ref[s, rows, ks].astype(F32)
            k = k_ref[s, rows, ks].astype(F32)
            v = v_ref[s, rows, vs]
            b = b_all[s][:, ks]
            b_last = b[GLA_CHUNK - 1:GLA_CHUNK, :]
            st = st_refs[s * GLA_HEADS + h][...]
            o_inter[s, h] = _dot_nt((q * jnp.exp(b)).astype(BF16), st.astype(BF16))
            kdec = (k * jnp.exp(b_last - b)).astype(BF16)
            new_states[s, h] = st * jnp.exp(b_last) + _dot_tn(v, kdec)
            head_scores = []
            for i in range(n_sub):
                lo, hi = GLA_SUB * i, GLA_SUB * (i + 1)
                b_i = b[lo:hi, :]
                qd = q[lo:hi, :] * jnp.exp(b_i - b[lo - 1:lo, :]) if i else q[lo:hi, :] * jnp.exp(b_i)
                kd = k[:hi, :] * jnp.exp(b[lo - 1:lo, :] - b[:hi, :]) if i else k[:hi, :] * jnp.exp(-b[:hi, :])
                head_scores.append(_dot_nt(qd.astype(BF16), kd.astype(BF16)))
            scores[s, h] = head_scores
            vals[s, h] = v
        outs = {}
        for s, h in streams:
            vs = slice(h * GLA_DV_PAD, (h + 1) * GLA_DV_PAD)
            o_parts = [_dot(jnp.where(masks[i], scores[s, h][i], 0.0).astype(BF16),
                            vals[s, h][:GLA_SUB * (i + 1), :]) for i in range(n_sub)]
            o = o_inter[s, h] + jnp.concatenate(o_parts, axis=0)
            ms = jnp.sum(o * o, axis=-1, keepdims=True) * (1.0 / GLA_DV)
            y = o * lax.rsqrt(ms + EPS) * on_ref[...] * _silu(r_ref[s, rows, vs].astype(F32))
            outs[s, h] = y.astype(BF16)
        for s, h in streams:
            st_refs[s * GLA_HEADS + h][...] = new_states[s, h]
            o_ref[s, rows, h * GLA_DV_PAD:(h + 1) * GLA_DV_PAD] = outs[s, h]
        return carry

    lax.fori_loop(0, chunks, chunk_body, 0, unroll=4)


def _gla(q, k, v, r, la, onorm, bsz, seq, tm):
    t = q.shape[0]
    n_seq = 4 if bsz % 4 == 0 else 2 if bsz % 2 == 0 else 1
    kw, vw = GLA_HEADS * GLA_DK_PAD, GLA_HEADS * GLA_DV_PAD
    rows = lambda n: pl.BlockSpec((n_seq, tm, n), lambda b, i: (b, i, 0))
    per_seq = lambda a: a.reshape(bsz, seq, a.shape[-1])
    out = pl.pallas_call(
        functools.partial(_gla_kernel, tm // GLA_CHUNK, n_seq),
        grid=(bsz // n_seq, seq // tm),
        in_specs=[rows(kw), rows(kw), rows(vw), rows(vw), rows(kw),
                  pl.BlockSpec((1, GLA_DV_PAD), lambda b, i: (0, 0))],
        out_specs=rows(vw),
        out_shape=jax.ShapeDtypeStruct((bsz, seq, vw), BF16),
        scratch_shapes=[pltpu.VMEM((GLA_DV_PAD, GLA_DK_PAD), F32)] * (GLA_HEADS * n_seq),
        compiler_params=_params(2),
        name="gla_scan",
    )(per_seq(q), per_seq(k), per_seq(v), per_seq(r), per_seq(la), onorm)
    return out.reshape(t, vw)


def _fox_kernel(tile, q_ref, k_ref, v_ref, o_ref):
    i = pl.program_id(2)
    q = q_ref[...]
    qs = (q[:, :FOX_BLOCK], q[:, FOX_BLOCK:])
    blks = (slice(0, FOX_BLOCK), slice(FOX_BLOCK, 2 * FOX_BLOCK))

    def absorb(state, s, v):
        if state is None:
            m_new = jnp.max(s, axis=-1, keepdims=True)
            return m_new, _dot(jnp.exp2(s - m_new).astype(BF16), v)
        m, acc = state
        m_new = jnp.maximum(m, jnp.max(s, axis=-1, keepdims=True))
        p = jnp.exp2(s - m_new).astype(BF16)
        return m_new, jnp.exp2(m - m_new) * acc + _dot(p, v)

    def step(j, state):
        keys = pl.ds(pl.multiple_of(j * tile, tile), tile)
        return tuple(absorb(state[hh], _dot_nt(qs[hh], k_ref[keys, blks[hh]]), v_ref[keys, blks[hh]])
                     for hh in range(2))

    band = tile // FOX_DIAG_BANDS
    start = pl.multiple_of(i * tile, tile)
    bands = [(hh, r) for r in range(FOX_DIAG_BANDS) for hh in range(2)]
    band_rows = lambda r: slice(r * band, (r + 1) * band)
    band_keys = lambda r: pl.ds(start, (r + 1) * band)

    def diagonal(state):
        scores = {(hh, r): _dot_nt(qs[hh][band_rows(r)], k_ref[band_keys(r), blks[hh]]) for hh, r in bands}
        causal = (lax.broadcasted_iota(jnp.int32, (band, band), 1)
                  <= lax.broadcasted_iota(jnp.int32, (band, band), 0))
        lane = lax.broadcasted_iota(jnp.int32, (band, FOX_BLOCK), 1)
        for r in range(FOX_DIAG_BANDS):
            outs = []
            for hh in range(2):
                s = scores[hh, r]
                tail = jnp.where(causal, s[:, r * band:], -jnp.inf)
                s = jnp.concatenate([s[:, :r * band], tail], axis=1) if r else tail
                before = None if state is None else tuple(a[band_rows(r)] for a in state[hh])
                _, acc = absorb(before, s, v_ref[band_keys(r), blks[hh]])
                outs.append(acc / jnp.sum(jnp.where(lane == FOX_HEAD_DIM, acc, 0.0), axis=-1, keepdims=True))
            o_ref[band_rows(r), :] = jnp.where(lane < FOX_HEAD_DIM, outs[0],
                                               pltpu.roll(outs[1], FOX_HEAD_DIM, axis=1)).astype(BF16)

    @pl.when(i == 0)
    def _():
        diagonal(None)

    @pl.when(i > 0)
    def _():
        init = (jnp.full((tile, 1), -jnp.inf, F32), jnp.zeros((tile, FOX_BLOCK), F32))
        diagonal(lax.fori_loop(0, i, step, (init, init)))


def _fox(q, k, v, bsz, seq, tile):
    t = q.shape[0]
    nq = seq // tile
    pairs = FOX_HEADS // 2
    return pl.pallas_call(
        functools.partial(_fox_kernel, tile),
        grid=(bsz, pairs, nq),
        in_specs=[pl.BlockSpec((tile, 2 * FOX_BLOCK), lambda b, p, i: (b * nq + i, p)),
                  pl.BlockSpec((seq, 2 * FOX_BLOCK), lambda b, p, i: (b, p)),
                  pl.BlockSpec((seq, 2 * FOX_BLOCK), lambda b, p, i: (b, p))],
        out_specs=pl.BlockSpec((tile, LANES), lambda b, p, i: (b * nq + i, p)),
        out_shape=jax.ShapeDtypeStruct((t, MAIN_WIDTH), BF16),
        compiler_params=_params(3),
        name="fox_attention",
    )(q, k, v)


def _out_kernel(x_ref, main_ref, mq_ref, mk_ref, mv_ref, wo_main_ref, wo_mem_ref, o_ref):
    mq, mk, mv = mq_ref[...], mk_ref[...], mv_ref[...]
    head = lambda a, h: a[:, h * MEM_HEAD_DIM:(h + 1) * MEM_HEAD_DIM]
    scores = [_dot_nt(head(mq, h), head(mk, h)) * (MEM_HEAD_DIM ** -0.5) for h in range(MEM_HEADS)]
    base = x_ref[...] + _dot(main_ref[...], wo_main_ref[...])
    heads = []
    for h in range(MEM_HEADS):
        s = scores[h]
        p = jnp.exp(s - jnp.max(s, axis=-1, keepdims=True))
        l = jnp.sum(p, axis=-1, keepdims=True)
        heads.append(_dot(p.astype(BF16), head(mv, h)) / l)
    mem_o = jnp.concatenate(heads, axis=-1).astype(BF16)
    o_ref[...] = base + _dot(mem_o, wo_mem_ref[...])


def _out_proj(x, main, mq, mem_k, mem_v, wo_main, wo_mem, seq, tm):
    t = x.shape[0]
    tiles_per_seq = seq // tm
    n_mem = mem_k.shape[1]
    row = lambda n: pl.BlockSpec((tm, n), lambda i: (i, 0))
    mem = pl.BlockSpec((None, n_mem, MEM_WIDTH), lambda i: (i // tiles_per_seq, 0, 0))
    return pl.pallas_call(
        _out_kernel,
        grid=(t // tm,),
        in_specs=[row(D_MODEL), row(main.shape[1]), row(MEM_WIDTH), mem, mem, _full(wo_main.shape),
                  _full(wo_mem.shape)],
        out_specs=row(D_MODEL),
        out_shape=jax.ShapeDtypeStruct((t, D_MODEL), F32),
        compiler_params=_params(1),
        name="out_proj_mem_attn",
    )(x, main, mq, mem_k, mem_v, wo_main, wo_mem)


def _swiglu_acc(hn, wg_ref, wu_ref, wd_ref, acc_ref, row_scale=None):
    for j in range(D_FF // FF_CHUNK):
        cs = slice(j * FF_CHUNK, (j + 1) * FF_CHUNK)
        a = _silu(_dot(hn, wg_ref[:, cs])) * _dot(hn, wu_ref[:, cs])
        if row_scale is not None:
            a = a * row_scale
        acc_ref[...] += _dot(a.astype(BF16), wd_ref[cs, :])


def _ffn_kernel(x_ref, g_ref, wg_ref, wu_ref, wd_ref, o_ref):
    x = x_ref[...]
    o_ref[...] = x
    _swiglu_acc(_rms_normed(x, g_ref[...]).astype(BF16), wg_ref, wu_ref, wd_ref, o_ref)


def _ffn(x, gain, wg, wu, wd, tm, riders):
    t = x.shape[0]
    steps = t // tm
    row = pl.BlockSpec((tm, D_MODEL), lambda i: (i, 0))
    held = lambda shape: pl.BlockSpec(shape, lambda i: (0, 0), pipeline_mode=pl.Buffered(1))
    rider_specs, rider_shapes = _rider_blocks(riders, steps)
    out = pl.pallas_call(
        _riding(_ffn_kernel, 5, 1, len(riders)),
        grid=(steps,),
        in_specs=[row, _full((1, D_MODEL)), held(wg.shape), held(wu.shape), held(wd.shape)] + rider_specs,
        out_specs=[row] + rider_specs,
        out_shape=[jax.ShapeDtypeStruct((t, D_MODEL), F32)] + rider_shapes,
        compiler_params=_params(1),
        name="dense_swiglu",
    )(x, gain, wg, wu, wd, *riders)
    return out[0], out[1:]


ROW_TILE = D_MODEL // LANES
ROUTE_E1, ROUTE_E2, ROUTE_W1, ROUTE_W2, ROUTE_RANK1, ROUTE_RANK2 = range(6)


def _rows_to_tiles(dst_ref, rows):
    n = rows.shape[0]
    for j in range(ROW_TILE):
        dst_ref[pl.ds(j, n, stride=ROW_TILE), :] = rows[:, j * LANES:(j + 1) * LANES]


def _tiles_to_rows(src_ref, n):
    return jnp.concatenate([src_ref[pl.ds(j, n, stride=ROW_TILE), :] for j in range(ROW_TILE)], axis=1)


def _row_tile(ref, row):
    return ref.at[pl.ds(pl.multiple_of(row * ROW_TILE, ROW_TILE), ROW_TILE)]


def _router_kernel(x_ref, g_ref, wr_ref, route_ref, fields_ref, counts_ref, carry_ref):
    @pl.when(pl.program_id(0) == 0)
    def _():
        carry_ref[...] = jnp.zeros_like(carry_ref)

    hn = _rms_normed(x_ref[...], g_ref[...])
    logits = _dot_3pass(hn, wr_ref[...]).T[:N_EXPERTS, :]
    tm = logits.shape[1]
    expert = lax.broadcasted_iota(jnp.int32, logits.shape, 0)
    m1 = jnp.max(logits, axis=0, keepdims=True)
    i1 = jnp.min(jnp.where(logits == m1, expert, N_EXPERTS), axis=0, keepdims=True)
    rest = jnp.where(expert == i1, -jnp.inf, logits)
    m2 = jnp.max(rest, axis=0, keepdims=True)
    i2 = jnp.min(jnp.where(rest == m2, expert, N_EXPERTS), axis=0, keepdims=True)
    e2 = jnp.exp(m2 - m1)
    w1 = 1.0 / (1.0 + e2)
    chosen = jnp.where((expert == i1) | (expert == i2), 1.0, 0.0)
    token = lax.broadcasted_iota(jnp.int32, logits.shape, 1)
    run = chosen
    shift = 1
    while shift < tm:
        run = run + jnp.where(token >= shift, pltpu.roll(run, shift, axis=1), 0.0)
        shift *= 2
    base = carry_ref[...][:, :1] + (run - chosen)
    rank1 = jnp.sum(jnp.where(expert == i1, base, 0.0), axis=0, keepdims=True)
    rank2 = jnp.sum(jnp.where(expert == i2, base, 0.0), axis=0, keepdims=True)
    carry_ref[...] += jnp.sum(chosen, axis=1, keepdims=True)
    counts_ref[...] = carry_ref[...]
    fields = jnp.zeros_like(logits)
    for slot, val in ((ROUTE_E1, i1.astype(F32)), (ROUTE_E2, i2.astype(F32)), (ROUTE_W1, w1), (ROUTE_W2, e2 * w1),
                      (ROUTE_RANK1, rank1), (ROUTE_RANK2, rank2)):
        fields = jnp.where(expert == slot, val, fields)
    fields_ref[...] = fields
    route_ref[...] = jnp.concatenate([fields, jnp.zeros((LANES - N_EXPERTS, tm), F32)], axis=0).T


def _router(x, gain, wr, tm):
    t = x.shape[0]
    row = lambda n: pl.BlockSpec((tm, n), lambda i: (i, 0))
    n_fields = N_EXPERTS
    return pl.pallas_call(
        _router_kernel,
        grid=(t // tm,),
        in_specs=[row(D_MODEL), _full((1, D_MODEL)), _full(wr.shape)],
        out_specs=[row(LANES), pl.BlockSpec((None, n_fields, tm), lambda i: (i, 0, 0)), _full((N_EXPERTS, LANES))],
        out_shape=[jax.ShapeDtypeStruct((t, LANES), F32), jax.ShapeDtypeStruct((t // tm, n_fields, tm), F32),
                   jax.ShapeDtypeStruct((N_EXPERTS, LANES), F32)],
        scratch_shapes=[pltpu.VMEM((N_EXPERTS, LANES), F32)],
        compiler_params=_params(1),
        name="moe_router",
    )(x, gain, wr)


def _dispatch_kernel(tm, cnt_ref, pad_ref, off_ref, pos1_ref, pos2_ref, x_ref, g_ref, xs_ref, stage_ref, zero_ref,
                     sem, pad_sem):
    @pl.when(pl.program_id(0) == 0)
    def _():
        zero_ref[...] = jnp.zeros_like(zero_ref)
        for e in range(N_EXPERTS + 1):
            lo, hi = off_ref[e] + cnt_ref[e], off_ref[e] + pad_ref[e]

            def fill(r, c):
                pltpu.make_async_copy(zero_ref, _row_tile(xs_ref, r), pad_sem).start()
                return c

            def drain(r, c):
                pltpu.make_async_copy(zero_ref, _row_tile(xs_ref, r), pad_sem).wait()
                return c

            lax.fori_loop(lo, hi, fill, 0)
            lax.fori_loop(lo, hi, drain, 0)

    _rows_to_tiles(stage_ref, _rms_normed(x_ref[...], g_ref[...]))

    def issue(r, c):
        src = _row_tile(stage_ref, r)
        pltpu.make_async_copy(src, _row_tile(xs_ref, pos1_ref[0, r]), sem).start(priority=0)
        pltpu.make_async_copy(src, _row_tile(xs_ref, pos2_ref[0, r]), sem).start(priority=1)
        return c

    lax.fori_loop(0, tm, issue, 0, unroll=8)
    for _ in range(2):
        pltpu.make_async_copy(stage_ref, xs_ref.at[pl.ds(0, tm * ROW_TILE)], sem).wait()


def _dispatch(x, gain, pos1, pos2, counts, padded, offsets, n_rows, tm):
    t = x.shape[0]
    nt = t // tm
    smem_row = pl.BlockSpec((None, 1, tm), lambda i, *_: (i, 0, 0), memory_space=pltpu.SMEM)
    return pl.pallas_call(
        functools.partial(_dispatch_kernel, tm),
        grid_spec=pltpu.PrefetchScalarGridSpec(
            num_scalar_prefetch=3,
            grid=(nt,),
            in_specs=[smem_row, smem_row, pl.BlockSpec((tm, D_MODEL), lambda i, *_: (i, 0)),
                      pl.BlockSpec((1, D_MODEL), lambda i, *_: (0, 0))],
            out_specs=pl.BlockSpec(memory_space=pl.ANY),
            scratch_shapes=[pltpu.VMEM((tm * ROW_TILE, LANES), F32), pltpu.VMEM((ROW_TILE, LANES), F32),
                            pltpu.SemaphoreType.DMA, pltpu.SemaphoreType.DMA],
        ),
        out_shape=jax.ShapeDtypeStruct((n_rows * ROW_TILE, LANES), F32),
        compiler_params=_params(1),
        name="moe_dispatch",
    )(counts, padded, offsets, pos1.reshape(nt, 1, tm), pos2.reshape(nt, 1, tm), x, gain)


def _experts_kernel(tm, tile_expert_ref, n_used_ref, xs_ref, wg_ref, wu_ref, wd_ref, o_ref, acc_ref):
    acc_ref[...] = jnp.zeros_like(acc_ref)

    @pl.when(pl.program_id(0) < n_used_ref[0])
    def _():
        _swiglu_acc(_tiles_to_rows(xs_ref, tm).astype(BF16), wg_ref, wu_ref, wd_ref, acc_ref)

    _rows_to_tiles(o_ref, acc_ref[...])


def _experts(xs, tile_expert, n_used, wg, wu, wd, tm):
    n_tiles = xs.shape[0] // (tm * ROW_TILE)
    wspec = lambda shape: pl.BlockSpec(
        (None,) + shape, lambda i, te, nu: (te[jnp.minimum(i, nu[0] - 1)], 0, 0))
    rows = pl.BlockSpec((tm * ROW_TILE, LANES), lambda i, te, nu: (i, 0))
    return pl.pallas_call(
        functools.partial(_experts_kernel, tm),
        grid_spec=pltpu.PrefetchScalarGridSpec(
            num_scalar_prefetch=2,
            grid=(n_tiles,),
            in_specs=[rows, wspec((D_MODEL, D_FF)), wspec((D_MODEL, D_FF)), wspec((D_FF, D_MODEL))],
            out_specs=rows,
            scratch_shapes=[pltpu.VMEM((tm, D_MODEL), F32)],
        ),
        out_shape=jax.ShapeDtypeStruct(xs.shape, F32),
        compiler_params=_params(1),
        name="moe_experts",
    )(tile_expert, n_used, xs, wg, wu, wd)


def _combine_kernel(tm, n_tiles, pos1_ref, pos2_ref, next1_ref, next2_ref, x_ref, route_ref, gf_ref, y_ref,
                    o_ref, buf_ref, sems):
    i = pl.program_id(0)
    slot = i % 2

    def gather(p1_ref, p2_ref, slot):
        def issue(r, c):
            pltpu.make_async_copy(_row_tile(y_ref, p1_ref[0, r]), _row_tile(buf_ref.at[slot, 0], r),
                                  sems.at[slot]).start(priority=0)
            pltpu.make_async_copy(_row_tile(y_ref, p2_ref[0, r]), _row_tile(buf_ref.at[slot, 1], r),
                                  sems.at[slot]).start(priority=1)
            return c

        lax.fori_loop(0, tm, issue, 0, unroll=8)

    @pl.when(i == 0)
    def _():
        gather(pos1_ref, pos2_ref, 0)

    @pl.when(i + 1 < n_tiles)
    def _():
        gather(next1_ref, next2_ref, 1 - slot)

    route = route_ref[...]
    lane = lax.broadcasted_iota(jnp.int32, route.shape, 1)
    w1 = jnp.sum(jnp.where(lane == ROUTE_W1, route, 0.0), axis=-1, keepdims=True)
    w2 = jnp.sum(jnp.where(lane == ROUTE_W2, route, 0.0), axis=-1, keepdims=True)
    for k in range(2):
        pltpu.make_async_copy(y_ref.at[pl.ds(0, tm * ROW_TILE)], buf_ref.at[slot, k], sems.at[slot]).wait()
    y1, y2 = (_tiles_to_rows(buf_ref.at[slot, k], tm) for k in range(2))
    o_ref[...] = _rms_normed(x_ref[...] + w1 * y1 + w2 * y2, gf_ref[...])


def _combine(x, route, pos1, pos2, y, gain_final, tm):
    t = x.shape[0]
    nt = t // tm
    smem_row = lambda ahead: pl.BlockSpec((None, 1, tm), lambda i: (jnp.minimum(i + ahead, nt - 1), 0, 0),
                                          memory_space=pltpu.SMEM)
    row = lambda n: pl.BlockSpec((tm, n), lambda i: (i, 0))
    pos1, pos2 = pos1.reshape(nt, 1, tm), pos2.reshape(nt, 1, tm)
    return pl.pallas_call(
        functools.partial(_combine_kernel, tm, nt),
        grid=(nt,),
        in_specs=[smem_row(0), smem_row(0), smem_row(1), smem_row(1), row(D_MODEL), row(LANES),
                  _full((1, D_MODEL)), pl.BlockSpec(memory_space=pl.ANY)],
        out_specs=row(D_MODEL),
        out_shape=jax.ShapeDtypeStruct((t, D_MODEL), F32),
        scratch_shapes=[pltpu.VMEM((2, 2, tm * ROW_TILE, LANES), F32), pltpu.SemaphoreType.DMA((2,))],
        compiler_params=_params(1),
        name="moe_combine",
    )(pos1, pos2, pos1, pos2, x, route, gain_final, y)


def _moe(x, gain, gain_final, wr, wg, wu, wd, tm):
    t = x.shape[0]
    route, fields, counts = _router(x, gain, wr, tm)
    col = lambda c: fields[:, c, :].reshape(t).astype(jnp.int32)
    counts = counts[:, 0].astype(jnp.int32)
    padded = (counts + tm - 1) // tm * tm
    ends = jnp.cumsum(padded)
    offsets = ends - padded
    pos1 = offsets[col(ROUTE_E1)] + col(ROUTE_RANK1)
    pos2 = offsets[col(ROUTE_E2)] + col(ROUTE_RANK2)
    n_rows = 2 * t + N_EXPERTS * tm
    tile_start = jnp.arange(n_rows // tm, dtype=jnp.int32) * tm
    tile_expert = jnp.minimum(jnp.sum(tile_start[:, None] >= ends[None, :], axis=1), N_EXPERTS - 1).astype(jnp.int32)
    n_used = (ends[-1:] // tm).astype(jnp.int32)
    tail = lambda a, v: jnp.concatenate([a, v.astype(jnp.int32)])
    xs = _dispatch(x, gain, pos1, pos2, tail(counts, jnp.zeros((1,))), tail(padded, n_rows - ends[-1:]),
                   tail(offsets, ends[-1:]), n_rows, tm)
    y = _experts(xs, tile_expert, n_used, wg, wu, wd, tm)
    return _combine(x, route, pos1, pos2, y, gain_final, tm // 2)


def _pad_heads(w, heads, width, padded):
    lead = w.shape[:-1]
    w = w.reshape(lead + (heads, width))
    w = jnp.pad(w, [(0, 0)] * len(lead) + [(0, 0), (0, padded - width)])
    return w.reshape(lead + (heads * padded,))


def _pad_cols(w, n):
    return jnp.pad(w, [(0, 0)] * (w.ndim - 1) + [(0, n - w.shape[-1])])


def kernel(x, mem, norm_mix, norm_mem, norm_ffn, norm_kv, norm_final, w_in_a, w_gla_gate2, b_gla_gate2,
           gla_onorm, w_in_b, w_kv, b_forget, w_mem_kv, w_out, w_ff_gate, w_ff_up, w_ff_down, w_router,
           w_moe_gate, w_moe_up, w_moe_down):
    bsz, seq, _ = x.shape
    n_mem = mem.shape[1]
    t = bsz * seq
    tm = 512
    xt = x.reshape(t, D_MODEL)
    memt = mem.reshape(bsz * n_mem, D_MODEL)
    gain = lambda g: g.reshape(1, D_MODEL)
    kw = GLA_HEADS * GLA_DK

    def mem_kv(layer):
        wk, wv = w_mem_kv[layer, :, :MEM_WIDTH], w_mem_kv[layer, :, MEM_WIDTH:]
        mk, mv = _norm_proj(memt, gain(norm_mem[layer]), [wk.astype(BF16), wv.astype(BF16)], (1.0, 1.0),
                            n_mem, "mem_kv_proj")
        return mk.reshape(bsz, n_mem, MEM_WIDTH), mv.reshape(bsz, n_mem, MEM_WIDTH)

    wa = w_in_a[0]
    wq = _pad_heads(wa[:, :kw], GLA_HEADS, GLA_DK, GLA_DK_PAD).astype(BF16)
    wk = _pad_heads(wa[:, kw:2 * kw], GLA_HEADS, GLA_DK, GLA_DK_PAD).astype(BF16)
    off = 2 * kw
    wv = _pad_heads(wa[:, off:off + MAIN_WIDTH], GLA_HEADS, GLA_DV, GLA_DV_PAD).astype(BF16)
    off += MAIN_WIDTH
    wr = _pad_heads(wa[:, off:off + MAIN_WIDTH], GLA_HEADS, GLA_DV, GLA_DV_PAD).astype(BF16)
    off += MAIN_WIDTH
    wg1 = _pad_cols(wa[:, off:off + GLA_GATE_RANK], LANES).astype(BF16)
    off += GLA_GATE_RANK
    wmq = wa[:, off:off + MEM_WIDTH].astype(BF16)
    wg2 = jnp.pad(_pad_heads(w_gla_gate2[0], GLA_HEADS, GLA_DK, GLA_DK_PAD), ((0, LANES - GLA_GATE_RANK), (0, 0)))
    bg2 = _pad_heads(b_gla_gate2[0], GLA_HEADS, GLA_DK, GLA_DK_PAD).reshape(1, -1)
    onorm = _pad_cols(gla_onorm[0], GLA_DV_PAD).reshape(1, GLA_DV_PAD)

    (q, k, v, r, mq, la), dense_weights = _in_a(xt, gain(norm_mix[0]), wq, wk, wv, wr, wmq, wg1, wg2, bg2, 2 * tm,
                                                [w_ff_gate[0], w_ff_up[0]])
    main = _gla(q, k, v, r, la, onorm, bsz, seq, tm)
    mk, mv = mem_kv(0)
    wo = w_out[0]
    wo_main = jnp.pad(wo[:MAIN_WIDTH].reshape(GLA_HEADS, GLA_DV, D_MODEL),
                      ((0, 0), (0, GLA_DV_PAD - GLA_DV), (0, 0))).reshape(GLA_HEADS * GLA_DV_PAD, D_MODEL)
    xt = _out_proj(xt, main, mq, mk, mv, wo_main.astype(BF16), wo[MAIN_WIDTH:].astype(BF16), seq, 2 * tm)
    expert_weights = [w_moe_gate[0].reshape(N_EXPERTS * D_MODEL, D_FF), w_moe_up[0].reshape(N_EXPERTS * D_MODEL, D_FF),
                      w_moe_down[0].reshape(N_EXPERTS * D_FF, D_MODEL)]
    xt, expert_weights = _ffn(xt, gain(norm_ffn[0]), dense_weights[0], dense_weights[1], w_ff_down[0].astype(BF16),
                              tm, expert_weights)
    wg, wu, wd = (w.reshape(N_EXPERTS, -1, w.shape[-1]) for w in expert_weights)

    wf = _pad_cols(w_kv[:, 2 * MAIN_WIDTH:], LANES).astype(BF16)
    bf = _pad_cols(b_forget, LANES).reshape(1, LANES)
    k_sh, v_sh, c = _kv_proj(xt, gain(norm_kv), w_kv[:, :MAIN_WIDTH].astype(BF16),
                             w_kv[:, MAIN_WIDTH:2 * MAIN_WIDTH].astype(BF16), wf, bf, seq, tm)

    wb = w_in_b[0]
    qf, mq = _in_b(xt, gain(norm_mix[1]), c, wb[:, :MAIN_WIDTH].astype(BF16), wb[:, MAIN_WIDTH:].astype(BF16),
                   2 * tm)
    main = _fox(qf, k_sh, v_sh, bsz, seq, 1024)
    mk, mv = mem_kv(1)
    wo = w_out[1]
    xt = _out_proj(xt, main, mq, mk, mv, wo[:MAIN_WIDTH].astype(BF16), wo[MAIN_WIDTH:].astype(BF16), seq, 2 * tm)
    out = _moe(xt, gain(norm_ffn[1]), gain(norm_final), _pad_cols(w_router[0], LANES), wg, wu, wd, tm)
    return out.reshape(bsz, seq, D_MODEL)
```

```python
import functools

import jax
import jax.numpy as jnp
import numpy as np
from jax import lax
from jax.experimental import pallas as pl
from jax.experimental.pallas import tpu as pltpu

D_MODEL = 1024
EPS = 1e-6
GLA_CHUNK = 64
GLA_SUB = 16
MEM_HEADS = 4
MEM_HEAD_DIM = 64
MEM_WIDTH = MEM_HEADS * MEM_HEAD_DIM
MAIN_WIDTH = D_MODEL - MEM_WIDTH
GLA_HEADS = 4
GLA_DK = MAIN_WIDTH // 2 // GLA_HEADS
GLA_DV = MAIN_WIDTH // GLA_HEADS
GLA_DK_PAD = 128
GLA_DV_PAD = 256
GLA_GATE_RANK = 16
GLA_GATE_TAU = 16.0
FOX_HEADS = 12
FOX_HEAD_DIM = 64
D_FF = 2816
N_EXPERTS = 8
LANES = 128
FF_CHUNK = 256
VMEM_LIMIT = 56 * 1024 * 1024

BF16 = jnp.bfloat16
F32 = jnp.float32


def _params(n_axes, vmem=VMEM_LIMIT):
    return pltpu.CompilerParams(dimension_semantics=("arbitrary",) * n_axes, vmem_limit_bytes=vmem)


def _rms_normed(x, gain):
    ms = jnp.mean(x * x, axis=-1, keepdims=True)
    return x * lax.rsqrt(ms + EPS) * gain


def _log_sigmoid(z):
    return jnp.minimum(z, 0.0) - jnp.log(1.0 + jnp.exp(-jnp.abs(z)))


def _silu(z):
    return z / (1.0 + jnp.exp(-z))


def _dot(a, b):
    return jnp.dot(a, b, preferred_element_type=F32)


def _dot_nt(a, b):
    return lax.dot_general(a, b, (((1,), (1,)), ((), ())), preferred_element_type=F32)


def _dot_tn(a, b):
    return lax.dot_general(a, b, (((0,), (0,)), ((), ())), preferred_element_type=F32)


def _bf16_terms(v):
    hi = v.astype(BF16).astype(F32)
    mid = (v - hi).astype(BF16).astype(F32)
    return hi, mid, v - hi - mid


def _dot_3pass(a, b):
    a_hi, a_lo, _ = _bf16_terms(a)
    b_hi, b_lo, _ = _bf16_terms(b)
    a_hi, a_lo, b_hi, b_lo = (t.astype(BF16) for t in (a_hi, a_lo, b_hi, b_lo))
    return _dot(a_hi, b_hi) + _dot(a_hi, b_lo) + _dot(a_lo, b_hi)


def _cumsum_rows(x):
    n = x.shape[0]
    r = lax.broadcasted_iota(jnp.int32, (n, n), 0)
    c = lax.broadcasted_iota(jnp.int32, (n, n), 1)
    tril = jnp.where(c <= r, 1.0, 0.0).astype(BF16)
    hi, mid, lo = _bf16_terms(x)
    return _dot(tril, hi.astype(BF16)) + _dot(tril, mid.astype(BF16)) + _dot(tril, lo.astype(BF16))


def _full(shape):
    return pl.BlockSpec(shape, lambda *_: (0,) * len(shape))


BF16_SUBLANES = 16


def _riding(kernel_fn, n_in, n_out, n_riders):
    def body(*refs):
        ins, rest = refs[:n_in], refs[n_in:]
        rider_in, rest = rest[:n_riders], rest[n_riders:]
        outs, rest = rest[:n_out], rest[n_out:]
        rider_out, scratch = rest[:n_riders], rest[n_riders:]
        for src_ref, dst_ref in zip(rider_in, rider_out):
            dst_ref[...] = src_ref[...].astype(BF16)
        kernel_fn(*ins, *outs, *scratch)
    return body


def _rider_blocks(riders, steps):
    for a in riders:
        assert a.shape[0] % (steps * BF16_SUBLANES) == 0, (a.shape, steps)
    specs = [pl.BlockSpec((a.shape[0] // steps, a.shape[1]), lambda i: (i, 0)) for a in riders]
    return specs, [jax.ShapeDtypeStruct(a.shape, BF16) for a in riders]


def _in_a_kernel(x_ref, g_ref, wq_ref, wk_ref, wv_ref, wr_ref, wmq_ref, wg1_ref, wg2_ref, bg2_ref,
                 q_ref, k_ref, v_ref, r_ref, mq_ref, la_ref):
    hn = _rms_normed(x_ref[...], g_ref[...]).astype(BF16)
    q_ref[...] = (_dot(hn, wq_ref[...]) * (GLA_DK ** -0.5)).astype(BF16)
    k_ref[...] = _dot(hn, wk_ref[...]).astype(BF16)
    v_ref[...] = _dot(hn, wv_ref[...]).astype(BF16)
    r_ref[...] = _dot(hn, wr_ref[...]).astype(BF16)
    mq_ref[...] = _dot(hn, wmq_ref[...]).astype(BF16)
    g_hi, g_lo, _ = _bf16_terms(_dot(hn, wg1_ref[...]))
    packed = g_hi + pltpu.roll(g_hi, GLA_GATE_RANK, axis=1) + pltpu.roll(g_lo, 2 * GLA_GATE_RANK, axis=1)
    z = _dot(packed.astype(BF16), wg2_ref[...])
    la_ref[...] = _log_sigmoid(z + bg2_ref[...]) * (1.0 / GLA_GATE_TAU)


def _in_a(x, gain, wq, wk, wv, wr, wmq, wg1, wg2, bg2, tm, riders):
    t = x.shape[0]
    steps = t // tm
    row = lambda n: pl.BlockSpec((tm, n), lambda i: (i, 0))
    kw, vw = GLA_HEADS * GLA_DK_PAD, GLA_HEADS * GLA_DV_PAD
    rider_specs, rider_shapes = _rider_blocks(riders, steps)
    out = pl.pallas_call(
        _riding(_in_a_kernel, 10, 6, len(riders)),
        grid=(steps,),
        in_specs=[row(D_MODEL), _full((1, D_MODEL)), _full(wq.shape), _full(wk.shape), _full(wv.shape),
                  _full(wr.shape), _full(wmq.shape), _full(wg1.shape), _full(wg2.shape), _full(bg2.shape)]
                 + rider_specs,
        out_specs=[row(kw), row(kw), row(vw), row(vw), row(MEM_WIDTH), row(kw)] + rider_specs,
        out_shape=[jax.ShapeDtypeStruct((t, kw), BF16), jax.ShapeDtypeStruct((t, kw), BF16),
                   jax.ShapeDtypeStruct((t, vw), BF16), jax.ShapeDtypeStruct((t, vw), BF16),
                   jax.ShapeDtypeStruct((t, MEM_WIDTH), BF16), jax.ShapeDtypeStruct((t, kw), F32)] + rider_shapes,
        compiler_params=_params(1),
        name="in_proj_gla",
    )(x, gain, wq, wk, wv, wr, wmq, wg1, wg2, bg2, *riders)
    return out[:6], out[6:]


def _norm_proj_kernel(n_out, scales, x_ref, g_ref, *refs):
    w_refs, o_refs = refs[:n_out], refs[n_out:]
    hn = _rms_normed(x_ref[...], g_ref[...]).astype(BF16)
    for w_ref, o_ref, s in zip(w_refs, o_refs, scales):
        o = _dot(hn, w_ref[...])
        if s != 1.0:
            o = o * s
        o_ref[...] = o.astype(o_ref.dtype)


def _norm_proj(x, gain, weights, scales, tm, name):
    t = x.shape[0]
    row = lambda n: pl.BlockSpec((tm, n), lambda i: (i, 0))
    return pl.pallas_call(
        functools.partial(_norm_proj_kernel, len(weights), scales),
        grid=(t // tm,),
        in_specs=[row(D_MODEL), _full((1, D_MODEL))] + [_full(w.shape) for w in weights],
        out_specs=[row(w.shape[1]) for w in weights],
        out_shape=[jax.ShapeDtypeStruct((t, w.shape[1]), BF16) for w in weights],
        compiler_params=_params(1),
        name=name,
    )(x, gain, *weights)


FOX_BLOCK = LANES
FOX_WIDTH = FOX_HEADS * FOX_BLOCK
LOG2E = 1.4426950408889634
FOX_DIAG_BANDS = 4


FOX_ONE_LANE = 3 * FOX_HEADS


def _fox_bias_matrix(query_side):
    e = np.zeros((LANES, FOX_WIDTH), np.float32)
    for h in range(FOX_HEADS):
        base = h * FOX_BLOCK + FOX_HEAD_DIM
        for n in range(3):
            if query_side:
                e[n * FOX_HEADS + h, base + n] = 1.0
                e[FOX_ONE_LANE, base + 3 + n] = 1.0
            else:
                e[FOX_ONE_LANE, base + n] = 1.0
                e[n * FOX_HEADS + h, base + 3 + n] = -1.0
    return jnp.asarray(e, BF16)


def _pack_head_terms(x, fill):
    hi, mid, lo = _bf16_terms(x)
    lane = lax.broadcasted_iota(jnp.int32, x.shape, 1)
    packed = fill(lane)
    for n, term in ((2, lo), (1, mid), (0, hi)):
        shifted = pltpu.roll(term, n * FOX_HEADS, axis=1) if n else term
        packed = jnp.where((lane >= n * FOX_HEADS) & (lane < (n + 1) * FOX_HEADS), shifted, packed)
    return packed.astype(BF16)


def _cumsum_heads(x):
    n = x.shape[0]
    r = lax.broadcasted_iota(jnp.int32, (n, n), 0)
    c = lax.broadcasted_iota(jnp.int32, (n, n), 1)
    tril = jnp.where(c <= r, 1.0, 0.0).astype(BF16)
    sums = _dot(tril, _pack_head_terms(x, lambda lane: jnp.zeros(lane.shape, F32)))
    total = sums + pltpu.roll(sums, LANES - FOX_HEADS, axis=1) + pltpu.roll(sums, LANES - 2 * FOX_HEADS, axis=1)
    lane = lax.broadcasted_iota(jnp.int32, x.shape, 1)
    return jnp.where(lane < FOX_HEADS, total, 0.0)


def _fox_bias_lanes(c, place_ref):
    packed = _pack_head_terms(c * LOG2E, lambda lane: jnp.where(lane == FOX_ONE_LANE, 1.0, 0.0))
    placed = _dot(packed, place_ref[...])
    return [placed[:, h * FOX_BLOCK:(h + 1) * FOX_BLOCK] for h in range(FOX_HEADS)]


def _fox_blocks(dense, spare):
    lane = lax.broadcasted_iota(jnp.int32, (dense.shape[0], FOX_BLOCK), 1)
    blocks = []
    for h in range(FOX_HEADS):
        pair = dense[:, (h // 2) * FOX_BLOCK:(h // 2 + 1) * FOX_BLOCK]
        own = pair if h % 2 == 0 else pltpu.roll(pair, FOX_HEAD_DIM, axis=1)
        blocks.append(jnp.where(lane < FOX_HEAD_DIM, own, spare[h]))
    return jnp.concatenate(blocks, axis=-1)


def _kv_kernel(tiles_per_seq, x_ref, g_ref, wk_ref, wv_ref, wf_ref, bf_ref, place_ref, k_ref, v_ref, c_ref,
               carry_ref):
    @pl.when(pl.program_id(0) % tiles_per_seq == 0)
    def _():
        carry_ref[...] = jnp.zeros_like(carry_ref)

    hn = _rms_normed(x_ref[...], g_ref[...]).astype(BF16)
    log_f = _log_sigmoid(_dot(hn, wf_ref[...]) + bf_ref[...])
    tm = log_f.shape[0]
    c = carry_ref[...] + _cumsum_heads(log_f)
    carry_ref[...] = c[tm - 1:tm, :]
    c_ref[...] = c
    k_ref[...] = _fox_blocks(_dot(hn, wk_ref[...]), _fox_bias_lanes(c, place_ref)).astype(BF16)
    lane = lax.broadcasted_iota(jnp.int32, (tm, FOX_BLOCK), 1)
    row_sum_lane = jnp.where(lane == FOX_HEAD_DIM, 1.0, 0.0)
    v_ref[...] = _fox_blocks(_dot(hn, wv_ref[...]), [row_sum_lane] * FOX_HEADS).astype(BF16)


def _kv_proj(x, gain, wk, wv, wf, bf, seq, tm):
    t = x.shape[0]
    place = _fox_bias_matrix(query_side=False)
    row = lambda n: pl.BlockSpec((tm, n), lambda i: (i, 0))
    return pl.pallas_call(
        functools.partial(_kv_kernel, seq // tm),
        grid=(t // tm,),
        in_specs=[row(D_MODEL), _full((1, D_MODEL)), _full(wk.shape), _full(wv.shape), _full(wf.shape),
                  _full(bf.shape), _full(place.shape)],
        out_specs=[row(FOX_WIDTH), row(FOX_WIDTH), row(LANES)],
        out_shape=[jax.ShapeDtypeStruct((t, FOX_WIDTH), BF16), jax.ShapeDtypeStruct((t, FOX_WIDTH), BF16),
                   jax.ShapeDtypeStruct((t, LANES), F32)],
        scratch_shapes=[pltpu.VMEM((1, LANES), F32)],
        compiler_params=_params(1),
        name="kv_proj_fox",
    )(x, gain, wk, wv, wf, bf, place)

def _in_b_kernel(x_ref, g_ref, c_ref, wq_ref, wmq_ref, place_ref, q_ref, mq_ref):
    hn = _rms_normed(x_ref[...], g_ref[...]).astype(BF16)
    q = _dot(hn, wq_ref[...]) * (FOX_HEAD_DIM ** -0.5 * LOG2E)
    q_ref[...] = _fox_blocks(q, _fox_bias_lanes(c_ref[...], place_ref)).astype(BF16)
    mq_ref[...] = _dot(hn, wmq_ref[...]).astype(BF16)


def _in_b(x, gain, c, wq, wmq, tm):
    t = x.shape[0]
    place = _fox_bias_matrix(query_side=True)
    row = lambda n: pl.BlockSpec((tm, n), lambda i: (i, 0))
    return pl.pallas_call(
        _in_b_kernel,
        grid=(t // tm,),
        in_specs=[row(D_MODEL), _full((1, D_MODEL)), row(LANES), _full(wq.shape), _full(wmq.shape),
                  _full(place.shape)],
        out_specs=[row(FOX_WIDTH), row(MEM_WIDTH)],
        out_shape=[jax.ShapeDtypeStruct((t, FOX_WIDTH), BF16), jax.ShapeDtypeStruct((t, MEM_WIDTH), BF16)],
        compiler_params=_params(1),
        name="in_proj_fox",
    )(x, gain, c, wq, wmq, place)


def _gla_kernel(chunks, n_seq, q_ref, k_ref, v_ref, r_ref, la_ref, on_ref, o_ref, *st_refs):
    @pl.when(pl.program_id(1) == 0)
    def _():
        for st_ref in st_refs:
            st_ref[...] = jnp.zeros_like(st_ref)

    n_sub = GLA_CHUNK // GLA_SUB
    masks = []
    for i in range(n_sub):
        n_keys = GLA_SUB * (i + 1)
        rr = lax.broadcasted_iota(jnp.int32, (GLA_SUB, n_keys), 0) + GLA_SUB * i
        cc = lax.broadcasted_iota(jnp.int32, (GLA_SUB, n_keys), 1)
        masks.append(cc <= rr)

    def chunk_body(c, carry):
        r0 = pl.multiple_of(c * GLA_CHUNK, GLA_CHUNK)
        rows = pl.ds(r0, GLA_CHUNK)
        streams = [(s, h) for s in range(n_seq) for h in range(GLA_HEADS)]
        b_all = [_cumsum_rows(la_ref[s, rows, :]) for s in range(n_seq)]
        vals, new_states, o_inter, scores = {}, {}, {}, {}
        for s, h in streams:
            ks = slice(h * GLA_DK_PAD, (h + 1) * GLA_DK_PAD)
            vs = slice(h * GLA_DV_PAD, (h + 1) * GLA_DV_PAD)
            q = q_ref[s, rows, ks].astype(F32)
            k = k_ref[s, rows, ks].astype(F32)
            v = v_ref[s, rows, vs]
            b = b_all[s][:, ks]
            b_last = b[GLA_CHUNK - 1:GLA_CHUNK, :]
            st = st_refs[s * GLA_HEADS + h][...]
            o_inter[s, h] = _dot_nt((q * jnp.exp(b)).astype(BF16), st.astype(BF16))
            kdec = (k * jnp.exp(b_last - b)).astype(BF16)
            new_states[s, h] = st * jnp.exp(b_last) + _dot_tn(v, kdec)
            head_scores = []
            for i in range(n_sub):
                lo, hi = GLA_SUB * i, GLA_SUB * (i + 1)
                b_i = b[lo:hi, :]
                qd = q[lo:hi, :] * jnp.exp(b_i - b[lo - 1:lo, :]) if i else q[lo:hi, :] * jnp.exp(b_i)
                kd = k[:hi, :] * jnp.exp(b[lo - 1:lo, :] - b[:hi, :]) if i else k[:hi, :] * jnp.exp(-b[:hi, :])
                head_scores.append(_dot_nt(qd.astype(BF16), kd.astype(BF16)))
            scores[s, h] = head_scores
            vals[s, h] = v
        outs = {}
        for s, h in streams:
            vs = slice(h * GLA_DV_PAD, (h + 1) * GLA_DV_PAD)
            o_parts = [_dot(jnp.where(masks[i], scores[s, h][i], 0.0).astype(BF16),
                            vals[s, h][:GLA_SUB * (i + 1), :]) for i in range(n_sub)]
            o = o_inter[s, h] + jnp.concatenate(o_parts, axis=0)
            ms = jnp.sum(o * o, axis=-1, keepdims=True) * (1.0 / GLA_DV)
            y = o * lax.rsqrt(ms + EPS) * on_ref[...] * _silu(r_ref[s, rows, vs].astype(F32))
            outs[s, h] = y.astype(BF16)
        for s, h in streams:
            st_refs[s * GLA_HEADS + h][...] = new_states[s, h]
            o_ref[s, rows, h * GLA_DV_PAD:(h + 1) * GLA_DV_PAD] = outs[s, h]
        return carry

    lax.fori_loop(0, chunks, chunk_body, 0, unroll=4)


def _gla(q, k, v, r, la, onorm, bsz, seq, tm):
    t = q.shape[0]
    n_seq = 4 if bsz % 4 == 0 else 2 if bsz % 2 == 0 else 1
    kw, vw = GLA_HEADS * GLA_DK_PAD, GLA_HEADS * GLA_DV_PAD
    rows = lambda n: pl.BlockSpec((n_seq, tm, n), lambda b, i: (b, i, 0))
    per_seq = lambda a: a.reshape(bsz, seq, a.shape[-1])
    out = pl.pallas_call(
        functools.partial(_gla_kernel, tm // GLA_CHUNK, n_seq),
        grid=(bsz // n_seq, seq // tm),
        in_specs=[rows(kw), rows(kw), rows(vw), rows(vw), rows(kw),
                  pl.BlockSpec((1, GLA_DV_PAD), lambda b, i: (0, 0))],
        out_specs=rows(vw),
        out_shape=jax.ShapeDtypeStruct((bsz, seq, vw), BF16),
        scratch_shapes=[pltpu.VMEM((GLA_DV_PAD, GLA_DK_PAD), F32)] * (GLA_HEADS * n_seq),
        compiler_params=_params(2),
        name="gla_scan",
    )(per_seq(q), per_seq(k), per_seq(v), per_seq(r), per_seq(la), onorm)
    return out.reshape(t, vw)


def _fox_kernel(tile, q_ref, k_ref, v_ref, o_ref):
    i = pl.program_id(2)
    q = q_ref[...]
    qs = (q[:, :FOX_BLOCK], q[:, FOX_BLOCK:])
    blks = (slice(0, FOX_BLOCK), slice(FOX_BLOCK, 2 * FOX_BLOCK))

    def absorb(state, s, v):
        if state is None:
            m_new = jnp.max(s, axis=-1, keepdims=True)
            return m_new, _dot(jnp.exp2(s - m_new).astype(BF16), v)
        m, acc = state
        m_new = jnp.maximum(m, jnp.max(s, axis=-1, keepdims=True))
        p = jnp.exp2(s - m_new).astype(BF16)
        return m_new, jnp.exp2(m - m_new) * acc + _dot(p, v)

    def step(j, state):
        keys = pl.ds(pl.multiple_of(j * tile, tile), tile)
        return tuple(absorb(state[hh], _dot_nt(qs[hh], k_ref[keys, blks[hh]]), v_ref[keys, blks[hh]])
                     for hh in range(2))

    band = tile // FOX_DIAG_BANDS
    start = pl.multiple_of(i * tile, tile)
    bands = [(hh, r) for r in range(FOX_DIAG_BANDS) for hh in range(2)]
    band_rows = lambda r: slice(r * band, (r + 1) * band)
    band_keys = lambda r: pl.ds(start, (r + 1) * band)

    def diagonal(state):
        scores = {(hh, r): _dot_nt(qs[hh][band_rows(r)], k_ref[band_keys(r), blks[hh]]) for hh, r in bands}
        causal = (lax.broadcasted_iota(jnp.int32, (band, band), 1)
                  <= lax.broadcasted_iota(jnp.int32, (band, band), 0))
        lane = lax.broadcasted_iota(jnp.int32, (band, FOX_BLOCK), 1)
        for r in range(FOX_DIAG_BANDS):
            outs = []
            for hh in range(2):
                s = scores[hh, r]
                tail = jnp.where(causal, s[:, r * band:], -jnp.inf)
                s = jnp.concatenate([s[:, :r * band], tail], axis=1) if r else tail
                before = None if state is None else tuple(a[band_rows(r)] for a in state[hh])
                _, acc = absorb(before, s, v_ref[band_keys(r), blks[hh]])
                outs.append(acc / jnp.sum(jnp.where(lane == FOX_HEAD_DIM, acc, 0.0), axis=-1, keepdims=True))
            o_ref[band_rows(r), :] = jnp.where(lane < FOX_HEAD_DIM, outs[0],
                                               pltpu.roll(outs[1], FOX_HEAD_DIM, axis=1)).astype(BF16)

    @pl.when(i == 0)
    def _():
        diagonal(None)

    @pl.when(i > 0)
    def _():
        init = (jnp.full((tile, 1), -jnp.inf, F32), jnp.zeros((tile, FOX_BLOCK), F32))
        diagonal(lax.fori_loop(0, i, step, (init, init)))


def _fox(q, k, v, bsz, seq, tile):
    t = q.shape[0]
    nq = seq // tile
    pairs = FOX_HEADS // 2
    return pl.pallas_call(
        functools.partial(_fox_kernel, tile),
        grid=(bsz, pairs, nq),
        in_specs=[pl.BlockSpec((tile, 2 * FOX_BLOCK), lambda b, p, i: (b * nq + i, p)),
                  pl.BlockSpec((seq, 2 * FOX_BLOCK), lambda b, p, i: (b, p)),
                  pl.BlockSpec((seq, 2 * FOX_BLOCK), lambda b, p, i: (b, p))],
        out_specs=pl.BlockSpec((tile, LANES), lambda b, p, i: (b * nq + i, p)),
        out_shape=jax.ShapeDtypeStruct((t, MAIN_WIDTH), BF16),
        compiler_params=_params(3),
        name="fox_attention",
    )(q, k, v)


def _out_kernel(x_ref, main_ref, mq_ref, mk_ref, mv_ref, wo_main_ref, wo_mem_ref, o_ref):
    mq, mk, mv = mq_ref[...], mk_ref[...], mv_ref[...]
    head = lambda a, h: a[:, h * MEM_HEAD_DIM:(h + 1) * MEM_HEAD_DIM]
    scores = [_dot_nt(head(mq, h), head(mk, h)) * (MEM_HEAD_DIM ** -0.5) for h in range(MEM_HEADS)]
    base = x_ref[...] + _dot(main_ref[...], wo_main_ref[...])
    heads = []
    for h in range(MEM_HEADS):
        s = scores[h]
        p = jnp.exp(s - jnp.max(s, axis=-1, keepdims=True))
        l = jnp.sum(p, axis=-1, keepdims=True)
        heads.append(_dot(p.astype(BF16), head(mv, h)) / l)
    mem_o = jnp.concatenate(heads, axis=-1).astype(BF16)
    o_ref[...] = base + _dot(mem_o, wo_mem_ref[...])


def _out_proj(x, main, mq, mem_k, mem_v, wo_main, wo_mem, seq, tm):
    t = x.shape[0]
    tiles_per_seq = seq // tm
    n_mem = mem_k.shape[1]
    row = lambda n: pl.BlockSpec((tm, n), lambda i: (i, 0))
    mem = pl.BlockSpec((None, n_mem, MEM_WIDTH), lambda i: (i // tiles_per_seq, 0, 0))
    return pl.pallas_call(
        _out_kernel,
        grid=(t // tm,),
        in_specs=[row(D_MODEL), row(main.shape[1]), row(MEM_WIDTH), mem, mem, _full(wo_main.shape),
                  _full(wo_mem.shape)],
        out_specs=row(D_MODEL),
        out_shape=jax.ShapeDtypeStruct((t, D_MODEL), F32),
        compiler_params=_params(1),
        name="out_proj_mem_attn",
    )(x, main, mq, mem_k, mem_v, wo_main, wo_mem)


def _swiglu_acc(hn, wg_ref, wu_ref, wd_ref, acc_ref, row_scale=None):
    for j in range(D_FF // FF_CHUNK):
        cs = slice(j * FF_CHUNK, (j + 1) * FF_CHUNK)
        a = _silu(_dot(hn, wg_ref[:, cs])) * _dot(hn, wu_ref[:, cs])
        if row_scale is not None:
            a = a * row_scale
        acc_ref[...] += _dot(a.astype(BF16), wd_ref[cs, :])


def _ffn_kernel(x_ref, g_ref, wg_ref, wu_ref, wd_ref, o_ref):
    x = x_ref[...]
    o_ref[...] = x
    _swiglu_acc(_rms_normed(x, g_ref[...]).astype(BF16), wg_ref, wu_ref, wd_ref, o_ref)


def _ffn(x, gain, wg, wu, wd, tm, riders):
    t = x.shape[0]
    steps = t // tm
    row = pl.BlockSpec((tm, D_MODEL), lambda i: (i, 0))
    held = lambda shape: pl.BlockSpec(shape, lambda i: (0, 0), pipeline_mode=pl.Buffered(1))
    rider_specs, rider_shapes = _rider_blocks(riders, steps)
    out = pl.pallas_call(
        _riding(_ffn_kernel, 5, 1, len(riders)),
        grid=(steps,),
        in_specs=[row, _full((1, D_MODEL)), held(wg.shape), held(wu.shape), held(wd.shape)] + rider_specs,
        out_specs=[row] + rider_specs,
        out_shape=[jax.ShapeDtypeStruct((t, D_MODEL), F32)] + rider_shapes,
        compiler_params=_params(1),
        name="dense_swiglu",
    )(x, gain, wg, wu, wd, *riders)
    return out[0], out[1:]


ROW_TILE = D_MODEL // LANES
ROUTE_E1, ROUTE_E2, ROUTE_W1, ROUTE_W2, ROUTE_RANK1, ROUTE_RANK2 = range(6)


def _rows_to_tiles(dst_ref, rows):
    n = rows.shape[0]
    for j in range(ROW_TILE):
        dst_ref[pl.ds(j, n, stride=ROW_TILE), :] = rows[:, j * LANES:(j + 1) * LANES]


def _tiles_to_rows(src_ref, n):
    return jnp.concatenate([src_ref[pl.ds(j, n, stride=ROW_TILE), :] for j in range(ROW_TILE)], axis=1)


def _row_tile(ref, row):
    return ref.at[pl.ds(pl.multiple_of(row * ROW_TILE, ROW_TILE), ROW_TILE)]


def _router_kernel(x_ref, g_ref, wr_ref, route_ref, fields_ref, counts_ref, carry_ref):
    @pl.when(pl.program_id(0) == 0)
    def _():
        carry_ref[...] = jnp.zeros_like(carry_ref)

    hn = _rms_normed(x_ref[...], g_ref[...])
    logits = _dot_3pass(hn, wr_ref[...]).T[:N_EXPERTS, :]
    tm = logits.shape[1]
    expert = lax.broadcasted_iota(jnp.int32, logits.shape, 0)
    m1 = jnp.max(logits, axis=0, keepdims=True)
    i1 = jnp.min(jnp.where(logits == m1, expert, N_EXPERTS), axis=0, keepdims=True)
    rest = jnp.where(expert == i1, -jnp.inf, logits)
    m2 = jnp.max(rest, axis=0, keepdims=True)
    i2 = jnp.min(jnp.where(rest == m2, expert, N_EXPERTS), axis=0, keepdims=True)
    e2 = jnp.exp(m2 - m1)
    w1 = 1.0 / (1.0 + e2)
    chosen = jnp.where((expert == i1) | (expert == i2), 1.0, 0.0)
    token = lax.broadcasted_iota(jnp.int32, logits.shape, 1)
    run = chosen
    shift = 1
    while shift < tm:
        run = run + jnp.where(token >= shift, pltpu.roll(run, shift, axis=1), 0.0)
        shift *= 2
    base = carry_ref[...][:, :1] + (run - chosen)
    rank1 = jnp.sum(jnp.where(expert == i1, base, 0.0), axis=0, keepdims=True)
    rank2 = jnp.sum(jnp.where(expert == i2, base, 0.0), axis=0, keepdims=True)
    carry_ref[...] += jnp.sum(chosen, axis=1, keepdims=True)
    counts_ref[...] = carry_ref[...]
    fields = jnp.zeros_like(logits)
    for slot, val in ((ROUTE_E1, i1.astype(F32)), (ROUTE_E2, i2.astype(F32)), (ROUTE_W1, w1), (ROUTE_W2, e2 * w1),
                      (ROUTE_RANK1, rank1), (ROUTE_RANK2, rank2)):
        fields = jnp.where(expert == slot, val, fields)
    fields_ref[...] = fields
    route_ref[...] = jnp.concatenate([fields, jnp.zeros((LANES - N_EXPERTS, tm), F32)], axis=0).T


def _router(x, gain, wr, tm):
    t = x.shape[0]
    row = lambda n: pl.BlockSpec((tm, n), lambda i: (i, 0))
    n_fields = N_EXPERTS
    return pl.pallas_call(
        _router_kernel,
        grid=(t // tm,),
        in_specs=[row(D_MODEL), _full((1, D_MODEL)), _full(wr.shape)],
        out_specs=[row(LANES), pl.BlockSpec((None, n_fields, tm), lambda i: (i, 0, 0)), _full((N_EXPERTS, LANES))],
        out_shape=[jax.ShapeDtypeStruct((t, LANES), F32), jax.ShapeDtypeStruct((t // tm, n_fields, tm), F32),
                   jax.ShapeDtypeStruct((N_EXPERTS, LANES), F32)],
        scratch_shapes=[pltpu.VMEM((N_EXPERTS, LANES), F32)],
        compiler_params=_params(1),
        name="moe_router",
    )(x, gain, wr)


def _dispatch_kernel(tm, cnt_ref, pad_ref, off_ref, pos1_ref, pos2_ref, x_ref, g_ref, xs_ref, stage_ref, zero_ref,
                     sem, pad_sem):
    @pl.when(pl.program_id(0) == 0)
    def _():
        zero_ref[...] = jnp.zeros_like(zero_ref)
        for e in range(N_EXPERTS + 1):
            lo, hi = off_ref[e] + cnt_ref[e], off_ref[e] + pad_ref[e]

            def fill(r, c):
                pltpu.make_async_copy(zero_ref, _row_tile(xs_ref, r), pad_sem).start()
                return c

            def drain(r, c):
                pltpu.make_async_copy(zero_ref, _row_tile(xs_ref, r), pad_sem).wait()
                return c

            lax.fori_loop(lo, hi, fill, 0)
            lax.fori_loop(lo, hi, drain, 0)

    _rows_to_tiles(stage_ref, _rms_normed(x_ref[...], g_ref[...]))

    def issue(r, c):
        src = _row_tile(stage_ref, r)
        pltpu.make_async_copy(src, _row_tile(xs_ref, pos1_ref[0, r]), sem).start(priority=0)
        pltpu.make_async_copy(src, _row_tile(xs_ref, pos2_ref[0, r]), sem).start(priority=1)
        return c

    lax.fori_loop(0, tm, issue, 0, unroll=8)
    for _ in range(2):
        pltpu.make_async_copy(stage_ref, xs_ref.at[pl.ds(0, tm * ROW_TILE)], sem).wait()


def _dispatch(x, gain, pos1, pos2, counts, padded, offsets, n_rows, tm):
    t = x.shape[0]
    nt = t // tm
    smem_row = pl.BlockSpec((None, 1, tm), lambda i, *_: (i, 0, 0), memory_space=pltpu.SMEM)
    return pl.pallas_call(
        functools.partial(_dispatch_kernel, tm),
        grid_spec=pltpu.PrefetchScalarGridSpec(
            num_scalar_prefetch=3,
            grid=(nt,),
            in_specs=[smem_row, smem_row, pl.BlockSpec((tm, D_MODEL), lambda i, *_: (i, 0)),
                      pl.BlockSpec((1, D_MODEL), lambda i, *_: (0, 0))],
            out_specs=pl.BlockSpec(memory_space=pl.ANY),
            scratch_shapes=[pltpu.VMEM((tm * ROW_TILE, LANES), F32), pltpu.VMEM((ROW_TILE, LANES), F32),
                            pltpu.SemaphoreType.DMA, pltpu.SemaphoreType.DMA],
        ),
        out_shape=jax.ShapeDtypeStruct((n_rows * ROW_TILE, LANES), F32),
        compiler_params=_params(1),
        name="moe_dispatch",
    )(counts, padded, offsets, pos1.reshape(nt, 1, tm), pos2.reshape(nt, 1, tm), x, gain)


def _experts_kernel(tm, tile_expert_ref, n_used_ref, xs_ref, wg_ref, wu_ref, wd_ref, o_ref, acc_ref):
    acc_ref[...] = jnp.zeros_like(acc_ref)

    @pl.when(pl.program_id(0) < n_used_ref[0])
    def _():
        _swiglu_acc(_tiles_to_rows(xs_ref, tm).astype(BF16), wg_ref, wu_ref, wd_ref, acc_ref)

    _rows_to_tiles(o_ref, acc_ref[...])


def _experts(xs, tile_expert, n_used, wg, wu, wd, tm):
    n_tiles = xs.shape[0] // (tm * ROW_TILE)
    wspec = lambda shape: pl.BlockSpec(
        (None,) + shape, lambda i, te, nu: (te[jnp.minimum(i, nu[0] - 1)], 0, 0))
    rows = pl.BlockSpec((tm * ROW_TILE, LANES), lambda i, te, nu: (i, 0))
    return pl.pallas_call(
        functools.partial(_experts_kernel, tm),
        grid_spec=pltpu.PrefetchScalarGridSpec(
            num_scalar_prefetch=2,
            grid=(n_tiles,),
            in_specs=[rows, wspec((D_MODEL, D_FF)), wspec((D_MODEL, D_FF)), wspec((D_FF, D_MODEL))],
            out_specs=rows,
            scratch_shapes=[pltpu.VMEM((tm, D_MODEL), F32)],
        ),
        out_shape=jax.ShapeDtypeStruct(xs.shape, F32),
        compiler_params=_params(1),
        name="moe_experts",
    )(tile_expert, n_used, xs, wg, wu, wd)


def _combine_kernel(tm, n_tiles, pos1_ref, pos2_ref, next1_ref, next2_ref, x_ref, route_ref, gf_ref, y_ref,
                    o_ref, buf_ref, sems):
    i = pl.program_id(0)
    slot = i % 2

    def gather(p1_ref, p2_ref, slot):
        def issue(r, c):
            pltpu.make_async_copy(_row_tile(y_ref, p1_ref[0, r]), _row_tile(buf_ref.at[slot, 0], r),
                                  sems.at[slot]).start(priority=0)
            pltpu.make_async_copy(_row_tile(y_ref, p2_ref[0, r]), _row_tile(buf_ref.at[slot, 1], r),
                                  sems.at[slot]).start(priority=1)
            return c

        lax.fori_loop(0, tm, issue, 0, unroll=8)

    @pl.when(i == 0)
    def _():
        gather(pos1_ref, pos2_ref, 0)

    @pl.when(i + 1 < n_tiles)
    def _():
        gather(next1_ref, next2_ref, 1 - slot)

    route = route_ref[...]
    lane = lax.broadcasted_iota(jnp.int32, route.shape, 1)
    w1 = jnp.sum(jnp.where(lane == ROUTE_W1, route, 0.0), axis=-1, keepdims=True)
    w2 = jnp.sum(jnp.where(lane == ROUTE_W2, route, 0.0), axis=-1, keepdims=True)
    for k in range(2):
        pltpu.make_async_copy(y_ref.at[pl.ds(0, tm * ROW_TILE)], buf_ref.at[slot, k], sems.at[slot]).wait()
    y1, y2 = (_tiles_to_rows(buf_ref.at[slot, k], tm) for k in range(2))
    o_ref[...] = _rms_normed(x_ref[...] + w1 * y1 + w2 * y2, gf_ref[...])


def _combine(x, route, pos1, pos2, y, gain_final, tm):
    t = x.shape[0]
    nt = t // tm
    smem_row = lambda ahead: pl.BlockSpec((None, 1, tm), lambda i: (jnp.minimum(i + ahead, nt - 1), 0, 0),
                                          memory_space=pltpu.SMEM)
    row = lambda n: pl.BlockSpec((tm, n), lambda i: (i, 0))
    pos1, pos2 = pos1.reshape(nt, 1, tm), pos2.reshape(nt, 1, tm)
    return pl.pallas_call(
        functools.partial(_combine_kernel, tm, nt),
        grid=(nt,),
        in_specs=[smem_row(0), smem_row(0), smem_row(1), smem_row(1), row(D_MODEL), row(LANES),
                  _full((1, D_MODEL)), pl.BlockSpec(memory_space=pl.ANY)],
        out_specs=row(D_MODEL),
        out_shape=jax.ShapeDtypeStruct((t, D_MODEL), F32),
        scratch_shapes=[pltpu.VMEM((2, 2, tm * ROW_TILE, LANES), F32), pltpu.SemaphoreType.DMA((2,))],
        compiler_params=_params(1),
        name="moe_combine",
    )(pos1, pos2, pos1, pos2, x, route, gain_final, y)


def _moe(x, gain, gain_final, wr, wg, wu, wd, tm):
    t = x.shape[0]
    route, fields, counts = _router(x, gain, wr, tm)
    col = lambda c: fields[:, c, :].reshape(t).astype(jnp.int32)
    counts = counts[:, 0].astype(jnp.int32)
    padded = (counts + tm - 1) // tm * tm
    ends = jnp.cumsum(padded)
    offsets = ends - padded
    pos1 = offsets[col(ROUTE_E1)] + col(ROUTE_RANK1)
    pos2 = offsets[col(ROUTE_E2)] + col(ROUTE_RANK2)
    n_rows = 2 * t + N_EXPERTS * tm
    tile_start = jnp.arange(n_rows // tm, dtype=jnp.int32) * tm
    tile_expert = jnp.minimum(jnp.sum(tile_start[:, None] >= ends[None, :], axis=1), N_EXPERTS - 1).astype(jnp.int32)
    n_used = (ends[-1:] // tm).astype(jnp.int32)
    tail = lambda a, v: jnp.concatenate([a, v.astype(jnp.int32)])
    xs = _dispatch(x, gain, pos1, pos2, tail(counts, jnp.zeros((1,))), tail(padded, n_rows - ends[-1:]),
                   tail(offsets, ends[-1:]), n_rows, tm)
    y = _experts(xs, tile_expert, n_used, wg, wu, wd, tm)
    return _combine(x, route, pos1, pos2, y, gain_final, tm // 2)


def _pad_heads(w, heads, width, padded):
    lead = w.shape[:-1]
    w = w.reshape(lead + (heads, width))
    w = jnp.pad(w, [(0, 0)] * len(lead) + [(0, 0), (0, padded - width)])
    return w.reshape(lead + (heads * padded,))


def _pad_cols(w, n):
    return jnp.pad(w, [(0, 0)] * (w.ndim - 1) + [(0, n - w.shape[-1])])


def kernel(x, mem, norm_mix, norm_mem, norm_ffn, norm_kv, norm_final, w_in_a, w_gla_gate2, b_gla_gate2,
           gla_onorm, w_in_b, w_kv, b_forget, w_mem_kv, w_out, w_ff_gate, w_ff_up, w_ff_down, w_router,
           w_moe_gate, w_moe_up, w_moe_down):
    bsz, seq, _ = x.shape
    n_mem = mem.shape[1]
    t = bsz * seq
    tm = 512
    xt = x.reshape(t, D_MODEL)
    memt = mem.reshape(bsz * n_mem, D_MODEL)
    gain = lambda g: g.reshape(1, D_MODEL)
    kw = GLA_HEADS * GLA_DK

    def mem_kv(layer):
        wk, wv = w_mem_kv[layer, :, :MEM_WIDTH], w_mem_kv[layer, :, MEM_WIDTH:]
        mk, mv = _norm_proj(memt, gain(norm_mem[layer]), [wk.astype(BF16), wv.astype(BF16)], (1.0, 1.0),
                            n_mem, "mem_kv_proj")
        return mk.reshape(bsz, n_mem, MEM_WIDTH), mv.reshape(bsz, n_mem, MEM_WIDTH)

    wa = w_in_a[0]
    wq = _pad_heads(wa[:, :kw], GLA_HEADS, GLA_DK, GLA_DK_PAD).astype(BF16)
    wk = _pad_heads(wa[:, kw:2 * kw], GLA_HEADS, GLA_DK, GLA_DK_PAD).astype(BF16)
    off = 2 * kw
    wv = _pad_heads(wa[:, off:off + MAIN_WIDTH], GLA_HEADS, GLA_DV, GLA_DV_PAD).astype(BF16)
    off += MAIN_WIDTH
    wr = _pad_heads(wa[:, off:off + MAIN_WIDTH], GLA_HEADS, GLA_DV, GLA_DV_PAD).astype(BF16)
    off += MAIN_WIDTH
    wg1 = _pad_cols(wa[:, off:off + GLA_GATE_RANK], LANES).astype(BF16)
    off += GLA_GATE_RANK
    wmq = wa[:, off:off + MEM_WIDTH].astype(BF16)
    w2_hi, w2_lo, _ = _bf16_terms(_pad_heads(w_gla_gate2[0], GLA_HEADS, GLA_DK, GLA_DK_PAD))
    wg2 = jnp.pad(jnp.concatenate([w2_hi, w2_lo, w2_hi]), ((0, LANES - 3 * GLA_GATE_RANK), (0, 0))).astype(BF16)
    bg2 = _pad_heads(b_gla_gate2[0], GLA_HEADS, GLA_DK, GLA_DK_PAD).reshape(1, -1)
    onorm = _pad_cols(gla_onorm[0], GLA_DV_PAD).reshape(1, GLA_DV_PAD)

    (q, k, v, r, mq, la), dense_weights = _in_a(xt, gain(norm_mix[0]), wq, wk, wv, wr, wmq, wg1, wg2, bg2, 2 * tm,
                                                [w_ff_gate[0], w_ff_up[0]])
    main = _gla(q, k, v, r, la, onorm, bsz, seq, tm)
    mk, mv = mem_kv(0)
    wo = w_out[0]
    wo_main = jnp.pad(wo[:MAIN_WIDTH].reshape(GLA_HEADS, GLA_DV, D_MODEL),
                      ((0, 0), (0, GLA_DV_PAD - GLA_DV), (0, 0))).reshape(GLA_HEADS * GLA_DV_PAD, D_MODEL)
    xt = _out_proj(xt, main, mq, mk, mv, wo_main.astype(BF16), wo[MAIN_WIDTH:].astype(BF16), seq, 2 * tm)
    expert_weights = [w_moe_gate[0].reshape(N_EXPERTS * D_MODEL, D_FF), w_moe_up[0].reshape(N_EXPERTS * D_MODEL, D_FF),
                      w_moe_down[0].reshape(N_EXPERTS * D_FF, D_MODEL)]
    xt, expert_weights = _ffn(xt, gain(norm_ffn[0]), dense_weights[0], dense_weights[1], w_ff_down[0].astype(BF16),
                              tm, expert_weights)
    wg, wu, wd = (w.reshape(N_EXPERTS, -1, w.shape[-1]) for w in expert_weights)

    wf = _pad_cols(w_kv[:, 2 * MAIN_WIDTH:], LANES).astype(BF16)
    bf = _pad_cols(b_forget, LANES).reshape(1, LANES)
    k_sh, v_sh, c = _kv_proj(xt, gain(norm_kv), w_kv[:, :MAIN_WIDTH].astype(BF16),
                             w_kv[:, MAIN_WIDTH:2 * MAIN_WIDTH].astype(BF16), wf, bf, seq, tm)

    wb = w_in_b[0]
    qf, mq = _in_b(xt, gain(norm_mix[1]), c, wb[:, :MAIN_WIDTH].astype(BF16), wb[:, MAIN_WIDTH:].astype(BF16),
                   2 * tm)
    main = _fox(qf, k_sh, v_sh, bsz, seq, 1024)
    mk, mv = mem_kv(1)
    wo = w_out[1]
    xt = _out_proj(xt, main, mq, mk, mv, wo[:MAIN_WIDTH].astype(BF16), wo[MAIN_WIDTH:].astype(BF16), seq, 2 * tm)
    out = _moe(xt, gain(norm_ffn[1]), gain(norm_final), _pad_cols(w_router[0], LANES), wg, wu, wd, tm)
    return out.reshape(bsz, seq, D_MODEL)
```

```python
import functools

import jax
import jax.numpy as jnp
import numpy as np
from jax import lax
from jax.experimental import pallas as pl
from jax.experimental.pallas import tpu as pltpu

D_MODEL = 1024
EPS = 1e-6
GLA_CHUNK = 64
GLA_SUB = 16
MEM_HEADS = 4
MEM_HEAD_DIM = 64
MEM_WIDTH = MEM_HEADS * MEM_HEAD_DIM
MAIN_WIDTH = D_MODEL - MEM_WIDTH
GLA_HEADS = 4
GLA_DK = MAIN_WIDTH // 2 // GLA_HEADS
GLA_DV = MAIN_WIDTH // GLA_HEADS
GLA_DK_PAD = 128
GLA_DV_PAD = 256
GLA_GATE_RANK = 16
GLA_GATE_TAU = 16.0
FOX_HEADS = 12
FOX_HEAD_DIM = 64
D_FF = 2816
N_EXPERTS = 8
LANES = 128
FF_CHUNK = 256
VMEM_LIMIT = 56 * 1024 * 1024

BF16 = jnp.bfloat16
F32 = jnp.float32


def _params(n_axes, vmem=VMEM_LIMIT):
    return pltpu.CompilerParams(dimension_semantics=("arbitrary",) * n_axes, vmem_limit_bytes=vmem)


def _rms_normed(x, gain):
    ms = jnp.mean(x * x, axis=-1, keepdims=True)
    return x * lax.rsqrt(ms + EPS) * gain


def _log_sigmoid(z):
    return jnp.minimum(z, 0.0) - jnp.log(1.0 + jnp.exp(-jnp.abs(z)))


def _silu(z):
    return z / (1.0 + jnp.exp(-z))


def _dot(a, b):
    return jnp.dot(a, b, preferred_element_type=F32)


def _dot_nt(a, b):
    return lax.dot_general(a, b, (((1,), (1,)), ((), ())), preferred_element_type=F32)


def _dot_tn(a, b):
    return lax.dot_general(a, b, (((0,), (0,)), ((), ())), preferred_element_type=F32)


def _bf16_terms(v):
    hi = v.astype(BF16).astype(F32)
    mid = (v - hi).astype(BF16).astype(F32)
    return hi, mid, v - hi - mid


def _dot_3pass(a, b):
    a_hi, a_lo, _ = _bf16_terms(a)
    b_hi, b_lo, _ = _bf16_terms(b)
    a_hi, a_lo, b_hi, b_lo = (t.astype(BF16) for t in (a_hi, a_lo, b_hi, b_lo))
    return _dot(a_hi, b_hi) + _dot(a_hi, b_lo) + _dot(a_lo, b_hi)


def _cumsum_rows(x):
    n = x.shape[0]
    r = lax.broadcasted_iota(jnp.int32, (n, n), 0)
    c = lax.broadcasted_iota(jnp.int32, (n, n), 1)
    tril = jnp.where(c <= r, 1.0, 0.0).astype(BF16)
    hi, mid, lo = _bf16_terms(x)
    return _dot(tril, hi.astype(BF16)) + _dot(tril, mid.astype(BF16)) + _dot(tril, lo.astype(BF16))


def _full(shape):
    return pl.BlockSpec(shape, lambda *_: (0,) * len(shape))


BF16_SUBLANES = 16


def _riding(kernel_fn, n_in, n_out, n_riders):
    def body(*refs):
        ins, rest = refs[:n_in], refs[n_in:]
        rider_in, rest = rest[:n_riders], rest[n_riders:]
        outs, rest = rest[:n_out], rest[n_out:]
        rider_out, scratch = rest[:n_riders], rest[n_riders:]
        for src_ref, dst_ref in zip(rider_in, rider_out):
            dst_ref[...] = src_ref[...].astype(BF16)
        kernel_fn(*ins, *outs, *scratch)
    return body


def _rider_blocks(riders, steps):
    for a in riders:
        assert a.shape[0] % (steps * BF16_SUBLANES) == 0, (a.shape, steps)
    specs = [pl.BlockSpec((a.shape[0] // steps, a.shape[1]), lambda i: (i, 0)) for a in riders]
    return specs, [jax.ShapeDtypeStruct(a.shape, BF16) for a in riders]


def _in_a_kernel(x_ref, g_ref, wq_ref, wk_ref, wv_ref, wr_ref, wmq_ref, wg1_ref, wg2_ref, bg2_ref,
                 q_ref, k_ref, v_ref, r_ref, mq_ref, la_ref):
    hn = _rms_normed(x_ref[...], g_ref[...]).astype(BF16)
    q_ref[...] = (_dot(hn, wq_ref[...]) * (GLA_DK ** -0.5)).astype(BF16)
    k_ref[...] = _dot(hn, wk_ref[...]).astype(BF16)
    v_ref[...] = _dot(hn, wv_ref[...]).astype(BF16)
    r_ref[...] = _dot(hn, wr_ref[...]).astype(BF16)
    mq_ref[...] = _dot(hn, wmq_ref[...]).astype(BF16)
    g_hi, g_lo, _ = _bf16_terms(_dot(hn, wg1_ref[...]))
    packed = g_hi + pltpu.roll(g_hi, GLA_GATE_RANK, axis=1) + pltpu.roll(g_lo, 2 * GLA_GATE_RANK, axis=1)
    z = _dot(packed.astype(BF16), wg2_ref[...])
    la_ref[...] = _log_sigmoid(z + bg2_ref[...]) * (1.0 / GLA_GATE_TAU)


def _in_a(x, gain, wq, wk, wv, wr, wmq, wg1, wg2, bg2, tm, riders):
    t = x.shape[0]
    steps = t // tm
    row = lambda n: pl.BlockSpec((tm, n), lambda i: (i, 0))
    kw, vw = GLA_HEADS * GLA_DK_PAD, GLA_HEADS * GLA_DV_PAD
    rider_specs, rider_shapes = _rider_blocks(riders, steps)
    out = pl.pallas_call(
        _riding(_in_a_kernel, 10, 6, len(riders)),
        grid=(steps,),
        in_specs=[row(D_MODEL), _full((1, D_MODEL)), _full(wq.shape), _full(wk.shape), _full(wv.shape),
                  _full(wr.shape), _full(wmq.shape), _full(wg1.shape), _full(wg2.shape), _full(bg2.shape)]
                 + rider_specs,
        out_specs=[row(kw), row(kw), row(vw), row(vw), row(MEM_WIDTH), row(kw)] + rider_specs,
        out_shape=[jax.ShapeDtypeStruct((t, kw), BF16), jax.ShapeDtypeStruct((t, kw), BF16),
                   jax.ShapeDtypeStruct((t, vw), BF16), jax.ShapeDtypeStruct((t, vw), BF16),
                   jax.ShapeDtypeStruct((t, MEM_WIDTH), BF16), jax.ShapeDtypeStruct((t, kw), F32)] + rider_shapes,
        compiler_params=_params(1),
        name="in_proj_gla",
    )(x, gain, wq, wk, wv, wr, wmq, wg1, wg2, bg2, *riders)
    return out[:6], out[6:]


def _norm_proj_kernel(n_out, scales, x_ref, g_ref, *refs):
    w_refs, o_refs = refs[:n_out], refs[n_out:]
    hn = _rms_normed(x_ref[...], g_ref[...]).astype(BF16)
    for w_ref, o_ref, s in zip(w_refs, o_refs, scales):
        o = _dot(hn, w_ref[...])
        if s != 1.0:
            o = o * s
        o_ref[...] = o.astype(o_ref.dtype)


def _norm_proj(x, gain, weights, scales, tm, name):
    t = x.shape[0]
    row = lambda n: pl.BlockSpec((tm, n), lambda i: (i, 0))
    return pl.pallas_call(
        functools.partial(_norm_proj_kernel, len(weights), scales),
        grid=(t // tm,),
        in_specs=[row(D_MODEL), _full((1, D_MODEL))] + [_full(w.shape) for w in weights],
        out_specs=[row(w.shape[1]) for w in weights],
        out_shape=[jax.ShapeDtypeStruct((t, w.shape[1]), BF16) for w in weights],
        compiler_params=_params(1),
        name=name,
    )(x, gain, *weights)


FOX_BLOCK = LANES
FOX_WIDTH = FOX_HEADS * FOX_BLOCK
LOG2E = 1.4426950408889634
FOX_DIAG_BANDS = 4


FOX_ONE_LANE = 3 * FOX_HEADS


def _fox_bias_matrix(query_side):
    e = np.zeros((LANES, FOX_WIDTH), np.float32)
    for h in range(FOX_HEADS):
        base = h * FOX_BLOCK + FOX_HEAD_DIM
        for n in range(3):
            if query_side:
                e[n * FOX_HEADS + h, base + n] = 1.0
                e[FOX_ONE_LANE, base + 3 + n] = 1.0
            else:
                e[FOX_ONE_LANE, base + n] = 1.0
                e[n * FOX_HEADS + h, base + 3 + n] = -1.0
    return jnp.asarray(e, BF16)


def _pack_head_terms(x, fill):
    hi, mid, lo = _bf16_terms(x)
    lane = lax.broadcasted_iota(jnp.int32, x.shape, 1)
    packed = fill(lane)
    for n, term in ((2, lo), (1, mid), (0, hi)):
        shifted = pltpu.roll(term, n * FOX_HEADS, axis=1) if n else term
        packed = jnp.where((lane >= n * FOX_HEADS) & (lane < (n + 1) * FOX_HEADS), shifted, packed)
    return packed.astype(BF16)


def _cumsum_heads(x):
    n = x.shape[0]
    r = lax.broadcasted_iota(jnp.int32, (n, n), 0)
    c = lax.broadcasted_iota(jnp.int32, (n, n), 1)
    tril = jnp.where(c <= r, 1.0, 0.0).astype(BF16)
    sums = _dot(tril, _pack_head_terms(x, lambda lane: jnp.zeros(lane.shape, F32)))
    total = sums + pltpu.roll(sums, LANES - FOX_HEADS, axis=1) + pltpu.roll(sums, LANES - 2 * FOX_HEADS, axis=1)
    lane = lax.broadcasted_iota(jnp.int32, x.shape, 1)
    return jnp.where(lane < FOX_HEADS, total, 0.0)


def _fox_bias_lanes(c, place_ref):
    packed = _pack_head_terms(c * LOG2E, lambda lane: jnp.where(lane == FOX_ONE_LANE, 1.0, 0.0))
    placed = _dot(packed, place_ref[...])
    return [placed[:, h * FOX_BLOCK:(h + 1) * FOX_BLOCK] for h in range(FOX_HEADS)]


def _fox_blocks(dense, spare):
    lane = lax.broadcasted_iota(jnp.int32, (dense.shape[0], FOX_BLOCK), 1)
    blocks = []
    for h in range(FOX_HEADS):
        pair = dense[:, (h // 2) * FOX_BLOCK:(h // 2 + 1) * FOX_BLOCK]
        own = pair if h % 2 == 0 else pltpu.roll(pair, FOX_HEAD_DIM, axis=1)
        blocks.append(jnp.where(lane < FOX_HEAD_DIM, own, spare[h]))
    return jnp.concatenate(blocks, axis=-1)


def _kv_kernel(tiles_per_seq, x_ref, g_ref, wk_ref, wv_ref, wf_ref, bf_ref, place_ref, k_ref, v_ref, c_ref,
               carry_ref):
    @pl.when(pl.program_id(0) % tiles_per_seq == 0)
    def _():
        carry_ref[...] = jnp.zeros_like(carry_ref)

    hn = _rms_normed(x_ref[...], g_ref[...]).astype(BF16)
    log_f = _log_sigmoid(_dot(hn, wf_ref[...]) + bf_ref[...])
    tm = log_f.shape[0]
    c = carry_ref[...] + _cumsum_heads(log_f)
    carry_ref[...] = c[tm - 1:tm, :]
    c_ref[...] = c
    k_ref[...] = _fox_blocks(_dot(hn, wk_ref[...]), _fox_bias_lanes(c, place_ref)).astype(BF16)
    lane = lax.broadcasted_iota(jnp.int32, (tm, FOX_BLOCK), 1)
    row_sum_lane = jnp.where(lane == FOX_HEAD_DIM, 1.0, 0.0)
    v_ref[...] = _fox_blocks(_dot(hn, wv_ref[...]), [row_sum_lane] * FOX_HEADS).astype(BF16)


def _kv_proj(x, gain, wk, wv, wf, bf, seq, tm):
    t = x.shape[0]
    place = _fox_bias_matrix(query_side=False)
    row = lambda n: pl.BlockSpec((tm, n), lambda i: (i, 0))
    return pl.pallas_call(
        functools.partial(_kv_kernel, seq // tm),
        grid=(t // tm,),
        in_specs=[row(D_MODEL), _full((1, D_MODEL)), _full(wk.shape), _full(wv.shape), _full(wf.shape),
                  _full(bf.shape), _full(place.shape)],
        out_specs=[row(FOX_WIDTH), row(FOX_WIDTH), row(LANES)],
        out_shape=[jax.ShapeDtypeStruct((t, FOX_WIDTH), BF16), jax.ShapeDtypeStruct((t, FOX_WIDTH), BF16),
                   jax.ShapeDtypeStruct((t, LANES), F32)],
        scratch_shapes=[pltpu.VMEM((1, LANES), F32)],
        compiler_params=_params(1),
        name="kv_proj_fox",
    )(x, gain, wk, wv, wf, bf, place)

def _in_b_kernel(x_ref, g_ref, c_ref, wq_ref, wmq_ref, place_ref, q_ref, mq_ref):
    hn = _rms_normed(x_ref[...], g_ref[...]).astype(BF16)
    q = _dot(hn, wq_ref[...]) * (FOX_HEAD_DIM ** -0.5 * LOG2E)
    q_ref[...] = _fox_blocks(q, _fox_bias_lanes(c_ref[...], place_ref)).astype(BF16)
    mq_ref[...] = _dot(hn, wmq_ref[...]).astype(BF16)


def _in_b(x, gain, c, wq, wmq, tm):
    t = x.shape[0]
    place = _fox_bias_matrix(query_side=True)
    row = lambda n: pl.BlockSpec((tm, n), lambda i: (i, 0))
    return pl.pallas_call(
        _in_b_kernel,
        grid=(t // tm,),
        in_specs=[row(D_MODEL), _full((1, D_MODEL)), row(LANES), _full(wq.shape), _full(wmq.shape),
                  _full(place.shape)],
        out_specs=[row(FOX_WIDTH), row(MEM_WIDTH)],
        out_shape=[jax.ShapeDtypeStruct((t, FOX_WIDTH), BF16), jax.ShapeDtypeStruct((t, MEM_WIDTH), BF16)],
        compiler_params=_params(1),
        name="in_proj_fox",
    )(x, gain, c, wq, wmq, place)


def _gla_kernel(chunks, n_seq, q_ref, k_ref, v_ref, r_ref, la_ref, on_ref, o_ref, *st_refs):
    @pl.when(pl.program_id(1) == 0)
    def _():
        for st_ref in st_refs:
            st_ref[...] = jnp.zeros_like(st_ref)

    n_sub = GLA_CHUNK // GLA_SUB
    masks = []
    for i in range(n_sub):
        n_keys = GLA_SUB * (i + 1)
        rr = lax.broadcasted_iota(jnp.int32, (GLA_SUB, n_keys), 0) + GLA_SUB * i
        cc = lax.broadcasted_iota(jnp.int32, (GLA_SUB, n_keys), 1)
        masks.append(cc <= rr)

    def chunk_body(c, carry):
        r0 = pl.multiple_of(c * GLA_CHUNK, GLA_CHUNK)
        rows = pl.ds(r0, GLA_CHUNK)
        streams = [(s, h) for s in range(n_seq) for h in range(GLA_HEADS)]
        b_all = [_cumsum_rows(la_ref[s, rows, :]) for s in range(n_seq)]
        vals, new_states, o_inter, scores = {}, {}, {}, {}
        for s, h in streams:
            ks = slice(h * GLA_DK_PAD, (h + 1) * GLA_DK_PAD)
            vs = slice(h * GLA_DV_PAD, (h + 1) * GLA_DV_PAD)
            q = q_ref[s, rows, ks].astype(F32)
            k = k_ref[s, rows, ks].astype(F32)
            v = v_ref[s, rows, vs]
            b = b_all[s][:, ks]
            b_last = b[GLA_CHUNK - 1:GLA_CHUNK, :]
            st = st_refs[s * GLA_HEADS + h][...]
            o_inter[s, h] = _dot_nt((q * jnp.exp(b)).astype(BF16), st.astype(BF16))
            kdec = (k * jnp.exp(b_last - b)).astype(BF16)
            new_states[s, h] = st * jnp.exp(b_last) + _dot_tn(v, kdec)
            head_scores = []
            for i in range(n_sub):
                lo, hi = GLA_SUB * i, GLA_SUB * (i + 1)
                b_i = b[lo:hi, :]
                qd = q[lo:hi, :] * jnp.exp(b_i - b[lo - 1:lo, :]) if i else q[lo:hi, :] * jnp.exp(b_i)
                kd = k[:hi, :] * jnp.exp(b[lo - 1:lo, :] - b[:hi, :]) if i else k[:hi, :] * jnp.exp(-b[:hi, :])
                head_scores.append(_dot_nt(qd.astype(BF16), kd.astype(BF16)))
            scores[s, h] = head_scores
            vals[s, h] = v
        outs = {}
        for s, h in streams:
            vs = slice(h * GLA_DV_PAD, (h + 1) * GLA_DV_PAD)
            o_parts = [_dot(jnp.where(masks[i], scores[s, h][i], 0.0).astype(BF16),
                            vals[s, h][:GLA_SUB * (i + 1), :]) for i in range(n_sub)]
            o = o_inter[s, h] + jnp.concatenate(o_parts, axis=0)
            ms = jnp.sum(o * o, axis=-1, keepdims=True) * (1.0 / GLA_DV)
            y = o * lax.rsqrt(ms + EPS) * on_ref[...] * _silu(r_ref[s, rows, vs].astype(F32))
            outs[s, h] = y.astype(BF16)
        for s, h in streams:
            st_refs[s * GLA_HEADS + h][...] = new_states[s, h]
            o_ref[s, rows, h * GLA_DV_PAD:(h + 1) * GLA_DV_PAD] = outs[s, h]
        return carry

    lax.fori_loop(0, chunks, chunk_body, 0, unroll=4)


def _gla(q, k, v, r, la, onorm, bsz, seq, tm, riders):
    t = q.shape[0]
    n_seq = 4 if bsz % 4 == 0 else 2 if bsz % 2 == 0 else 1
    kw, vw = GLA_HEADS * GLA_DK_PAD, GLA_HEADS * GLA_DV_PAD
    tiles = seq // tm
    rows = lambda n: pl.BlockSpec((n_seq, tm, n), lambda b, i: (b, i, 0))
    per_seq = lambda a: a.reshape(bsz, seq, a.shape[-1])
    rider_specs, rider_shapes = _rider_blocks(riders, (bsz // n_seq) * tiles)
    rider_specs = [pl.BlockSpec(s.block_shape, lambda b, i: (b * tiles + i, 0)) for s in rider_specs]
    out = pl.pallas_call(
        _riding(functools.partial(_gla_kernel, tm // GLA_CHUNK, n_seq), 6, 1, len(riders)),
        grid=(bsz // n_seq, tiles),
        in_specs=[rows(kw), rows(kw), rows(vw), rows(vw), rows(kw),
                  pl.BlockSpec((1, GLA_DV_PAD), lambda b, i: (0, 0))] + rider_specs,
        out_specs=[rows(vw)] + rider_specs,
        out_shape=[jax.ShapeDtypeStruct((bsz, seq, vw), BF16)] + rider_shapes,
        scratch_shapes=[pltpu.VMEM((GLA_DV_PAD, GLA_DK_PAD), F32)] * (GLA_HEADS * n_seq),
        compiler_params=_params(2),
        name="gla_scan",
    )(per_seq(q), per_seq(k), per_seq(v), per_seq(r), per_seq(la), onorm, *riders)
    return out[0].reshape(t, vw), out[1:]


def _fox_kernel(tile, q_ref, k_ref, v_ref, o_ref):
    i = pl.program_id(2)
    q = q_ref[...]
    qs = (q[:, :FOX_BLOCK], q[:, FOX_BLOCK:])
    blks = (slice(0, FOX_BLOCK), slice(FOX_BLOCK, 2 * FOX_BLOCK))

    def absorb(state, s, v):
        if state is None:
            m_new = jnp.max(s, axis=-1, keepdims=True)
            return m_new, _dot(jnp.exp2(s - m_new).astype(BF16), v)
        m, acc = state
        m_new = jnp.maximum(m, jnp.max(s, axis=-1, keepdims=True))
        p = jnp.exp2(s - m_new).astype(BF16)
        return m_new, jnp.exp2(m - m_new) * acc + _dot(p, v)

    def step(j, state):
        keys = pl.ds(pl.multiple_of(j * tile, tile), tile)
        return tuple(absorb(state[hh], _dot_nt(qs[hh], k_ref[keys, blks[hh]]), v_ref[keys, blks[hh]])
                     for hh in range(2))

    band = tile // FOX_DIAG_BANDS
    start = pl.multiple_of(i * tile, tile)
    bands = [(hh, r) for r in range(FOX_DIAG_BANDS) for hh in range(2)]
    band_rows = lambda r: slice(r * band, (r + 1) * band)
    band_keys = lambda r: pl.ds(start, (r + 1) * band)

    def diagonal(state):
        scores = {(hh, r): _dot_nt(qs[hh][band_rows(r)], k_ref[band_keys(r), blks[hh]]) for hh, r in bands}
        causal = (lax.broadcasted_iota(jnp.int32, (band, band), 1)
                  <= lax.broadcasted_iota(jnp.int32, (band, band), 0))
        lane = lax.broadcasted_iota(jnp.int32, (band, FOX_BLOCK), 1)
        for r in range(FOX_DIAG_BANDS):
            outs = []
            for hh in range(2):
                s = scores[hh, r]
                tail = jnp.where(causal, s[:, r * band:], -jnp.inf)
                s = jnp.concatenate([s[:, :r * band], tail], axis=1) if r else tail
                before = None if state is None else tuple(a[band_rows(r)] for a in state[hh])
                _, acc = absorb(before, s, v_ref[band_keys(r), blks[hh]])
                outs.append(acc / jnp.sum(jnp.where(lane == FOX_HEAD_DIM, acc, 0.0), axis=-1, keepdims=True))
            o_ref[band_rows(r), :] = jnp.where(lane < FOX_HEAD_DIM, outs[0],
                                               pltpu.roll(outs[1], FOX_HEAD_DIM, axis=1)).astype(BF16)

    @pl.when(i == 0)
    def _():
        diagonal(None)

    @pl.when(i > 0)
    def _():
        init = (jnp.full((tile, 1), -jnp.inf, F32), jnp.zeros((tile, FOX_BLOCK), F32))
        diagonal(lax.fori_loop(0, i, step, (init, init)))


def _fox(q, k, v, bsz, seq, tile):
    t = q.shape[0]
    nq = seq // tile
    pairs = FOX_HEADS // 2
    return pl.pallas_call(
        functools.partial(_fox_kernel, tile),
        grid=(bsz, pairs, nq),
        in_specs=[pl.BlockSpec((tile, 2 * FOX_BLOCK), lambda b, p, i: (b * nq + i, p)),
                  pl.BlockSpec((seq, 2 * FOX_BLOCK), lambda b, p, i: (b, p)),
                  pl.BlockSpec((seq, 2 * FOX_BLOCK), lambda b, p, i: (b, p))],
        out_specs=pl.BlockSpec((tile, LANES), lambda b, p, i: (b * nq + i, p)),
        out_shape=jax.ShapeDtypeStruct((t, MAIN_WIDTH), BF16),
        compiler_params=_params(3),
        name="fox_attention",
    )(q, k, v)


def _out_kernel(x_ref, main_ref, mq_ref, mk_ref, mv_ref, wo_main_ref, wo_mem_ref, o_ref):
    mq, mk, mv = mq_ref[...], mk_ref[...], mv_ref[...]
    head = lambda a, h: a[:, h * MEM_HEAD_DIM:(h + 1) * MEM_HEAD_DIM]
    scores = [_dot_nt(head(mq, h), head(mk, h)) * (MEM_HEAD_DIM ** -0.5) for h in range(MEM_HEADS)]
    base = x_ref[...] + _dot(main_ref[...], wo_main_ref[...])
    heads = []
    for h in range(MEM_HEADS):
        s = scores[h]
        p = jnp.exp(s - jnp.max(s, axis=-1, keepdims=True))
        l = jnp.sum(p, axis=-1, keepdims=True)
        heads.append(_dot(p.astype(BF16), head(mv, h)) / l)
    mem_o = jnp.concatenate(heads, axis=-1).astype(BF16)
    o_ref[...] = base + _dot(mem_o, wo_mem_ref[...])


def _out_proj(x, main, mq, mem_k, mem_v, wo_main, wo_mem, seq, tm):
    t = x.shape[0]
    tiles_per_seq = seq // tm
    n_mem = mem_k.shape[1]
    row = lambda n: pl.BlockSpec((tm, n), lambda i: (i, 0))
    mem = pl.BlockSpec((None, n_mem, MEM_WIDTH), lambda i: (i // tiles_per_seq, 0, 0))
    return pl.pallas_call(
        _out_kernel,
        grid=(t // tm,),
        in_specs=[row(D_MODEL), row(main.shape[1]), row(MEM_WIDTH), mem, mem, _full(wo_main.shape),
                  _full(wo_mem.shape)],
        out_specs=row(D_MODEL),
        out_shape=jax.ShapeDtypeStruct((t, D_MODEL), F32),
        compiler_params=_params(1),
        name="out_proj_mem_attn",
    )(x, main, mq, mem_k, mem_v, wo_main, wo_mem)


def _swiglu_acc(hn, wg_ref, wu_ref, wd_ref, acc_ref, row_scale=None):
    for j in range(D_FF // FF_CHUNK):
        cs = slice(j * FF_CHUNK, (j + 1) * FF_CHUNK)
        a = _silu(_dot(hn, wg_ref[:, cs])) * _dot(hn, wu_ref[:, cs])
        if row_scale is not None:
            a = a * row_scale
        acc_ref[...] += _dot(a.astype(BF16), wd_ref[cs, :])


def _ffn_kernel(x_ref, g_ref, wg_ref, wu_ref, wd_ref, o_ref):
    x = x_ref[...]
    o_ref[...] = x
    _swiglu_acc(_rms_normed(x, g_ref[...]).astype(BF16), wg_ref, wu_ref, wd_ref, o_ref)


def _ffn(x, gain, wg, wu, wd, tm, riders):
    t = x.shape[0]
    steps = t // tm
    row = pl.BlockSpec((tm, D_MODEL), lambda i: (i, 0))
    held = lambda shape: pl.BlockSpec(shape, lambda i: (0, 0), pipeline_mode=pl.Buffered(1))
    rider_specs, rider_shapes = _rider_blocks(riders, steps)
    out = pl.pallas_call(
        _riding(_ffn_kernel, 5, 1, len(riders)),
        grid=(steps,),
        in_specs=[row, _full((1, D_MODEL)), held(wg.shape), held(wu.shape), held(wd.shape)] + rider_specs,
        out_specs=[row] + rider_specs,
        out_shape=[jax.ShapeDtypeStruct((t, D_MODEL), F32)] + rider_shapes,
        compiler_params=_params(1),
        name="dense_swiglu",
    )(x, gain, wg, wu, wd, *riders)
    return out[0], out[1:]


ROW_TILE = D_MODEL // LANES
ROUTE_E1, ROUTE_E2, ROUTE_W1, ROUTE_W2, ROUTE_RANK1, ROUTE_RANK2 = range(6)


def _rows_to_tiles(dst_ref, rows):
    n = rows.shape[0]
    for j in range(ROW_TILE):
        dst_ref[pl.ds(j, n, stride=ROW_TILE), :] = rows[:, j * LANES:(j + 1) * LANES]


def _tiles_to_rows(src_ref, n):
    return jnp.concatenate([src_ref[pl.ds(j, n, stride=ROW_TILE), :] for j in range(ROW_TILE)], axis=1)


def _row_tile(ref, row):
    return ref.at[pl.ds(pl.multiple_of(row * ROW_TILE, ROW_TILE), ROW_TILE)]


def _router_kernel(x_ref, g_ref, wr_ref, route_ref, fields_ref, counts_ref, carry_ref):
    @pl.when(pl.program_id(0) == 0)
    def _():
        carry_ref[...] = jnp.zeros_like(carry_ref)

    hn = _rms_normed(x_ref[...], g_ref[...])
    logits = _dot_3pass(hn, wr_ref[...]).T[:N_EXPERTS, :]
    tm = logits.shape[1]
    expert = lax.broadcasted_iota(jnp.int32, logits.shape, 0)
    m1 = jnp.max(logits, axis=0, keepdims=True)
    i1 = jnp.min(jnp.where(logits == m1, expert, N_EXPERTS), axis=0, keepdims=True)
    rest = jnp.where(expert == i1, -jnp.inf, logits)
    m2 = jnp.max(rest, axis=0, keepdims=True)
    i2 = jnp.min(jnp.where(rest == m2, expert, N_EXPERTS), axis=0, keepdims=True)
    e2 = jnp.exp(m2 - m1)
    w1 = 1.0 / (1.0 + e2)
    chosen = jnp.where((expert == i1) | (expert == i2), 1.0, 0.0)
    token = lax.broadcasted_iota(jnp.int32, logits.shape, 1)
    run = chosen
    shift = 1
    while shift < tm:
        run = run + jnp.where(token >= shift, pltpu.roll(run, shift, axis=1), 0.0)
        shift *= 2
    base = carry_ref[...][:, :1] + (run - chosen)
    rank1 = jnp.sum(jnp.where(expert == i1, base, 0.0), axis=0, keepdims=True)
    rank2 = jnp.sum(jnp.where(expert == i2, base, 0.0), axis=0, keepdims=True)
    carry_ref[...] += jnp.sum(chosen, axis=1, keepdims=True)
    counts_ref[...] = carry_ref[...]
    fields = jnp.zeros_like(logits)
    for slot, val in ((ROUTE_E1, i1.astype(F32)), (ROUTE_E2, i2.astype(F32)), (ROUTE_W1, w1), (ROUTE_W2, e2 * w1),
                      (ROUTE_RANK1, rank1), (ROUTE_RANK2, rank2)):
        fields = jnp.where(expert == slot, val, fields)
    fields_ref[...] = fields
    route_ref[...] = jnp.concatenate([fields, jnp.zeros((LANES - N_EXPERTS, tm), F32)], axis=0).T


def _router(x, gain, wr, tm):
    t = x.shape[0]
    row = lambda n: pl.BlockSpec((tm, n), lambda i: (i, 0))
    n_fields = N_EXPERTS
    return pl.pallas_call(
        _router_kernel,
        grid=(t // tm,),
        in_specs=[row(D_MODEL), _full((1, D_MODEL)), _full(wr.shape)],
        out_specs=[row(LANES), pl.BlockSpec((None, n_fields, tm), lambda i: (i, 0, 0)), _full((N_EXPERTS, LANES))],
        out_shape=[jax.ShapeDtypeStruct((t, LANES), F32), jax.ShapeDtypeStruct((t // tm, n_fields, tm), F32),
                   jax.ShapeDtypeStruct((N_EXPERTS, LANES), F32)],
        scratch_shapes=[pltpu.VMEM((N_EXPERTS, LANES), F32)],
        compiler_params=_params(1),
        name="moe_router",
    )(x, gain, wr)


def _dispatch_kernel(tm, cnt_ref, pad_ref, off_ref, pos1_ref, pos2_ref, x_ref, g_ref, xs_ref, stage_ref, zero_ref,
                     sem, pad_sem):
    @pl.when(pl.program_id(0) == 0)
    def _():
        zero_ref[...] = jnp.zeros_like(zero_ref)
        for e in range(N_EXPERTS + 1):
            lo, hi = off_ref[e] + cnt_ref[e], off_ref[e] + pad_ref[e]

            def fill(r, c):
                pltpu.make_async_copy(zero_ref, _row_tile(xs_ref, r), pad_sem).start()
                return c

            def drain(r, c):
                pltpu.make_async_copy(zero_ref, _row_tile(xs_ref, r), pad_sem).wait()
                return c

            lax.fori_loop(lo, hi, fill, 0)
            lax.fori_loop(lo, hi, drain, 0)

    _rows_to_tiles(stage_ref, _rms_normed(x_ref[...], g_ref[...]))

    def issue(r, c):
        src = _row_tile(stage_ref, r)
        pltpu.make_async_copy(src, _row_tile(xs_ref, pos1_ref[0, r]), sem).start(priority=0)
        pltpu.make_async_copy(src, _row_tile(xs_ref, pos2_ref[0, r]), sem).start(priority=1)
        return c

    lax.fori_loop(0, tm, issue, 0, unroll=8)
    for _ in range(2):
        pltpu.make_async_copy(stage_ref, xs_ref.at[pl.ds(0, tm * ROW_TILE)], sem).wait()


def _dispatch(x, gain, pos1, pos2, counts, padded, offsets, n_rows, tm):
    t = x.shape[0]
    nt = t // tm
    smem_row = pl.BlockSpec((None, 1, tm), lambda i, *_: (i, 0, 0), memory_space=pltpu.SMEM)
    return pl.pallas_call(
        functools.partial(_dispatch_kernel, tm),
        grid_spec=pltpu.PrefetchScalarGridSpec(
            num_scalar_prefetch=3,
            grid=(nt,),
            in_specs=[smem_row, smem_row, pl.BlockSpec((tm, D_MODEL), lambda i, *_: (i, 0)),
                      pl.BlockSpec((1, D_MODEL), lambda i, *_: (0, 0))],
            out_specs=pl.BlockSpec(memory_space=pl.ANY),
            scratch_shapes=[pltpu.VMEM((tm * ROW_TILE, LANES), F32), pltpu.VMEM((ROW_TILE, LANES), F32),
                            pltpu.SemaphoreType.DMA, pltpu.SemaphoreType.DMA],
        ),
        out_shape=jax.ShapeDtypeStruct((n_rows * ROW_TILE, LANES), F32),
        compiler_params=_params(1),
        name="moe_dispatch",
    )(counts, padded, offsets, pos1.reshape(nt, 1, tm), pos2.reshape(nt, 1, tm), x, gain)


def _experts_kernel(tm, tile_expert_ref, n_used_ref, xs_ref, wg_ref, wu_ref, wd_ref, o_ref, acc_ref):
    acc_ref[...] = jnp.zeros_like(acc_ref)

    @pl.when(pl.program_id(0) < n_used_ref[0])
    def _():
        _swiglu_acc(_tiles_to_rows(xs_ref, tm).astype(BF16), wg_ref, wu_ref, wd_ref, acc_ref)

    _rows_to_tiles(o_ref, acc_ref[...])


def _experts(xs, tile_expert, n_used, wg, wu, wd, tm):
    n_tiles = xs.shape[0] // (tm * ROW_TILE)
    wspec = lambda shape: pl.BlockSpec(
        (None,) + shape, lambda i, te, nu: (te[jnp.minimum(i, nu[0] - 1)], 0, 0))
    rows = pl.BlockSpec((tm * ROW_TILE, LANES), lambda i, te, nu: (i, 0))
    return pl.pallas_call(
        functools.partial(_experts_kernel, tm),
        grid_spec=pltpu.PrefetchScalarGridSpec(
            num_scalar_prefetch=2,
            grid=(n_tiles,),
            in_specs=[rows, wspec((D_MODEL, D_FF)), wspec((D_MODEL, D_FF)), wspec((D_FF, D_MODEL))],
            out_specs=rows,
            scratch_shapes=[pltpu.VMEM((tm, D_MODEL), F32)],
        ),
        out_shape=jax.ShapeDtypeStruct(xs.shape, F32),
        compiler_params=_params(1),
        name="moe_experts",
    )(tile_expert, n_used, xs, wg, wu, wd)


def _combine_kernel(tm, n_tiles, pos1_ref, pos2_ref, next1_ref, next2_ref, x_ref, route_ref, gf_ref, y_ref,
                    o_ref, buf_ref, sems):
    i = pl.program_id(0)
    slot = i % 2

    def gather(p1_ref, p2_ref, slot):
        def issue(r, c):
            pltpu.make_async_copy(_row_tile(y_ref, p1_ref[0, r]), _row_tile(buf_ref.at[slot, 0], r),
                                  sems.at[slot]).start(priority=0)
            pltpu.make_async_copy(_row_tile(y_ref, p2_ref[0, r]), _row_tile(buf_ref.at[slot, 1], r),
                                  sems.at[slot]).start(priority=1)
            return c

        lax.fori_loop(0, tm, issue, 0, unroll=8)

    @pl.when(i == 0)
    def _():
        gather(pos1_ref, pos2_ref, 0)

    @pl.when(i + 1 < n_tiles)
    def _():
        gather(next1_ref, next2_ref, 1 - slot)

    route = route_ref[...]
    lane = lax.broadcasted_iota(jnp.int32, route.shape, 1)
    w1 = jnp.sum(jnp.where(lane == ROUTE_W1, route, 0.0), axis=-1, keepdims=True)
    w2 = jnp.sum(jnp.where(lane == ROUTE_W2, route, 0.0), axis=-1, keepdims=True)
    for k in range(2):
        pltpu.make_async_copy(y_ref.at[pl.ds(0, tm * ROW_TILE)], buf_ref.at[slot, k], sems.at[slot]).wait()
    y1, y2 = (_tiles_to_rows(buf_ref.at[slot, k], tm) for k in range(2))
    o_ref[...] = _rms_normed(x_ref[...] + w1 * y1 + w2 * y2, gf_ref[...])


def _combine(x, route, pos1, pos2, y, gain_final, tm):
    t = x.shape[0]
    nt = t // tm
    smem_row = lambda ahead: pl.BlockSpec((None, 1, tm), lambda i: (jnp.minimum(i + ahead, nt - 1), 0, 0),
                                          memory_space=pltpu.SMEM)
    row = lambda n: pl.BlockSpec((tm, n), lambda i: (i, 0))
    pos1, pos2 = pos1.reshape(nt, 1, tm), pos2.reshape(nt, 1, tm)
    return pl.pallas_call(
        functools.partial(_combine_kernel, tm, nt),
        grid=(nt,),
        in_specs=[smem_row(0), smem_row(0), smem_row(1), smem_row(1), row(D_MODEL), row(LANES),
                  _full((1, D_MODEL)), pl.BlockSpec(memory_space=pl.ANY)],
        out_specs=row(D_MODEL),
        out_shape=jax.ShapeDtypeStruct((t, D_MODEL), F32),
        scratch_shapes=[pltpu.VMEM((2, 2, tm * ROW_TILE, LANES), F32), pltpu.SemaphoreType.DMA((2,))],
        compiler_params=_params(1),
        name="moe_combine",
    )(pos1, pos2, pos1, pos2, x, route, gain_final, y)


def _moe(x, gain, gain_final, wr, wg, wu, wd, tm):
    t = x.shape[0]
    route, fields, counts = _router(x, gain, wr, tm)
    col = lambda c: fields[:, c, :].reshape(t).astype(jnp.int32)
    counts = counts[:, 0].astype(jnp.int32)
    padded = (counts + tm - 1) // tm * tm
    ends = jnp.cumsum(padded)
    offsets = ends - padded
    pos1 = offsets[col(ROUTE_E1)] + col(ROUTE_RANK1)
    pos2 = offsets[col(ROUTE_E2)] + col(ROUTE_RANK2)
    n_rows = 2 * t + N_EXPERTS * tm
    tile_start = jnp.arange(n_rows // tm, dtype=jnp.int32) * tm
    tile_expert = jnp.minimum(jnp.sum(tile_start[:, None] >= ends[None, :], axis=1), N_EXPERTS - 1).astype(jnp.int32)
    n_used = (ends[-1:] // tm).astype(jnp.int32)
    tail = lambda a, v: jnp.concatenate([a, v.astype(jnp.int32)])
    xs = _dispatch(x, gain, pos1, pos2, tail(counts, jnp.zeros((1,))), tail(padded, n_rows - ends[-1:]),
                   tail(offsets, ends[-1:]), n_rows, tm)
    y = _experts(xs, tile_expert, n_used, wg, wu, wd, tm)
    return _combine(x, route, pos1, pos2, y, gain_final, tm // 2)


def _pad_heads(w, heads, width, padded):
    lead = w.shape[:-1]
    w = w.reshape(lead + (heads, width))
    w = jnp.pad(w, [(0, 0)] * len(lead) + [(0, 0), (0, padded - width)])
    return w.reshape(lead + (heads * padded,))


def _pad_cols(w, n):
    return jnp.pad(w, [(0, 0)] * (w.ndim - 1) + [(0, n - w.shape[-1])])


def kernel(x, mem, norm_mix, norm_mem, norm_ffn, norm_kv, norm_final, w_in_a, w_gla_gate2, b_gla_gate2,
           gla_onorm, w_in_b, w_kv, b_forget, w_mem_kv, w_out, w_ff_gate, w_ff_up, w_ff_down, w_router,
           w_moe_gate, w_moe_up, w_moe_down):
    bsz, seq, _ = x.shape
    n_mem = mem.shape[1]
    t = bsz * seq
    tm = 512
    xt = x.reshape(t, D_MODEL)
    memt = mem.reshape(bsz * n_mem, D_MODEL)
    gain = lambda g: g.reshape(1, D_MODEL)
    kw = GLA_HEADS * GLA_DK

    def mem_kv(layer):
        wk, wv = w_mem_kv[layer, :, :MEM_WIDTH], w_mem_kv[layer, :, MEM_WIDTH:]
        mk, mv = _norm_proj(memt, gain(norm_mem[layer]), [wk.astype(BF16), wv.astype(BF16)], (1.0, 1.0),
                            n_mem, "mem_kv_proj")
        return mk.reshape(bsz, n_mem, MEM_WIDTH), mv.reshape(bsz, n_mem, MEM_WIDTH)

    wa = w_in_a[0]
    wq = _pad_heads(wa[:, :kw], GLA_HEADS, GLA_DK, GLA_DK_PAD).astype(BF16)
    wk = _pad_heads(wa[:, kw:2 * kw], GLA_HEADS, GLA_DK, GLA_DK_PAD).astype(BF16)
    off = 2 * kw
    wv = _pad_heads(wa[:, off:off + MAIN_WIDTH], GLA_HEADS, GLA_DV, GLA_DV_PAD).astype(BF16)
    off += MAIN_WIDTH
    wr = _pad_heads(wa[:, off:off + MAIN_WIDTH], GLA_HEADS, GLA_DV, GLA_DV_PAD).astype(BF16)
    off += MAIN_WIDTH
    wg1 = _pad_cols(wa[:, off:off + GLA_GATE_RANK], LANES).astype(BF16)
    off += GLA_GATE_RANK
    wmq = wa[:, off:off + MEM_WIDTH].astype(BF16)
    w2_hi, w2_lo, _ = _bf16_terms(_pad_heads(w_gla_gate2[0], GLA_HEADS, GLA_DK, GLA_DK_PAD))
    wg2 = jnp.pad(jnp.concatenate([w2_hi, w2_lo, w2_hi]), ((0, LANES - 3 * GLA_GATE_RANK), (0, 0))).astype(BF16)
    bg2 = _pad_heads(b_gla_gate2[0], GLA_HEADS, GLA_DK, GLA_DK_PAD).reshape(1, -1)
    onorm = _pad_cols(gla_onorm[0], GLA_DV_PAD).reshape(1, GLA_DV_PAD)

    (q, k, v, r, mq, la), dense_weights = _in_a(xt, gain(norm_mix[0]), wq, wk, wv, wr, wmq, wg1, wg2, bg2, 2 * tm,
                                                [w_ff_gate[0], w_ff_up[0]])
    main, (w_down,) = _gla(q, k, v, r, la, onorm, bsz, seq, tm, [w_ff_down[0]])
    mk, mv = mem_kv(0)
    wo = w_out[0]
    wo_main = jnp.pad(wo[:MAIN_WIDTH].reshape(GLA_HEADS, GLA_DV, D_MODEL),
                      ((0, 0), (0, GLA_DV_PAD - GLA_DV), (0, 0))).reshape(GLA_HEADS * GLA_DV_PAD, D_MODEL)
    xt = _out_proj(xt, main, mq, mk, mv, wo_main.astype(BF16), wo[MAIN_WIDTH:].astype(BF16), seq, 2 * tm)
    expert_weights = [w_moe_gate[0].reshape(N_EXPERTS * D_MODEL, D_FF), w_moe_up[0].reshape(N_EXPERTS * D_MODEL, D_FF),
                      w_moe_down[0].reshape(N_EXPERTS * D_FF, D_MODEL)]
    xt, expert_weights = _ffn(xt, gain(norm_ffn[0]), dense_weights[0], dense_weights[1], w_down,
                              tm, expert_weights)
    wg, wu, wd = (w.reshape(N_EXPERTS, -1, w.shape[-1]) for w in expert_weights)

    wf = _pad_cols(w_kv[:, 2 * MAIN_WIDTH:], LANES).astype(BF16)
    bf = _pad_cols(b_forget, LANES).reshape(1, LANES)
    k_sh, v_sh, c = _kv_proj(xt, gain(norm_kv), w_kv[:, :MAIN_WIDTH].astype(BF16),
                             w_kv[:, MAIN_WIDTH:2 * MAIN_WIDTH].astype(BF16), wf, bf, seq, tm)

    wb = w_in_b[0]
    qf, mq = _in_b(xt, gain(norm_mix[1]), c, wb[:, :MAIN_WIDTH].astype(BF16), wb[:, MAIN_WIDTH:].astype(BF16),
                   2 * tm)
    main = _fox(qf, k_sh, v_sh, bsz, seq, 1024)
    mk, mv = mem_kv(1)
    wo = w_out[1]
    xt = _out_proj(xt, main, mq, mk, mv, wo[:MAIN_WIDTH].astype(BF16), wo[MAIN_WIDTH:].astype(BF16), seq, 2 * tm)
    out = _moe(xt, gain(norm_ffn[1]), gain(norm_final), _pad_cols(w_router[0], LANES), wg, wu, wd, tm)
    return out.reshape(bsz, seq, D_MODEL)
```
